```python
import jax, jax.numpy as jnp
from jax import lax
import numpy as np

D_MODEL = 1024
BATCH = 32
SEQ = 256
DEPTH = 1
DEC_BATCH = 8
DEC_SEQ = 4096
PAST_LEN = 512

GRID_W = 64
RWKV_WIDTH = D_MODEL // 2
HEAD_SIZE = 64
RWKV_HEADS = RWKV_WIDTH // HEAD_SIZE
CONV_WIDTH = D_MODEL - RWKV_WIDTH
CONV_GROUPS = 8
DECAY_LORA = 64
AAA_LORA = 64
GATE_LORA = 128
D_FF = 4 * D_MODEL
RWKV_SPLITS = (RWKV_WIDTH, RWKV_WIDTH, RWKV_WIDTH, DECAY_LORA, DECAY_LORA, AAA_LORA, AAA_LORA, GATE_LORA)
RWKV_OFFSETS = [int(o) for o in np.cumsum(RWKV_SPLITS)[:-1]]
RWKV_COLS = int(sum(RWKV_SPLITS))
IN_COLS = RWKV_COLS + 3 * CONV_WIDTH
NORM_EPS = 1e-6
GN_EPS = HEAD_SIZE * 1e-5

kernel_name = "hybrid_rwkv7_shortconv_diffusion_step"


def _rmsnorm(x, g):
    xf = x.astype(jnp.float32)
    y = xf * lax.rsqrt(jnp.mean(xf * xf, axis=-1, keepdims=True) + NORM_EPS)
    return (y * g.astype(jnp.float32)).astype(x.dtype)


def _neighbours(z, rows):
    b, l, ch = z.shape
    zg = z.reshape(b, 1, l, ch) if rows is None else z.reshape(b, rows, l // rows, ch)
    n = zg.shape[2]
    zp = jnp.pad(zg, ((0, 0), (0, 0), (1, 1), (0, 0)))
    prev = zp[:, :, :n].reshape(b, l, ch)
    nxt = zp[:, :, 2:].reshape(b, l, ch)
    return prev, nxt


def _rwkv_direction(k, wl, al, w0, w_w2, a0, w_a2, k_a):
    w = -jax.nn.softplus(-(w0 + jnp.tanh(wl) @ w_w2)) - 0.5
    decay = jnp.exp(-jnp.exp(w))
    a = jax.nn.sigmoid(a0 + al @ w_a2)
    k_dir = k * (1.0 + (a - 1.0) * k_a)
    return decay, a, k_dir


def _wkv7_scan(s0, r, w, k, v, kk, a, reverse):
    def step(s, inp):
        r_t, w_t, k_t, v_t, kk_t, a_t = inp
        sa = jnp.einsum('bhij,bhj->bhi', s, kk_t)
        s = (s * w_t[:, :, None, :]
             - sa[..., None] * (kk_t * a_t)[:, :, None, :]
             + v_t[..., None] * k_t[:, :, None, :])
        y = jnp.einsum('bhij,bhj->bhi', s, r_t)
        return s, y
    xs = tuple(jnp.swapaxes(t, 0, 1) for t in (r, w, k, v, kk, a))
    s, ys = lax.scan(step, s0, xs, reverse=reverse)
    return s, jnp.swapaxes(ys, 0, 1)


def _layer(x, mod, s_f0, s_b0, rows, norm1_g, w_in, mu_shift, w0_f, w_w2_f, w0_b, w_w2_b,
           a0_f, w_a2_f, a0_b, w_a2_b, w_g2, k_k, k_a, r_k, ln_x_w, ln_x_b,
           conv_w, conv_gain, w_out, norm2_g, w_ff1, w_ff2):
    b, l, _ = x.shape
    f32 = jnp.float32
    shift_a, scale_a, gate_a, shift_f, scale_f, gate_f = jnp.split(mod[:, None, :], 6, axis=-1)

    h = _rmsnorm(x, norm1_g) * (1 + scale_a) + shift_a
    z = h @ w_in
    z_rwkv = z[..., :RWKV_COLS].astype(f32)
    z_conv = z[..., RWKV_COLS:]

    prev, nxt = _neighbours(z_rwkv, rows)
    z_rwkv = z_rwkv + mu_shift * (0.5 * (prev + nxt) - z_rwkv)
    r, k, v, wl_f, wl_b, al_f, al_b, gl = jnp.split(z_rwkv, RWKV_OFFSETS, axis=-1)
    heads = lambda t: t.reshape(b, l, RWKV_HEADS, HEAD_SIZE)
    kk = heads(k * k_k)
    kk = kk * lax.rsqrt(jnp.sum(kk * kk, axis=-1, keepdims=True) + 1e-12)
    dec_f, a_f, k_f = _rwkv_direction(k, wl_f, al_f, w0_f, w_w2_f, a0_f, w_a2_f, k_a)
    dec_b, a_b, k_b = _rwkv_direction(k, wl_b, al_b, w0_b, w_w2_b, a0_b, w_a2_b, k_a)
    rh, vh, k_fh, k_bh = heads(r), heads(v), heads(k_f), heads(k_b)
    s_f, y_f = _wkv7_scan(s_f0.astype(f32), rh, heads(dec_f), k_fh, vh, kk, heads(a_f), reverse=False)
    s_b, y_b = _wkv7_scan(s_b0.astype(f32), rh, heads(dec_b), k_bh, vh, kk, heads(a_b), reverse=True)
    y = y_f + y_b
    y_mu = jnp.mean(y, axis=-1, keepdims=True)
    y_var = jnp.mean(jnp.square(y - y_mu), axis=-1, keepdims=True)
    yn = ((y - y_mu) * lax.rsqrt(y_var + GN_EPS)).reshape(b, l, RWKV_WIDTH) * ln_x_w + ln_x_b
    bonus = jnp.sum(rh * (k_fh + k_bh) * r_k, axis=-1, keepdims=True) * vh
    g = jax.nn.sigmoid(gl) @ w_g2
    o_rwkv = (yn + bonus.reshape(b, l, RWKV_WIDTH)) * g

    gate_b, gate_c, hc = jnp.split(z_conv, 3, axis=-1)
    u = gate_c * hc
    up, un = _neighbours(u, rows)
    o_conv = gate_b * (conv_w[0] * up + conv_w[1] * u + conv_w[2] * un)
    o_conv = _rmsnorm(o_conv, conv_gain)

    mix = jnp.concatenate([o_rwkv.astype(x.dtype), o_conv.astype(x.dtype)], axis=-1) @ w_out
    x = (x + gate_a * mix).astype(x.dtype)

    h = _rmsnorm(x, norm2_g) * (1 + scale_f) + shift_f
    f = jnp.square(jax.nn.relu(h @ w_ff1)) @ w_ff2
    x = (x + gate_f * f).astype(x.dtype)
    return x, s_f, s_b


def setup_inputs(seed: int = 0) -> dict:
    key = jax.random.key(seed)
    ks = iter(jax.random.split(key, 40))
    nrm = lambda shape, scale: scale * jax.random.normal(next(ks), shape, jnp.float32)
    L, D = DEPTH, D_MODEL
    st = (DEC_BATCH, L, RWKV_HEADS, HEAD_SIZE, HEAD_SIZE)
    return {
        "x_prompt": nrm((BATCH, SEQ, D), 1.0),
        "x_sample": nrm((DEC_BATCH, DEC_SEQ, D), 1.0),
        "c": nrm((DEC_BATCH, D), 1.0),
        "state_rwkv_fwd": nrm(st, 0.5),
        "state_rwkv_bwd": nrm(st, 0.5),
        "c_ctx": nrm((D,), 1.0),
        "w_mod": nrm((L, D, 6 * D), 0.5 * D ** -0.5),
        "b_mod": nrm((L, 6 * D), 0.02),
        "norm1_g": 1.0 + nrm((L, D), 0.02),
        "w_in": nrm((L, D, IN_COLS), D ** -0.5),
        "mu_shift": jax.random.uniform(next(ks), (L, RWKV_COLS), jnp.float32),
        "w0_f": -1.0 + nrm((L, RWKV_WIDTH), 0.5),
        "w_w2_f": nrm((L, DECAY_LORA, RWKV_WIDTH), 0.1 * DECAY_LORA ** -0.5),
        "w0_b": -1.0 + nrm((L, RWKV_WIDTH), 0.5),
        "w_w2_b": nrm((L, DECAY_LORA, RWKV_WIDTH), 0.1 * DECAY_LORA ** -0.5),
        "a0_f": nrm((L, RWKV_WIDTH), 0.1),
        "w_a2_f": nrm((L, AAA_LORA, RWKV_WIDTH), 0.5 * AAA_LORA ** -0.5),
        "a0_b": nrm((L, RWKV_WIDTH), 0.1),
        "w_a2_b": nrm((L, AAA_LORA, RWKV_WIDTH), 0.5 * AAA_LORA ** -0.5),
        "w_g2": nrm((L, GATE_LORA, RWKV_WIDTH), GATE_LORA ** -0.5),
        "k_k": 0.85 + nrm((L, RWKV_WIDTH), 0.02),
        "k_a": 1.0 + nrm((L, RWKV_WIDTH), 0.02),
        "r_k": nrm((L, RWKV_HEADS, HEAD_SIZE), 0.1),
        "ln_x_w": 1.0 + nrm((L, RWKV_WIDTH), 0.02),
        "ln_x_b": nrm((L, RWKV_WIDTH), 0.02),
        "conv_w": nrm((L, 3, CONV_WIDTH), 3 ** -0.5),
        "conv_gain": 1.0 + nrm((L, CONV_WIDTH), 0.02),
        "w_out": nrm((L, D, D), D ** -0.5),
        "norm2_g": 1.0 + nrm((L, D), 0.02),
        "w_ff1": nrm((L, D, D_FF), D ** -0.5),
        "w_ff2": nrm((L, D_FF, D), D_FF ** -0.5),
        "final_g": 1.0 + nrm((D,), 0.02),
    }


def reference(x_prompt, x_sample, c, state_rwkv_fwd, state_rwkv_bwd, c_ctx, w_mod, b_mod,
              norm1_g, w_in, mu_shift, w0_f, w_w2_f, w0_b, w_w2_b, a0_f, w_a2_f, a0_b, w_a2_b,
              w_g2, k_k, k_a, r_k, ln_x_w, ln_x_b, conv_w, conv_gain, w_out, norm2_g,
              w_ff1, w_ff2, final_g):
    rows = x_sample.shape[1] // GRID_W
    b_ctx = x_prompt.shape[0]
    zero_state = jnp.zeros((b_ctx, RWKV_HEADS, HEAD_SIZE, HEAD_SIZE), jnp.float32)
    x_ctx, x_lat = x_prompt, x_sample
    new_f, new_b = [], []
    for i in range(DEPTH):
        lw = dict(norm1_g=norm1_g[i], w_in=w_in[i], mu_shift=mu_shift[i],
                  w0_f=w0_f[i], w_w2_f=w_w2_f[i], w0_b=w0_b[i], w_w2_b=w_w2_b[i],
                  a0_f=a0_f[i], w_a2_f=w_a2_f[i], a0_b=a0_b[i], w_a2_b=w_a2_b[i],
                  w_g2=w_g2[i], k_k=k_k[i], k_a=k_a[i], r_k=r_k[i],
                  ln_x_w=ln_x_w[i], ln_x_b=ln_x_b[i], conv_w=conv_w[i], conv_gain=conv_gain[i],
                  w_out=w_out[i], norm2_g=norm2_g[i], w_ff1=w_ff1[i], w_ff2=w_ff2[i])
        mod_ctx = jnp.broadcast_to(jax.nn.silu(c_ctx) @ w_mod[i] + b_mod[i], (b_ctx, 6 * D_MODEL))
        x_ctx, s_f, s_b = _layer(x_ctx, mod_ctx, zero_state, zero_state, None, **lw)
        new_f.append(s_f)
        new_b.append(s_b)
        mod_lat = jax.nn.silu(c) @ w_mod[i] + b_mod[i]
        x_lat, _, _ = _layer(x_lat, mod_lat, state_rwkv_fwd[:, i], state_rwkv_bwd[:, i], rows, **lw)
    y_prompt = _rmsnorm(x_ctx, final_g)
    y_sample = _rmsnorm(x_lat, final_g)
    new_state_rwkv_fwd = jnp.stack(new_f, axis=1)
    new_state_rwkv_bwd = jnp.stack(new_b, axis=1)
    return (y_prompt, y_sample, new_state_rwkv_fwd, new_state_rwkv_bwd)
```

```python
import functools

import jax
import jax.numpy as jnp
from jax import lax
from jax.experimental import pallas as pl
from jax.experimental.pallas import tpu as pltpu

D_MODEL = 1024
GRID_W = 64
RWKV_WIDTH = D_MODEL // 2
HEAD_SIZE = 64
RWKV_HEADS = RWKV_WIDTH // HEAD_SIZE
CONV_WIDTH = D_MODEL - RWKV_WIDTH
LORA_COLS = 256
GATE_LORA = 128
RWKV_COLS = 3 * RWKV_WIDTH + LORA_COLS + GATE_LORA
IN_COLS = RWKV_COLS + 3 * CONV_WIDTH
D_FF = 4 * D_MODEL
NORM_EPS = 1e-6
GN_EPS = HEAD_SIZE * 1e-5

CHUNK = 64
TILE = 256
CHUNKS_PER_TILE = TILE // CHUNK
SCAN_BATCH = 8
VMEM_LIMIT = 56 * 1024 * 1024

F32 = jnp.float32
BF16 = jnp.bfloat16
HI = lax.Precision.HIGHEST


def _dot(a, b):
    return jnp.dot(a, b, preferred_element_type=F32)


def _dot_hi(a, b):
    return jnp.dot(a, b, precision=HI, preferred_element_type=F32)


def _dot_nt_hi(a, b):
    return lax.dot_general(a, b, (((1,), (1,)), ((), ())), precision=HI, preferred_element_type=F32)


def _dot_tn_hi(a, b):
    return lax.dot_general(a, b, (((0,), (0,)), ((), ())), precision=HI, preferred_element_type=F32)


def _split2(x):
    hi = x.astype(BF16)
    lo = (x - hi.astype(F32)).astype(BF16)
    return hi, lo


def _split3(x):
    hi = x.astype(BF16)
    r1 = x - hi.astype(F32)
    mid = r1.astype(BF16)
    lo = (r1 - mid.astype(F32)).astype(BF16)
    return hi, mid, lo


def _head_sum(x, bones):
    hi, lo = _split2(x)
    return _dot(hi, bones) + _dot(lo, bones)


def _sigmoid(x):
    return 1.0 / (1.0 + jnp.exp(-x))


def _softplus(x):
    return jnp.maximum(x, 0.0) + jnp.log(1.0 + jnp.exp(-jnp.abs(x)))


def _const_spec(shape):
    nd = len(shape)
    return pl.BlockSpec(shape, lambda *_: (0,) * nd, pipeline_mode=pl.Buffered(1))


def _mod_body(c_ref, w_ref, b_ref, o_ref):
    cv = c_ref[...]
    s = cv * _sigmoid(cv)
    o_ref[...] = _dot_hi(s, w_ref[...]) + b_ref[...]


def _modulation(cvec, w_mod, b_mod):
    rows = cvec.shape[0]
    n = w_mod.shape[1]
    bn = 1536
    return pl.pallas_call(
        _mod_body,
        grid=(n // bn,),
        in_specs=[pl.BlockSpec((rows, D_MODEL), lambda j: (0, 0)),
                  pl.BlockSpec((D_MODEL, bn), lambda j: (0, j)),
                  pl.BlockSpec((1, bn), lambda j: (0, j))],
        out_specs=pl.BlockSpec((rows, bn), lambda j: (0, j)),
        out_shape=jax.ShapeDtypeStruct((rows, n), F32),
        compiler_params=pltpu.CompilerParams(dimension_semantics=("arbitrary",),
                                             vmem_limit_bytes=VMEM_LIMIT),
        name="modulation",
    )(cvec, w_mod, b_mod)


def _phase1_body(roww, x_ref, mod_ref, g1_ref, win_ref, mu_ref, wl_ref, bl_ref, wg2_ref, kk_ref,
                 ka_ref, rk_ref, cw_ref, cg_ref, bones_ref, trif_ref, trib_ref,
                 y0_ref, rqf_ref, rqb_ref, bonus_ref, gg_ref, oconv_ref, mnf_ref, mnb_ref,
                 z_scr, ops_scr, v_scr, row_scr, res_scr):
    W = RWKV_WIDTH
    x = x_ref[0]
    mod = mod_ref[0]
    shift_a = mod[:, 0:D_MODEL]
    scale_a = mod[:, D_MODEL:2 * D_MODEL]
    ms = jnp.mean(x * x, axis=-1, keepdims=True)
    h = (x * lax.rsqrt(ms + NORM_EPS)) * g1_ref[...] * (1.0 + scale_a) + shift_a
    z_scr[...] = _dot(h.astype(BF16), win_ref[...])

    t_idx = lax.broadcasted_iota(jnp.int32, (TILE, 1), 0)
    pos = lax.rem(t_idx, roww)
    is_first = pos == 0
    is_last = pos == roww - 1

    def neighbours(zz):
        prev = jnp.where(is_first, 0.0, pltpu.roll(zz, 1, 0))
        nxt = jnp.where(is_last, 0.0, pltpu.roll(zz, TILE - 1, 0))
        return prev, nxt

    def shifted(lo, hi):
        zz = z_scr[:, lo:hi]
        prev, nxt = neighbours(zz)
        return zz + mu_ref[:, lo:hi] * (0.5 * (prev + nxt) - zz)

    r = shifted(0, W)
    k = shifted(W, 2 * W)
    v = shifted(2 * W, 3 * W)
    lora = shifted(3 * W, 3 * W + LORA_COLS)
    gl = shifted(3 * W + LORA_COLS, RWKV_COLS)

    lora_in = jnp.concatenate([jnp.tanh(lora[:, :128]), lora[:, 128:]], axis=1)
    pre = _dot_hi(lora_in, wl_ref[...]) + bl_ref[...]
    gg_ref[0] = _dot_hi(_sigmoid(gl), wg2_ref[...])

    bones = bones_ref[...]
    kraw = k * kk_ref[...]
    kk = kraw * lax.rsqrt(_head_sum(kraw * kraw, bones) + 1e-12)

    for hd in range(RWKV_HEADS):
        v_scr[hd] = v[:, hd * HEAD_SIZE:(hd + 1) * HEAD_SIZE]

    kd_sum = None
    for d in range(2):
        wlog = -_softplus(-pre[:, d * W:(d + 1) * W]) - 0.5
        lw = -jnp.exp(wlog)
        a = _sigmoid(pre[:, (2 + d) * W:(3 + d) * W])
        kd = k * (1.0 + (a - 1.0) * ka_ref[...])
        kd_sum = kd if kd_sum is None else kd_sum + kd
        b = kk * a
        tri = trif_ref[...] if d == 0 else trib_ref[...]
        l_hi, l_mid, l_lo = _split3(lw)
        c = _dot(tri, l_hi) + _dot(tri, l_mid) + _dot(tri, l_lo)
        c3 = c.reshape(CHUNKS_PER_TILE, CHUNK, W)
        mid = c3[:, CHUNK // 2:CHUNK // 2 + 1, :]
        end_row = CHUNK - 1 if d == 0 else 0
        end = c3[:, end_row:end_row + 1, :]
        cm = (c3 - mid).reshape(TILE, W)
        e_r = jnp.exp(cm)
        e_kap = jnp.exp(cm - lw)
        e_inv = jnp.exp(-cm)
        arrs = (kk * e_kap, r * e_r, kd * e_inv, b * e_inv)
        for j, arr in enumerate(arrs):
            for hd in range(RWKV_HEADS):
                ops_scr[d, j, hd] = arr[:, hd * HEAD_SIZE:(hd + 1) * HEAD_SIZE]
        for j, rv in enumerate((jnp.exp(end - mid), jnp.exp(end), jnp.exp(mid))):
            rv8 = jnp.broadcast_to(rv, (CHUNKS_PER_TILE, 8, W)).reshape(CHUNKS_PER_TILE * 8, W)
            for hd in range(RWKV_HEADS):
                row_scr[d, j, hd] = rv8[:, hd * HEAD_SIZE:(hd + 1) * HEAD_SIZE]

    bonus_ref[0] = _head_sum(r * kd_sum * rk_ref[...], bones) * v

    gate_b = z_scr[:, RWKV_COLS:RWKV_COLS + CONV_WIDTH]
    u = z_scr[:, RWKV_COLS + CONV_WIDTH:RWKV_COLS + 2 * CONV_WIDTH] * z_scr[:, RWKV_COLS + 2 * CONV_WIDTH:IN_COLS]
    up, un = neighbours(u)
    oc = gate_b * (cw_ref[0:1, :] * up + cw_ref[1:2, :] * u + cw_ref[2:3, :] * un)
    oc = oc * lax.rsqrt(jnp.mean(oc * oc, axis=-1, keepdims=True) + NORM_EPS) * cg_ref[...]
    oconv_ref[0] = oc

    row = lax.broadcasted_iota(jnp.int32, (CHUNK, CHUNK), 0)
    col = lax.broadcasted_iota(jnp.int32, (CHUNK, CHUNK), 1)
    eye = row == col
    masks = ((row > col, row >= col), (row < col, row <= col))
    mn_refs = (mnf_ref, mnb_ref)

    def chunk_head(i, carry):
        ci = i // RWKV_HEADS
        hd = i % RWKV_HEADS
        rows = pl.ds(pl.multiple_of(ci * CHUNK, CHUNK), CHUNK)
        vv = v_scr[hd, rows, :]
        y0 = None
        for d in range(2):
            strict, incl = masks[d]
            kap = ops_scr[d, 0, hd, rows, :]
            rt = ops_scr[d, 1, hd, rows, :]
            kt = ops_scr[d, 2, hd, rows, :]
            bt = ops_scr[d, 3, hd, rows, :]
            crow = pl.ds(pl.multiple_of(ci * 8, 8), 1)
            k_end = kt * row_scr[d, 0, hd, crow, :]
            b_end = bt * row_scr[d, 0, hd, crow, :]
            w_tot = row_scr[d, 1, hd, crow, :]
            e_mid = row_scr[d, 2, hd, crow, :]
            a_kk = jnp.where(strict, _dot_nt_hi(kap, kt), 0.0)
            a_kb = jnp.where(strict, _dot_nt_hi(kap, bt), 0.0)
            a_rk = jnp.where(incl, _dot_nt_hi(rt, kt), 0.0)
            a_rb = jnp.where(incl, _dot_nt_hi(rt, bt), 0.0)
            t_inv = jnp.where(eye, 1.0, 0.0) - a_kb
            pw = _dot_hi(a_kb, a_kb)
            n_fac = 2
            while True:
                t_inv = t_inv + _dot_hi(t_inv, pw)
                n_fac *= 2
                if n_fac >= CHUNK:
                    break
                pw = _dot_hi(pw, pw)
            p = _dot_hi(t_inv, kap * e_mid)
            q = _dot_hi(t_inv, _dot_hi(a_kk, vv))
            rq = rt * e_mid - _dot_hi(a_rb, p)
            y0d = _dot_hi(a_rk, vv) - _dot_hi(a_rb, q)
            y0 = y0d if y0 is None else y0 + y0d
            m_mat = jnp.where(eye, jnp.broadcast_to(w_tot, (CHUNK, CHUNK)), 0.0) - _dot_tn_hi(p, b_end)
            n_mat = _dot_tn_hi(vv, k_end) - _dot_tn_hi(q, b_end)
            mn_refs[d][0, ci, hd, 0:HEAD_SIZE, :] = m_mat
            mn_refs[d][0, ci, hd, HEAD_SIZE:2 * HEAD_SIZE, :] = n_mat
            res_scr[d, hd, rows, :] = rq
        res_scr[2, hd, rows, :] = y0
        return carry

    lax.fori_loop(0, CHUNKS_PER_TILE * RWKV_HEADS, chunk_head, 0)

    for j, ref in enumerate((rqf_ref, rqb_ref, y0_ref)):
        ref[0] = jnp.concatenate([res_scr[j, hd] for hd in range(RWKV_HEADS)], axis=1)


def _phase1(x, mod, roww, lw):
    bsz, seq, _ = x.shape
    nt = seq // TILE
    nc = seq // CHUNK
    mod_map = (lambda b, t: (b, 0, 0)) if mod.shape[0] == bsz else (lambda b, t: (0, 0, 0))
    tok = lambda n: pl.BlockSpec((1, TILE, n), lambda b, t: (b, t, 0))
    mn_spec = pl.BlockSpec((1, CHUNKS_PER_TILE, RWKV_HEADS, 2 * HEAD_SIZE, HEAD_SIZE),
                           lambda b, t: (b, t, 0, 0, 0))
    W = RWKV_WIDTH
    consts = (lw["norm1_g"], lw["w_in"], lw["mu"], lw["w_lora"], lw["b_lora"], lw["w_g2"], lw["k_k"],
              lw["k_a"], lw["r_k"], lw["conv_w"], lw["conv_gain"], lw["bones"], lw["tri_f"], lw["tri_b"])
    tok_shape = jax.ShapeDtypeStruct((bsz, seq, W), F32)
    mn_shape = jax.ShapeDtypeStruct((bsz, nc, RWKV_HEADS, 2 * HEAD_SIZE, HEAD_SIZE), F32)
    return pl.pallas_call(
        functools.partial(_phase1_body, roww),
        grid=(bsz, nt),
        in_specs=[tok(D_MODEL), pl.BlockSpec((1, 1, 6 * D_MODEL), mod_map)]
                 + [_const_spec(a.shape) for a in consts],
        out_specs=[tok(W)] * 6 + [mn_spec, mn_spec],
        out_shape=[tok_shape] * 6 + [mn_shape, mn_shape],
        scratch_shapes=[pltpu.VMEM((TILE, IN_COLS), F32),
                        pltpu.VMEM((2, 4, RWKV_HEADS, TILE, HEAD_SIZE), F32),
                        pltpu.VMEM((RWKV_HEADS, TILE, HEAD_SIZE), F32),
                        pltpu.VMEM((2, 3, RWKV_HEADS, CHUNKS_PER_TILE * 8, HEAD_SIZE), F32),
                        pltpu.VMEM((3, RWKV_HEADS, TILE, HEAD_SIZE), F32)],
        compiler_params=pltpu.CompilerParams(dimension_semantics=("parallel", "parallel"),
                                             vmem_limit_bytes=VMEM_LIMIT),
        name="phase1_chunk_summaries",
    )(x, mod, *consts)


def _phase2_body(nb, s0f_ref, s0b_ref, mnf_ref, mnb_ref, sf_ref, sb_ref, finf_ref, finb_ref, st_scr):
    step = pl.program_id(1)

    @pl.when(step == 0)
    def _():
        st_scr[0] = s0f_ref[...]
        st_scr[1] = s0b_ref[...]

    def per_batch(b, carry):
        for d, (mn_ref, s_ref) in enumerate(((mnf_ref, sf_ref), (mnb_ref, sb_ref))):
            for hd in range(RWKV_HEADS):
                s = st_scr[d, b, hd]
                s_ref[b, 0, hd] = s
                m_mat = mn_ref[b, 0, hd, 0:HEAD_SIZE, :]
                n_mat = mn_ref[b, 0, hd, HEAD_SIZE:2 * HEAD_SIZE, :]
                st_scr[d, b, hd] = _dot_hi(s, m_mat) + n_mat
        return carry

    lax.fori_loop(0, nb, per_batch, 0)

    @pl.when(step == pl.num_programs(1) - 1)
    def _():
        finf_ref[...] = st_scr[0]
        finb_ref[...] = st_scr[1]


def _phase2(s0f, s0b, mnf, mnb):
    bsz, nc = mnf.shape[0], mnf.shape[1]
    bb = min(bsz, SCAN_BATCH)
    assert bsz % bb == 0
    st_blk = (bb, RWKV_HEADS, HEAD_SIZE, HEAD_SIZE)
    mn_blk = (bb, 1, RWKV_HEADS, 2 * HEAD_SIZE, HEAD_SIZE)
    s_blk = (bb, 1, RWKV_HEADS, HEAD_SIZE, HEAD_SIZE)
    fwd = lambda g, i: (g, i, 0, 0, 0)
    bwd = lambda g, i: (g, nc - 1 - i, 0, 0, 0)
    full = pl.BlockSpec(st_blk, lambda g, i: (g, 0, 0, 0))
    s_all = jax.ShapeDtypeStruct((bsz, nc, RWKV_HEADS, HEAD_SIZE, HEAD_SIZE), F32)
    s_fin = jax.ShapeDtypeStruct((bsz, RWKV_HEADS, HEAD_SIZE, HEAD_SIZE), F32)
    return pl.pallas_call(
        functools.partial(_phase2_body, bb),
        grid=(bsz // bb, nc),
        in_specs=[full, full, pl.BlockSpec(mn_blk, fwd), pl.BlockSpec(mn_blk, bwd)],
        out_specs=[pl.BlockSpec(s_blk, fwd), pl.BlockSpec(s_blk, bwd), full, full],
        out_shape=[s_all, s_all, s_fin, s_fin],
        scratch_shapes=[pltpu.VMEM((2,) + st_blk, F32)],
        compiler_params=pltpu.CompilerParams(dimension_semantics=("parallel", "arbitrary"),
                                             vmem_limit_bytes=VMEM_LIMIT),
        name="phase2_state_scan",
    )(s0f, s0b, mnf, mnb)


def _phase3_body(x_ref, mod_ref, y0_ref, rqf_ref, rqb_ref, bonus_ref, gg_ref, oconv_ref, sf_ref, sb_ref,
                 lnw_ref, lnb_ref, bones_ref, wout_ref, g2_ref, wff1_ref, wff2_ref, gfin_ref,
                 o_ref, rq_scr, y_scr):
    W = RWKV_WIDTH
    for d, ref in enumerate((rqf_ref, rqb_ref)):
        full = ref[0]
        for hd in range(RWKV_HEADS):
            rq_scr[d, hd] = full[:, hd * HEAD_SIZE:(hd + 1) * HEAD_SIZE]

    def chunk_head(i, carry):
        ci = i // RWKV_HEADS
        hd = i % RWKV_HEADS
        rows = pl.ds(pl.multiple_of(ci * CHUNK, CHUNK), CHUNK)
        y_scr[hd, rows, :] = (_dot_nt_hi(rq_scr[0, hd, rows, :], sf_ref[0, ci, hd])
                              + _dot_nt_hi(rq_scr[1, hd, rows, :], sb_ref[0, ci, hd]))
        return carry

    lax.fori_loop(0, CHUNKS_PER_TILE * RWKV_HEADS, chunk_head, 0)

    y = y0_ref[0] + jnp.concatenate([y_scr[hd] for hd in range(RWKV_HEADS)], axis=1)
    bones = bones_ref[...]
    inv_n = 1.0 / HEAD_SIZE
    y_mu = _head_sum(y, bones) * inv_n
    yc = y - y_mu
    y_var = _head_sum(yc * yc, bones) * inv_n
    yn = yc * lax.rsqrt(y_var + GN_EPS) * lnw_ref[...] + lnb_ref[...]
    o_rwkv = (yn + bonus_ref[0]) * gg_ref[0]

    mod = mod_ref[0]
    gate_a = mod[:, 2 * D_MODEL:3 * D_MODEL]
    shift_f = mod[:, 3 * D_MODEL:4 * D_MODEL]
    scale_f = mod[:, 4 * D_MODEL:5 * D_MODEL]
    gate_f = mod[:, 5 * D_MODEL:6 * D_MODEL]
    mix = (_dot(o_rwkv.astype(BF16), wout_ref[0:W, :])
           + _dot(oconv_ref[0].astype(BF16), wout_ref[W:D_MODEL, :]))
    x1 = x_ref[0] + gate_a * mix
    ms = jnp.mean(x1 * x1, axis=-1, keepdims=True)
    h2 = (x1 * lax.rsqrt(ms + NORM_EPS)) * g2_ref[...] * (1.0 + scale_f) + shift_f
    f1 = jnp.maximum(_dot(h2.astype(BF16), wff1_ref[...]), 0.0)
    f2 = _dot((f1 * f1).astype(BF16), wff2_ref[...])
    x2 = x1 + gate_f * f2
    ms2 = jnp.mean(x2 * x2, axis=-1, keepdims=True)
    o_ref[0] = (x2 * lax.rsqrt(ms2 + NORM_EPS)) * gfin_ref[...]


def _phase3(x, mod, p1, sf, sb, lw, final_g):
    bsz, seq, _ = x.shape
    nt = seq // TILE
    W = RWKV_WIDTH
    mod_map = (lambda b, t: (b, 0, 0)) if mod.shape[0] == bsz else (lambda b, t: (0, 0, 0))
    tok = lambda n: pl.BlockSpec((1, TILE, n), lambda b, t: (b, t, 0))
    s_spec = pl.BlockSpec((1, CHUNKS_PER_TILE, RWKV_HEADS, HEAD_SIZE, HEAD_SIZE), lambda b, t: (b, t, 0, 0, 0))
    consts = (lw["ln_x_w"], lw["ln_x_b"], lw["bones"], lw["w_out"], lw["norm2_g"], lw["w_ff1"], lw["w_ff2"],
              final_g)
    return pl.pallas_call(
        _phase3_body,
        grid=(bsz, nt),
        in_specs=[tok(D_MODEL), pl.BlockSpec((1, 1, 6 * D_MODEL), mod_map)] + [tok(W)] * 6 + [s_spec, s_spec]
                 + [_const_spec(a.shape) for a in consts],
        out_specs=tok(D_MODEL),
        out_shape=jax.ShapeDtypeStruct((bsz, seq, D_MODEL), F32),
        scratch_shapes=[pltpu.VMEM((2, RWKV_HEADS, TILE, HEAD_SIZE), F32),
                        pltpu.VMEM((RWKV_HEADS, TILE, HEAD_SIZE), F32)],
        compiler_params=pltpu.CompilerParams(dimension_semantics=("parallel", "parallel"),
                                             vmem_limit_bytes=VMEM_LIMIT),
        name="phase3_mix_mlp",
    )(x, mod, *p1, sf, sb, *consts)


def _block_diag(blocks):
    rows = sum(b.shape[0] for b in blocks)
    cols = sum(b.shape[1] for b in blocks)
    out = jnp.zeros((rows, cols), blocks[0].dtype)
    r = c = 0
    for b in blocks:
        out = lax.dynamic_update_slice(out, b, (r, c))
        r += b.shape[0]
        c += b.shape[1]
    return out


def _tri_constants():
    t = jnp.arange(TILE)
    same = (t[:, None] // CHUNK) == (t[None, :] // CHUNK)
    tri_f = (same & (t[None, :] <= t[:, None])).astype(BF16)
    tri_b = (same & (t[None, :] >= t[:, None])).astype(BF16)
    lane = jnp.arange(RWKV_WIDTH)
    bones = ((lane[:, None] // HEAD_SIZE) == (lane[None, :] // HEAD_SIZE)).astype(BF16)
    return tri_f, tri_b, bones


def kernel(x_prompt, x_sample, c, state_rwkv_fwd, state_rwkv_bwd, c_ctx, w_mod, b_mod, norm1_g, w_in, mu_shift, w0_f, w_w2_f, w0_b, w_w2_b, a0_f, w_a2_f, a0_b, w_a2_b, w_g2, k_k, k_a, r_k, ln_x_w, ln_x_b, conv_w, conv_gain, w_out, norm2_g, w_ff1, w_ff2, final_g):
    depth = w_in.shape[0]
    b_ctx, seq_ctx, _ = x_prompt.shape
    b_lat, seq_lat, _ = x_sample.shape
    assert depth == 1
    assert seq_ctx == TILE and seq_lat % TILE == 0 and TILE % GRID_W == 0
    tri_f, tri_b, bones = _tri_constants()
    row = lambda a: a.reshape(1, -1)
    n_pad = (-(b_lat + 1)) % 8
    cvec = jnp.concatenate([c, c_ctx[None, :], jnp.zeros((n_pad, D_MODEL), F32)], axis=0)
    zero_state = jnp.zeros((b_ctx, RWKV_HEADS, HEAD_SIZE, HEAD_SIZE), F32)

    x_ctx, x_lat = x_prompt, x_sample
    new_f, new_b = [], []
    for i in range(depth):
        lw = dict(
            norm1_g=row(norm1_g[i]), w_in=w_in[i].astype(BF16), mu=row(mu_shift[i]),
            w_lora=_block_diag([w_w2_f[i], w_w2_b[i], w_a2_f[i], w_a2_b[i]]),
            b_lora=jnp.concatenate([w0_f[i], w0_b[i], a0_f[i], a0_b[i]]).reshape(1, -1),
            w_g2=w_g2[i], k_k=row(k_k[i]), k_a=row(k_a[i]), r_k=row(r_k[i]),
            conv_w=conv_w[i], conv_gain=row(conv_gain[i]), bones=bones, tri_f=tri_f, tri_b=tri_b,
            ln_x_w=row(ln_x_w[i]), ln_x_b=row(ln_x_b[i]), w_out=w_out[i].astype(BF16),
            norm2_g=row(norm2_g[i]), w_ff1=w_ff1[i].astype(BF16), w_ff2=w_ff2[i].astype(BF16))
        g_fin = row(final_g)
        mod =_modulation(cvec, w_mod[i], row(b_mod[i]))
        mod_lat = mod[:b_lat].reshape(b_lat, 1, 6 * D_MODEL)
        mod_ctx = mod[b_lat:b_lat + 1].reshape(1, 1, 6 * D_MODEL)

        def run(x, mod_s, roww, s0f, s0b):
            *p1, mnf, mnb = _phase1(x, mod_s, roww, lw)
            sf, sb, fin_f, fin_b = _phase2(s0f, s0b, mnf, mnb)
            return _phase3(x, mod_s, p1, sf, sb, lw, g_fin), fin_f, fin_b

        x_ctx, s_f, s_b = run(x_ctx, mod_ctx, seq_ctx, zero_state, zero_state)
        new_f.append(s_f)
        new_b.append(s_b)
        x_lat, _, _ = run(x_lat, mod_lat, GRID_W, state_rwkv_fwd[:, i], state_rwkv_bwd[:, i])
    return (x_ctx, x_lat, jnp.stack(new_f, axis=1), jnp.stack(new_b, axis=1))
```

```python
import functools

import jax
import jax.numpy as jnp
from jax import lax
from jax.experimental import pallas as pl
from jax.experimental.pallas import tpu as pltpu

D_MODEL = 1024
GRID_W = 64
RWKV_WIDTH = D_MODEL // 2
HEAD_SIZE = 64
RWKV_HEADS = RWKV_WIDTH // HEAD_SIZE
CONV_WIDTH = D_MODEL - RWKV_WIDTH
LORA_COLS = 256
GATE_LORA = 128
RWKV_COLS = 3 * RWKV_WIDTH + LORA_COLS + GATE_LORA
IN_COLS = RWKV_COLS + 3 * CONV_WIDTH
D_FF = 4 * D_MODEL
NORM_EPS = 1e-6
GN_EPS = HEAD_SIZE * 1e-5

LANES = 128
SUBLANES = 8
PAIRS = RWKV_WIDTH // LANES
CHUNK = 64
TILE = 256
CHUNKS_PER_TILE = TILE // CHUNK
GROUP = 2
SCAN_BATCH = 8
VMEM_LIMIT = 56 * 1024 * 1024

assert 2 * HEAD_SIZE == LANES and CHUNK == HEAD_SIZE

F32 = jnp.float32
BF16 = jnp.bfloat16
HI = lax.Precision.HIGHEST


def _dot(a, b):
    return jnp.dot(a, b, preferred_element_type=F32)


def _dot_hi(a, b):
    return jnp.dot(a, b, precision=HI, preferred_element_type=F32)


def _mm(a, b):
    return jnp.dot(a.astype(BF16), b.astype(BF16), preferred_element_type=F32)


def _mm_nt(a, b):
    return lax.dot_general(a.astype(BF16), b.astype(BF16), (((1,), (1,)), ((), ())),
                           preferred_element_type=F32)


def _mm_tn(a, b):
    return lax.dot_general(a.astype(BF16), b.astype(BF16), (((0,), (0,)), ((), ())),
                           preferred_element_type=F32)


def _split2(x):
    hi = x.astype(BF16)
    lo = (x - hi.astype(F32)).astype(BF16)
    return hi, lo


def _split3(x):
    hi = x.astype(BF16)
    r1 = x - hi.astype(F32)
    mid = r1.astype(BF16)
    lo = (r1 - mid.astype(F32)).astype(BF16)
    return hi, mid, lo


def _head_sum(x, bones):
    hi, lo = _split2(x)
    return _dot(hi, bones) + _dot(lo, bones)


def _sigmoid(x):
    return 1.0 / (1.0 + jnp.exp(-x))


def _softplus(x):
    return jnp.maximum(x, 0.0) + jnp.log(1.0 + jnp.exp(-jnp.abs(x)))


def _block_diag_rhs(xc):
    first = lax.broadcasted_iota(jnp.int32, xc.shape, 1) < HEAD_SIZE
    zero = jnp.zeros_like(xc)
    return jnp.concatenate([jnp.where(first, xc, zero), jnp.where(first, zero, xc)], axis=0)


def _compact(xbd):
    first = lax.broadcasted_iota(jnp.int32, (HEAD_SIZE, LANES), 1) < HEAD_SIZE
    return jnp.where(first, xbd[0:HEAD_SIZE], xbd[HEAD_SIZE:2 * HEAD_SIZE])


def _const_spec(shape):
    nd = len(shape)
    return pl.BlockSpec(shape, lambda *_: (0,) * nd, pipeline_mode=pl.Buffered(1))


def _mod_body(c_ref, w_ref, b_ref, o_ref):
    cv = c_ref[...]
    s = cv * _sigmoid(cv)
    o_ref[...] = _dot_hi(s, w_ref[...]) + b_ref[...]


def _modulation(cvec, w_mod, b_mod):
    rows = cvec.shape[0]
    n = w_mod.shape[1]
    bn = 1536
    return pl.pallas_call(
        _mod_body,
        grid=(n // bn,),
        in_specs=[pl.BlockSpec((rows, D_MODEL), lambda j: (0, 0)),
                  pl.BlockSpec((D_MODEL, bn), lambda j: (0, j)),
                  pl.BlockSpec((1, bn), lambda j: (0, j))],
        out_specs=pl.BlockSpec((rows, bn), lambda j: (0, j)),
        out_shape=jax.ShapeDtypeStruct((rows, n), F32),
        compiler_params=pltpu.CompilerParams(dimension_semantics=("arbitrary",),
                                             vmem_limit_bytes=VMEM_LIMIT),
        name="modulation",
    )(cvec, w_mod, b_mod)


def _phase1_body(roww, x_ref, mod_ref, g1_ref, win_ref, mu_ref, wl_ref, bl_ref, wg2_ref, kk_ref,
                 ka_ref, rk_ref, cw_ref, cg_ref, bones_ref, trif_ref, trib_ref,
                 y0_ref, rqf_ref, rqb_ref, bonus_ref, gg_ref, oconv_ref, mnf_ref, mnb_ref,
                 z_scr, ops_scr, v_scr, row_scr):
    W = RWKV_WIDTH
    x = x_ref[0]
    mod = mod_ref[0]
    shift_a = mod[:, 0:D_MODEL]
    scale_a = mod[:, D_MODEL:2 * D_MODEL]
    ms = jnp.mean(x * x, axis=-1, keepdims=True)
    h = (x * lax.rsqrt(ms + NORM_EPS)) * g1_ref[...] * (1.0 + scale_a) + shift_a
    z_scr[...] = _dot(h.astype(BF16), win_ref[...])

    t_idx = lax.broadcasted_iota(jnp.int32, (TILE, 1), 0)
    pos = lax.rem(t_idx, roww)
    is_first = pos == 0
    is_last = pos == roww - 1

    def neighbours(zz):
        prev = jnp.where(is_first, 0.0, pltpu.roll(zz, 1, 0))
        nxt = jnp.where(is_last, 0.0, pltpu.roll(zz, TILE - 1, 0))
        return prev, nxt

    def shifted(lo, hi):
        zz = z_scr[:, lo:hi]
        prev, nxt = neighbours(zz)
        return zz + mu_ref[:, lo:hi] * (0.5 * (prev + nxt) - zz)

    r = shifted(0, W)
    k = shifted(W, 2 * W)
    v = shifted(2 * W, 3 * W)
    lora = shifted(3 * W, 3 * W + LORA_COLS)
    gl = shifted(3 * W + LORA_COLS, RWKV_COLS)
    v_scr[...] = v

    lora_in = jnp.concatenate([jnp.tanh(lora[:, :LANES]), lora[:, LANES:]], axis=1)
    pre = _dot_hi(lora_in, wl_ref[...]) + bl_ref[...]
    gg_ref[0] = _dot_hi(_sigmoid(gl), wg2_ref[...])

    bones = bones_ref[...]
    kraw = k * kk_ref[...]
    kk = kraw * lax.rsqrt(_head_sum(kraw * kraw, bones) + 1e-12)

    kd_sum = None
    for d in range(2):
        wlog = -_softplus(-pre[:, d * W:(d + 1) * W]) - 0.5
        lw = -jnp.exp(wlog)
        a = _sigmoid(pre[:, (2 + d) * W:(3 + d) * W])
        kd = k * (1.0 + (a - 1.0) * ka_ref[...])
        kd_sum = kd if kd_sum is None else kd_sum + kd
        b = kk * a
        tri = trif_ref[...] if d == 0 else trib_ref[...]
        l_hi, l_mid, l_lo = _split3(lw)
        c = _dot(tri, l_hi) + _dot(tri, l_mid) + _dot(tri, l_lo)
        c3 = c.reshape(CHUNKS_PER_TILE, CHUNK, W)
        mid = c3[:, CHUNK // 2:CHUNK // 2 + 1, :]
        end_row = CHUNK - 1 if d == 0 else 0
        end = c3[:, end_row:end_row + 1, :]
        cm = (c3 - mid).reshape(TILE, W)
        e_r = jnp.exp(cm)
        e_kap = jnp.exp(cm - lw)
        e_inv = jnp.exp(-cm)
        ops_scr[d, 0] = kk * e_kap
        ops_scr[d, 1] = r * e_r
        ops_scr[d, 2] = kd * e_inv
        ops_scr[d, 3] = b * e_inv
        for j, rv in enumerate((jnp.exp(end - mid), jnp.exp(end), jnp.exp(mid))):
            row_scr[d, j] = jnp.broadcast_to(rv, (CHUNKS_PER_TILE, SUBLANES, W)).reshape(
                CHUNKS_PER_TILE * SUBLANES, W)

    bonus_ref[0] = _head_sum(r * kd_sum * rk_ref[...], bones) * v

    gate_b = z_scr[:, RWKV_COLS:RWKV_COLS + CONV_WIDTH]
    u = z_scr[:, RWKV_COLS + CONV_WIDTH:RWKV_COLS + 2 * CONV_WIDTH] * z_scr[:, RWKV_COLS + 2 * CONV_WIDTH:IN_COLS]
    up, un = neighbours(u)
    oc = gate_b * (cw_ref[0:1, :] * up + cw_ref[1:2, :] * u + cw_ref[2:3, :] * un)
    oc = oc * lax.rsqrt(jnp.mean(oc * oc, axis=-1, keepdims=True) + NORM_EPS) * cg_ref[...]
    oconv_ref[0] = oc

    row = lax.broadcasted_iota(jnp.int32, (CHUNK, LANES), 0)
    col = lax.bitwise_and(lax.broadcasted_iota(jnp.int32, (CHUNK, LANES), 1), HEAD_SIZE - 1)
    eye = row == col
    masks = (jnp.concatenate([row > col, row >= col], axis=0),
             jnp.concatenate([row < col, row <= col], axis=0))
    mn_refs = (mnf_ref, mnb_ref)
    rq_refs = (rqf_ref, rqb_ref)

    def chunk_group(g, carry):
        cis = [g * GROUP + j for j in range(GROUP)]
        rows = [pl.ds(pl.multiple_of(ci * CHUNK, CHUNK), CHUNK) for ci in cis]
        crow = [pl.ds(pl.multiple_of(ci * SUBLANES, SUBLANES), 1) for ci in cis]
        lsl = [slice(p * LANES, (p + 1) * LANES) for p in range(PAIRS)]
        cp = [(c, p) for c in range(GROUP) for p in range(PAIRS)]
        ch = [(c, p, d) for (c, p) in cp for d in range(2)]
        cp_of = {(c, p): i for i, (c, p) in enumerate(cp)}

        v_pair = [v_scr[rows[c], lsl[p]] for (c, p) in cp]
        vbd = [_block_diag_rhs(vp.astype(BF16)) for vp in v_pair]
        kap = [ops_scr[d, 0, rows[c], lsl[p]] for (c, p, d) in ch]
        rt = [ops_scr[d, 1, rows[c], lsl[p]] for (c, p, d) in ch]
        kt = [ops_scr[d, 2, rows[c], lsl[p]] for (c, p, d) in ch]
        bt = [ops_scr[d, 3, rows[c], lsl[p]] for (c, p, d) in ch]
        to_end = [row_scr[d, 0, crow[c], lsl[p]] for (c, p, d) in ch]
        w_tot = [row_scr[d, 1, crow[c], lsl[p]] for (c, p, d) in ch]
        e_mid = [row_scr[d, 2, crow[c], lsl[p]] for (c, p, d) in ch]
        n = len(ch)
        lhs = [jnp.concatenate([kap[i], rt[i]], axis=0).astype(BF16) for i in range(n)]
        a_k = [jnp.where(masks[ch[i][2]], _mm_nt(lhs[i], _block_diag_rhs(kt[i].astype(BF16))), 0.0)
               for i in range(n)]
        a_b = [jnp.where(masks[ch[i][2]], _mm_nt(lhs[i], _block_diag_rhs(bt[i].astype(BF16))), 0.0)
               for i in range(n)]
        a_kb = [x[0:CHUNK] for x in a_b]
        a_rb = [x[CHUNK:2 * CHUNK] for x in a_b]
        av = [_mm(a_k[i], vbd[cp_of[ch[i][:2]]]) for i in range(n)]
        t_inv = [jnp.where(eye, 1.0, 0.0) - x for x in a_kb]
        pw = [_mm(x, _block_diag_rhs(x.astype(BF16))) for x in a_kb]
        n_fac = 2
        while True:
            t_inv = [t + _mm(t, _block_diag_rhs(w.astype(BF16))) for t, w in zip(t_inv, pw)]
            n_fac *= 2
            if n_fac >= CHUNK:
                break
            pw = [_mm(w, _block_diag_rhs(w.astype(BF16))) for w in pw]
        rhs = [jnp.concatenate([_block_diag_rhs((kap[i] * e_mid[i]).astype(BF16)),
                                _block_diag_rhs(av[i][0:CHUNK].astype(BF16))], axis=1) for i in range(n)]
        pq = [_mm(t_inv[i], rhs[i]) for i in range(n)]
        rhs2 = [jnp.concatenate([_block_diag_rhs(x[:, 0:LANES].astype(BF16)),
                                 _block_diag_rhs(x[:, LANES:2 * LANES].astype(BF16))], axis=1) for x in pq]
        rb = [_mm(a_rb[i], rhs2[i]) for i in range(n)]
        for i, (c, p, d) in enumerate(ch):
            rq_refs[d][0, rows[c], lsl[p]] = rt[i] * e_mid[i] - rb[i][:, 0:LANES]
        y0d = [av[i][CHUNK:2 * CHUNK] - rb[i][:, LANES:2 * LANES] for i in range(n)]
        for i, (c, p, d) in enumerate(ch):
            if d == 0:
                y0_ref[0, rows[c], lsl[p]] = y0d[i] + y0d[i + 1]
        b_end = [(bt[i] * to_end[i]).astype(BF16) for i in range(n)]
        k_end = [(kt[i] * to_end[i]).astype(BF16) for i in range(n)]
        pq_b = [_mm(pq[i].T, b_end[i]) for i in range(n)]
        v_t = [vp.T for vp in v_pair]
        vk = [_mm(v_t[cp_of[ch[i][:2]]], k_end[i]) for i in range(n)]
        for i, (c, p, d) in enumerate(ch):
            m_c = (jnp.where(eye, jnp.broadcast_to(w_tot[i], (CHUNK, LANES)), 0.0)
                   - _compact(pq_b[i][0:LANES]))
            n_c = _compact(vk[i]) - _compact(pq_b[i][LANES:2 * LANES])
            mn_refs[d][0, cis[c], p, 0:HEAD_SIZE, :] = m_c
            mn_refs[d][0, cis[c], p, HEAD_SIZE:2 * HEAD_SIZE, :] = n_c
        return carry

    lax.fori_loop(0, CHUNKS_PER_TILE // GROUP, chunk_group, 0)


def _phase1(x, mod, roww, lw):
    bsz, seq, _ = x.shape
    nt = seq // TILE
    nc = seq // CHUNK
    mod_map = (lambda b, t: (b, 0, 0)) if mod.shape[0] == bsz else (lambda b, t: (0, 0, 0))
    tok = lambda n: pl.BlockSpec((1, TILE, n), lambda b, t: (b, t, 0))
    mn_spec = pl.BlockSpec((1, CHUNKS_PER_TILE, PAIRS, 2 * HEAD_SIZE, LANES), lambda b, t: (b, t, 0, 0, 0))
    W = RWKV_WIDTH
    consts = (lw["norm1_g"], lw["w_in"], lw["mu"], lw["w_lora"], lw["b_lora"], lw["w_g2"], lw["k_k"],
              lw["k_a"], lw["r_k"], lw["conv_w"], lw["conv_gain"], lw["bones"], lw["tri_f"], lw["tri_b"])
    tok_shape = jax.ShapeDtypeStruct((bsz, seq, W), F32)
    mn_shape = jax.ShapeDtypeStruct((bsz, nc, PAIRS, 2 * HEAD_SIZE, LANES), F32)
    return pl.pallas_call(
        functools.partial(_phase1_body, roww),
        grid=(bsz, nt),
        in_specs=[tok(D_MODEL), pl.BlockSpec((1, 1, 6 * D_MODEL), mod_map)]
                 + [_const_spec(a.shape) for a in consts],
        out_specs=[tok(W)] * 6 + [mn_spec, mn_spec],
        out_shape=[tok_shape] * 6 + [mn_shape, mn_shape],
        scratch_shapes=[pltpu.VMEM((TILE, IN_COLS), F32),
                        pltpu.VMEM((2, 4, TILE, W), F32),
                        pltpu.VMEM((TILE, W), F32),
                        pltpu.VMEM((2, 3, CHUNKS_PER_TILE * SUBLANES, W), F32)],
        compiler_params=pltpu.CompilerParams(dimension_semantics=("parallel", "parallel"),
                                             vmem_limit_bytes=VMEM_LIMIT),
        name="phase1_chunk_summaries",
    )(x, mod, *consts)


def _phase2_body(nb, s0f_ref, s0b_ref, mnf_ref, mnb_ref, sf_ref, sb_ref, finf_ref, finb_ref, st_scr):
    step = pl.program_id(1)

    @pl.when(step == 0)
    def _():
        st_scr[0] = s0f_ref[...]
        st_scr[1] = s0b_ref[...]

    def per_batch(b, carry):
        for d, (mn_ref, s_ref) in enumerate(((mnf_ref, sf_ref), (mnb_ref, sb_ref))):
            for p in range(PAIRS):
                s = st_scr[d, b, p]
                s_ref[b, 0, p] = s
                m_bd = _block_diag_rhs(mn_ref[b, 0, p, 0:HEAD_SIZE, :])
                st_scr[d, b, p] = _dot_hi(s, m_bd) + mn_ref[b, 0, p, HEAD_SIZE:2 * HEAD_SIZE, :]
        return carry

    lax.fori_loop(0, nb, per_batch, 0)

    @pl.when(step == pl.num_programs(1) - 1)
    def _():
        finf_ref[...] = st_scr[0]
        finb_ref[...] = st_scr[1]


def _phase2(s0f, s0b, mnf, mnb):
    bsz, nc = mnf.shape[0], mnf.shape[1]
    bb = min(bsz, SCAN_BATCH)
    assert bsz % bb == 0
    st_blk = (bb, PAIRS, HEAD_SIZE, LANES)
    mn_blk = (bb, 1, PAIRS, 2 * HEAD_SIZE, LANES)
    s_blk = (bb, 1, PAIRS, HEAD_SIZE, LANES)
    fwd = lambda g, i: (g, i, 0, 0, 0)
    bwd = lambda g, i: (g, nc - 1 - i, 0, 0, 0)
    full = pl.BlockSpec(st_blk, lambda g, i: (g, 0, 0, 0))
    s_all = jax.ShapeDtypeStruct((bsz, nc, PAIRS, HEAD_SIZE, LANES), F32)
    s_fin = jax.ShapeDtypeStruct((bsz, PAIRS, HEAD_SIZE, LANES), F32)
    return pl.pallas_call(
        functools.partial(_phase2_body, bb),
        grid=(bsz // bb, nc),
        in_specs=[full, full, pl.BlockSpec(mn_blk, fwd), pl.BlockSpec(mn_blk, bwd)],
        out_specs=[pl.BlockSpec(s_blk, fwd), pl.BlockSpec(s_blk, bwd), full, full],
        out_shape=[s_all, s_all, s_fin, s_fin],
        scratch_shapes=[pltpu.VMEM((2,) + st_blk, F32)],
        compiler_params=pltpu.CompilerParams(dimension_semantics=("parallel", "arbitrary"),
                                             vmem_limit_bytes=VMEM_LIMIT),
        name="phase2_state_scan",
    )(s0f, s0b, mnf, mnb)


def _phase3_body(x_ref, mod_ref, y0_ref, rqf_ref, rqb_ref, bonus_ref, gg_ref, oconv_ref, sf_ref, sb_ref,
                 lnw_ref, lnb_ref, bones_ref, wout_ref, g2_ref, wff1_ref, wff2_ref, gfin_ref,
                 o_ref, y_scr):
    W = RWKV_WIDTH
    for ci in range(CHUNKS_PER_TILE):
        rows = slice(ci * CHUNK, (ci + 1) * CHUNK)
        for p in range(PAIRS):
            ls = slice(p * LANES, (p + 1) * LANES)
            y_scr[rows, ls] = (y0_ref[0, rows, ls]
                               + _mm_nt(rqf_ref[0, rows, ls], _block_diag_rhs(sf_ref[0, ci, p].astype(BF16)))
                               + _mm_nt(rqb_ref[0, rows, ls], _block_diag_rhs(sb_ref[0, ci, p].astype(BF16))))

    y = y_scr[...]
    bones = bones_ref[...]
    inv_n = 1.0 / HEAD_SIZE
    y_mu = _head_sum(y, bones) * inv_n
    yc = y - y_mu
    y_var = _head_sum(yc * yc, bones) * inv_n
    yn = yc * lax.rsqrt(y_var + GN_EPS) * lnw_ref[...] + lnb_ref[...]
    o_rwkv = (yn + bonus_ref[0]) * gg_ref[0]

    mod = mod_ref[0]
    gate_a = mod[:, 2 * D_MODEL:3 * D_MODEL]
    shift_f = mod[:, 3 * D_MODEL:4 * D_MODEL]
    scale_f = mod[:, 4 * D_MODEL:5 * D_MODEL]
    gate_f = mod[:, 5 * D_MODEL:6 * D_MODEL]
    mix = (_dot(o_rwkv.astype(BF16), wout_ref[0:W, :])
           + _dot(oconv_ref[0].astype(BF16), wout_ref[W:D_MODEL, :]))
    x1 = x_ref[0] + gate_a * mix
    ms = jnp.mean(x1 * x1, axis=-1, keepdims=True)
    h2 = (x1 * lax.rsqrt(ms + NORM_EPS)) * g2_ref[...] * (1.0 + scale_f) + shift_f
    f1 = jnp.maximum(_dot(h2.astype(BF16), wff1_ref[...]), 0.0)
    f2 = _dot((f1 * f1).astype(BF16), wff2_ref[...])
    x2 = x1 + gate_f * f2
    ms2 = jnp.mean(x2 * x2, axis=-1, keepdims=True)
    o_ref[0] = (x2 * lax.rsqrt(ms2 + NORM_EPS)) * gfin_ref[...]


def _phase3(x, mod, p1, sf, sb, lw, final_g):
    bsz, seq, _ = x.shape
    nt = seq // TILE
    W = RWKV_WIDTH
    mod_map = (lambda b, t: (b, 0, 0)) if mod.shape[0] == bsz else (lambda b, t: (0, 0, 0))
    tok = lambda n: pl.BlockSpec((1, TILE, n), lambda b, t: (b, t, 0))
    s_spec = pl.BlockSpec((1, CHUNKS_PER_TILE, PAIRS, HEAD_SIZE, LANES), lambda b, t: (b, t, 0, 0, 0))
    consts = (lw["ln_x_w"], lw["ln_x_b"], lw["bones"], lw["w_out"], lw["norm2_g"], lw["w_ff1"], lw["w_ff2"],
              final_g)
    return pl.pallas_call(
        _phase3_body,
        grid=(bsz, nt),
        in_specs=[tok(D_MODEL), pl.BlockSpec((1, 1, 6 * D_MODEL), mod_map)] + [tok(W)] * 6 + [s_spec, s_spec]
                 + [_const_spec(a.shape) for a in consts],
        out_specs=tok(D_MODEL),
        out_shape=jax.ShapeDtypeStruct((bsz, seq, D_MODEL), F32),
        scratch_shapes=[pltpu.VMEM((TILE, W), F32)],
        compiler_params=pltpu.CompilerParams(dimension_semantics=("parallel", "parallel"),
                                             vmem_limit_bytes=VMEM_LIMIT),
        name="phase3_mix_mlp",
    )(x, mod, *p1, sf, sb, *consts)


def _block_diag(blocks):
    rows = sum(b.shape[0] for b in blocks)
    cols = sum(b.shape[1] for b in blocks)
    out = jnp.zeros((rows, cols), blocks[0].dtype)
    r = c = 0
    for b in blocks:
        out = lax.dynamic_update_slice(out, b, (r, c))
        r += b.shape[0]
        c += b.shape[1]
    return out


def _tri_constants():
    t = jnp.arange(TILE)
    same = (t[:, None] // CHUNK) == (t[None, :] // CHUNK)
    tri_f = (same & (t[None, :] <= t[:, None])).astype(BF16)
    tri_b = (same & (t[None, :] >= t[:, None])).astype(BF16)
    lane = jnp.arange(RWKV_WIDTH)
    bones = ((lane[:, None] // HEAD_SIZE) == (lane[None, :] // HEAD_SIZE)).astype(BF16)
    return tri_f, tri_b, bones


def _pair_compact(s):
    b = s.shape[0]
    s = s.reshape(b, PAIRS, 2, HEAD_SIZE, HEAD_SIZE)
    return jnp.swapaxes(s, 2, 3).reshape(b, PAIRS, HEAD_SIZE, LANES)


def _pair_expand(sc):
    b = sc.shape[0]
    s = sc.reshape(b, PAIRS, HEAD_SIZE, 2, HEAD_SIZE)
    return jnp.swapaxes(s, 2, 3).reshape(b, RWKV_HEADS, HEAD_SIZE, HEAD_SIZE)


def kernel(x_prompt, x_sample, c, state_rwkv_fwd, state_rwkv_bwd, c_ctx, w_mod, b_mod, norm1_g, w_in, mu_shift, w0_f, w_w2_f, w0_b, w_w2_b, a0_f, w_a2_f, a0_b, w_a2_b, w_g2, k_k, k_a, r_k, ln_x_w, ln_x_b, conv_w, conv_gain, w_out, norm2_g, w_ff1, w_ff2, final_g):
    depth = w_in.shape[0]
    b_ctx, seq_ctx, _ = x_prompt.shape
    b_lat, seq_lat, _ = x_sample.shape
    assert depth == 1
    assert seq_ctx == TILE and seq_lat % TILE == 0 and TILE % GRID_W == 0
    tri_f, tri_b, bones = _tri_constants()
    row = lambda a: a.reshape(1, -1)
    n_pad = (-(b_lat + 1)) % SUBLANES
    cvec = jnp.concatenate([c, c_ctx[None, :], jnp.zeros((n_pad, D_MODEL), F32)], axis=0)
    zero_state = jnp.zeros((b_ctx, PAIRS, HEAD_SIZE, LANES), F32)

    i = 0
    lw = dict(
        norm1_g=row(norm1_g[i]), w_in=w_in[i].astype(BF16), mu=row(mu_shift[i]),
        w_lora=_block_diag([w_w2_f[i], w_w2_b[i], w_a2_f[i], w_a2_b[i]]),
        b_lora=jnp.concatenate([w0_f[i], w0_b[i], a0_f[i], a0_b[i]]).reshape(1, -1),
        w_g2=w_g2[i], k_k=row(k_k[i]), k_a=row(k_a[i]), r_k=row(r_k[i]),
        conv_w=conv_w[i], conv_gain=row(conv_gain[i]), bones=bones, tri_f=tri_f, tri_b=tri_b,
        ln_x_w=row(ln_x_w[i]), ln_x_b=row(ln_x_b[i]), w_out=w_out[i].astype(BF16),
        norm2_g=row(norm2_g[i]), w_ff1=w_ff1[i].astype(BF16), w_ff2=w_ff2[i].astype(BF16))
    mod = _modulation(cvec, w_mod[i], row(b_mod[i]))
    mod_lat = mod[:b_lat].reshape(b_lat, 1, 6 * D_MODEL)
    mod_ctx = mod[b_lat:b_lat + 1].reshape(1, 1, 6 * D_MODEL)

    def run(x, mod_s, roww, s0f, s0b):
        *p1, mnf, mnb = _phase1(x, mod_s, roww, lw)
        sf, sb, fin_f, fin_b = _phase2(s0f, s0b, mnf, mnb)
        return _phase3(x, mod_s, p1, sf, sb, lw, row(final_g)), fin_f, fin_b

    y_ctx, s_f, s_b = run(x_prompt, mod_ctx, seq_ctx, zero_state, zero_state)
    y_lat, _, _ = run(x_sample, mod_lat, GRID_W, _pair_compact(state_rwkv_fwd[:, i]),
                      _pair_compact(state_rwkv_bwd[:, i]))
    return (y_ctx, y_lat, _pair_expand(s_f)[:, None], _pair_expand(s_b)[:, None])
```

```python
import functools

import jax
import jax.numpy as jnp
from jax import lax
from jax.experimental import pallas as pl
from jax.experimental.pallas import tpu as pltpu

D_MODEL = 1024
GRID_W = 64
RWKV_WIDTH = D_MODEL // 2
HEAD_SIZE = 64
RWKV_HEADS = RWKV_WIDTH // HEAD_SIZE
CONV_WIDTH = D_MODEL - RWKV_WIDTH
LORA_COLS = 256
GATE_LORA = 128
RWKV_COLS = 3 * RWKV_WIDTH + LORA_COLS + GATE_LORA
IN_COLS = RWKV_COLS + 3 * CONV_WIDTH
D_FF = 4 * D_MODEL
NORM_EPS = 1e-6
GN_EPS = HEAD_SIZE * 1e-5

LANES = 128
SUBLANES = 8
PAIRS = RWKV_WIDTH // LANES
CHUNK = 64
TILE = 256
CHUNKS_PER_TILE = TILE // CHUNK
GROUP = 4
SCAN_BATCH = 8
VMEM_LIMIT = 56 * 1024 * 1024

assert 2 * HEAD_SIZE == LANES and CHUNK == HEAD_SIZE

F32 = jnp.float32
BF16 = jnp.bfloat16
HI = lax.Precision.HIGHEST


def _dot(a, b):
    return jnp.dot(a, b, preferred_element_type=F32)


def _dot_hi(a, b):
    return jnp.dot(a, b, precision=HI, preferred_element_type=F32)


def _mm(a, b):
    return jnp.dot(a.astype(BF16), b.astype(BF16), preferred_element_type=F32)


def _mm_nt(a, b):
    return lax.dot_general(a.astype(BF16), b.astype(BF16), (((1,), (1,)), ((), ())),
                           preferred_element_type=F32)


def _mm_tn(a, b):
    return lax.dot_general(a.astype(BF16), b.astype(BF16), (((0,), (0,)), ((), ())),
                           preferred_element_type=F32)


def _split2(x):
    hi = x.astype(BF16)
    lo = (x - hi.astype(F32)).astype(BF16)
    return hi, lo


def _split3(x):
    hi = x.astype(BF16)
    r1 = x - hi.astype(F32)
    mid = r1.astype(BF16)
    lo = (r1 - mid.astype(F32)).astype(BF16)
    return hi, mid, lo


def _head_sum(x, bones):
    hi, lo = _split2(x)
    return _dot(hi, bones) + _dot(lo, bones)


def _sigmoid(x):
    return 1.0 / (1.0 + jnp.exp(-x))


def _softplus(x):
    return jnp.maximum(x, 0.0) + jnp.log(1.0 + jnp.exp(-jnp.abs(x)))


def _block_diag_rhs(xc):
    first = lax.broadcasted_iota(jnp.int32, xc.shape, 1) < HEAD_SIZE
    zero = jnp.zeros_like(xc)
    return jnp.concatenate([jnp.where(first, xc, zero), jnp.where(first, zero, xc)], axis=0)


def _compact(xbd):
    first = lax.broadcasted_iota(jnp.int32, (HEAD_SIZE, LANES), 1) < HEAD_SIZE
    return jnp.where(first, xbd[0:HEAD_SIZE], xbd[HEAD_SIZE:2 * HEAD_SIZE])


def _const_spec(shape):
    nd = len(shape)
    return pl.BlockSpec(shape, lambda *_: (0,) * nd, pipeline_mode=pl.Buffered(1))


def _mod_body(c_ref, w_ref, b_ref, o_ref):
    cv = c_ref[...]
    s = cv * _sigmoid(cv)
    o_ref[...] = _dot_hi(s, w_ref[...]) + b_ref[...]


def _modulation(cvec, w_mod, b_mod):
    rows = cvec.shape[0]
    n = w_mod.shape[1]
    bn = 1536
    return pl.pallas_call(
        _mod_body,
        grid=(n // bn,),
        in_specs=[pl.BlockSpec((rows, D_MODEL), lambda j: (0, 0)),
                  pl.BlockSpec((D_MODEL, bn), lambda j: (0, j)),
                  pl.BlockSpec((1, bn), lambda j: (0, j))],
        out_specs=pl.BlockSpec((rows, bn), lambda j: (0, j)),
        out_shape=jax.ShapeDtypeStruct((rows, n), F32),
        compiler_params=pltpu.CompilerParams(dimension_semantics=("arbitrary",),
                                             vmem_limit_bytes=VMEM_LIMIT),
        name="modulation",
    )(cvec, w_mod, b_mod)


def _phase1_body(roww, x_ref, mod_ref, g1_ref, win_ref, mu_ref, wl_ref, bl_ref, wg2_ref, kk_ref,
                 ka_ref, rk_ref, cw_ref, cg_ref, bones_ref, trif_ref, trib_ref,
                 y0_ref, rqf_ref, rqb_ref, bonus_ref, gg_ref, oconv_ref, mnf_ref, mnb_ref,
                 wtf_ref, wtb_ref,
                 z_scr, ops_scr, v_scr, row_scr):
    W = RWKV_WIDTH
    x = x_ref[0]
    mod = mod_ref[0]
    shift_a = mod[:, 0:D_MODEL]
    scale_a = mod[:, D_MODEL:2 * D_MODEL]
    ms = jnp.mean(x * x, axis=-1, keepdims=True)
    h = (x * lax.rsqrt(ms + NORM_EPS)) * g1_ref[...] * (1.0 + scale_a) + shift_a
    z_scr[...] = _dot(h.astype(BF16), win_ref[...])

    t_idx = lax.broadcasted_iota(jnp.int32, (TILE, 1), 0)
    pos = lax.rem(t_idx, roww)
    is_first = pos == 0
    is_last = pos == roww - 1

    def neighbours(zz):
        prev = jnp.where(is_first, 0.0, pltpu.roll(zz, 1, 0))
        nxt = jnp.where(is_last, 0.0, pltpu.roll(zz, TILE - 1, 0))
        return prev, nxt

    def shifted(lo, hi):
        zz = z_scr[:, lo:hi]
        prev, nxt = neighbours(zz)
        return zz + mu_ref[:, lo:hi] * (0.5 * (prev + nxt) - zz)

    r = shifted(0, W)
    k = shifted(W, 2 * W)
    v = shifted(2 * W, 3 * W)
    lora = shifted(3 * W, 3 * W + LORA_COLS)
    gl = shifted(3 * W + LORA_COLS, RWKV_COLS)
    v_scr[...] = v

    lora_in = jnp.concatenate([jnp.tanh(lora[:, :LANES]), lora[:, LANES:]], axis=1)
    pre = _mm(lora_in, wl_ref[...]) + bl_ref[...]
    gg_ref[0] = _mm(_sigmoid(gl), wg2_ref[...])

    bones = bones_ref[...]
    kraw = k * kk_ref[...]
    kk = kraw * lax.rsqrt(_head_sum(kraw * kraw, bones) + 1e-12)

    kd_sum = None
    for d in range(2):
        wlog = -_softplus(-pre[:, d * W:(d + 1) * W]) - 0.5
        lw = -jnp.exp(wlog)
        a = _sigmoid(pre[:, (2 + d) * W:(3 + d) * W])
        kd = k * (1.0 + (a - 1.0) * ka_ref[...])
        kd_sum = kd if kd_sum is None else kd_sum + kd
        b = kk * a
        tri = trif_ref[...] if d == 0 else trib_ref[...]
        l_hi, l_mid, l_lo = _split3(lw)
        c = _dot(tri, l_hi) + _dot(tri, l_mid) + _dot(tri, l_lo)
        c3 = c.reshape(CHUNKS_PER_TILE, CHUNK, W)
        mid = c3[:, CHUNK // 2:CHUNK // 2 + 1, :]
        end_row = CHUNK - 1 if d == 0 else 0
        end = c3[:, end_row:end_row + 1, :]
        cm = (c3 - mid).reshape(TILE, W)
        e_r = jnp.exp(cm)
        e_kap = jnp.exp(cm - lw)
        e_inv = jnp.exp(-cm)
        ops_scr[d, 0] = kk * e_kap
        ops_scr[d, 1] = r * e_r
        ops_scr[d, 2] = kd * e_inv
        ops_scr[d, 3] = b * e_inv
        for j, rv in enumerate((jnp.exp(end - mid), jnp.exp(end), jnp.exp(mid))):
            row_scr[d, j] = jnp.broadcast_to(rv, (CHUNKS_PER_TILE, SUBLANES, W)).reshape(
                CHUNKS_PER_TILE * SUBLANES, W)

    bonus_ref[0] = _head_sum(r * kd_sum * rk_ref[...], bones) * v

    gate_b = z_scr[:, RWKV_COLS:RWKV_COLS + CONV_WIDTH]
    u = z_scr[:, RWKV_COLS + CONV_WIDTH:RWKV_COLS + 2 * CONV_WIDTH] * z_scr[:, RWKV_COLS + 2 * CONV_WIDTH:IN_COLS]
    up, un = neighbours(u)
    oc = gate_b * (cw_ref[0:1, :] * up + cw_ref[1:2, :] * u + cw_ref[2:3, :] * un)
    oc = oc * lax.rsqrt(jnp.mean(oc * oc, axis=-1, keepdims=True) + NORM_EPS) * cg_ref[...]
    oconv_ref[0] = oc

    row = lax.broadcasted_iota(jnp.int32, (CHUNK, LANES), 0)
    col = lax.bitwise_and(lax.broadcasted_iota(jnp.int32, (CHUNK, LANES), 1), HEAD_SIZE - 1)
    eye = row == col
    masks = (jnp.concatenate([row > col, row >= col], axis=0),
             jnp.concatenate([row < col, row <= col], axis=0))
    mn_refs = (mnf_ref, mnb_ref)
    rq_refs = (rqf_ref, rqb_ref)

    def chunk_group(g, carry):
        cis = [g * GROUP + j for j in range(GROUP)]
        aligned = (lambda i, m: i * m) if isinstance(g, int) else (lambda i, m: pl.multiple_of(i * m, m))
        rows = [pl.ds(aligned(ci, CHUNK), CHUNK) for ci in cis]
        crow = [pl.ds(aligned(ci, SUBLANES), 1) for ci in cis]
        lsl = [slice(p * LANES, (p + 1) * LANES) for p in range(PAIRS)]
        cp = [(c, p) for c in range(GROUP) for p in range(PAIRS)]
        ch = [(c, p, d) for (c, p) in cp for d in range(2)]
        cp_of = {(c, p): i for i, (c, p) in enumerate(cp)}

        v_pair = [v_scr[rows[c], lsl[p]] for (c, p) in cp]
        vbd = [_block_diag_rhs(vp.astype(BF16)) for vp in v_pair]
        kap = [ops_scr[d, 0, rows[c], lsl[p]] for (c, p, d) in ch]
        rt = [ops_scr[d, 1, rows[c], lsl[p]] for (c, p, d) in ch]
        kt = [ops_scr[d, 2, rows[c], lsl[p]] for (c, p, d) in ch]
        bt = [ops_scr[d, 3, rows[c], lsl[p]] for (c, p, d) in ch]
        to_end = [row_scr[d, 0, crow[c], lsl[p]] for (c, p, d) in ch]
        e_mid = [row_scr[d, 2, crow[c], lsl[p]] for (c, p, d) in ch]
        n = len(ch)
        lhs = [jnp.concatenate([kap[i], rt[i]], axis=0).astype(BF16) for i in range(n)]
        a_k = [jnp.where(masks[ch[i][2]], _mm_nt(lhs[i], _block_diag_rhs(kt[i].astype(BF16))), 0.0)
               for i in range(n)]
        a_b = [jnp.where(masks[ch[i][2]], _mm_nt(lhs[i], _block_diag_rhs(bt[i].astype(BF16))), 0.0)
               for i in range(n)]
        a_kb = [x[0:CHUNK] for x in a_b]
        a_rb = [x[CHUNK:2 * CHUNK] for x in a_b]
        av = [_mm(a_k[i], vbd[cp_of[ch[i][:2]]]) for i in range(n)]
        t_inv = [jnp.where(eye, 1.0, 0.0) - x for x in a_kb]
        pw = [_mm(x, _block_diag_rhs(x.astype(BF16))) for x in a_kb]
        n_fac = 2
        while True:
            t_inv = [t + _mm(t, _block_diag_rhs(w.astype(BF16))) for t, w in zip(t_inv, pw)]
            n_fac *= 2
            if n_fac >= CHUNK:
                break
            pw = [_mm(w, _block_diag_rhs(w.astype(BF16))) for w in pw]
        rhs = [jnp.concatenate([_block_diag_rhs((kap[i] * e_mid[i]).astype(BF16)),
                                _block_diag_rhs(av[i][0:CHUNK].astype(BF16))], axis=1) for i in range(n)]
        pq = [_mm(t_inv[i], rhs[i]) for i in range(n)]
        rhs2 = [jnp.concatenate([_block_diag_rhs(x[:, 0:LANES].astype(BF16)),
                                 _block_diag_rhs(x[:, LANES:2 * LANES].astype(BF16))], axis=1) for x in pq]
        rb = [_mm(a_rb[i], rhs2[i]) for i in range(n)]
        for i, (c, p, d) in enumerate(ch):
            rq_refs[d][0, rows[c], lsl[p]] = rt[i] * e_mid[i] - rb[i][:, 0:LANES]
        y0d = [av[i][CHUNK:2 * CHUNK] - rb[i][:, LANES:2 * LANES] for i in range(n)]
        for i, (c, p, d) in enumerate(ch):
            if d == 0:
                y0_ref[0, rows[c], lsl[p]] = y0d[i] + y0d[i + 1]
        b_end = [(bt[i] * to_end[i]).astype(BF16) for i in range(n)]
        k_end = [(kt[i] * to_end[i]).astype(BF16) for i in range(n)]
        pq_b = [_mm(pq[i].T, b_end[i]) for i in range(n)]
        v_t = [vp.T for vp in v_pair]
        vk = [_mm(v_t[cp_of[ch[i][:2]]], k_end[i]) for i in range(n)]
        for i, (c, p, d) in enumerate(ch):
            mn_refs[d][0, cis[c], p, 0:HEAD_SIZE, :] = _compact(pq_b[i][0:LANES])
            mn_refs[d][0, cis[c], p, HEAD_SIZE:2 * HEAD_SIZE, :] = (
                _compact(vk[i]) - _compact(pq_b[i][LANES:2 * LANES]))
        return carry

    if CHUNKS_PER_TILE == GROUP:
        chunk_group(0, 0)
    else:
        lax.fori_loop(0, CHUNKS_PER_TILE // GROUP, chunk_group, 0)
    wtf_ref[0] = row_scr[0, 1]
    wtb_ref[0] = row_scr[1, 1]


def _phase1(x, mod, roww, lw):
    bsz, seq, _ = x.shape
    nt = seq // TILE
    nc = seq // CHUNK
    mod_map = (lambda b, t: (b, 0, 0)) if mod.shape[0] == bsz else (lambda b, t: (0, 0, 0))
    tok = lambda n: pl.BlockSpec((1, TILE, n), lambda b, t: (b, t, 0))
    mn_spec = pl.BlockSpec((1, CHUNKS_PER_TILE, PAIRS, 2 * HEAD_SIZE, LANES), lambda b, t: (b, t, 0, 0, 0))
    W = RWKV_WIDTH
    consts = (lw["norm1_g"], lw["w_in"], lw["mu"], lw["w_lora"], lw["b_lora"], lw["w_g2"], lw["k_k"],
              lw["k_a"], lw["r_k"], lw["conv_w"], lw["conv_gain"], lw["bones"], lw["tri_f"], lw["tri_b"])
    tok_shape = jax.ShapeDtypeStruct((bsz, seq, W), F32)
    mn_shape = jax.ShapeDtypeStruct((bsz, nc, PAIRS, 2 * HEAD_SIZE, LANES), F32)
    wt_spec = pl.BlockSpec((1, CHUNKS_PER_TILE * SUBLANES, W), lambda b, t: (b, t, 0))
    wt_shape = jax.ShapeDtypeStruct((bsz, nc * SUBLANES, W), F32)
    return pl.pallas_call(
        functools.partial(_phase1_body, roww),
        grid=(bsz, nt),
        in_specs=[tok(D_MODEL), pl.BlockSpec((1, 1, 6 * D_MODEL), mod_map)]
                 + [_const_spec(a.shape) for a in consts],
        out_specs=[tok(W)] * 6 + [mn_spec, mn_spec, wt_spec, wt_spec],
        out_shape=[tok_shape] * 6 + [mn_shape, mn_shape, wt_shape, wt_shape],
        scratch_shapes=[pltpu.VMEM((TILE, IN_COLS), F32),
                        pltpu.VMEM((2, 4, TILE, W), F32),
                        pltpu.VMEM((TILE, W), F32),
                        pltpu.VMEM((2, 3, CHUNKS_PER_TILE * SUBLANES, W), F32)],
        compiler_params=pltpu.CompilerParams(dimension_semantics=("parallel", "parallel"),
                                             vmem_limit_bytes=VMEM_LIMIT),
        name="phase1_chunk_summaries",
    )(x, mod, *consts)


def _phase2_body(nb, s0f_ref, s0b_ref, mnf_ref, mnb_ref, wtf_ref, wtb_ref, sf_ref, sb_ref, finf_ref,
                 finb_ref, st_scr):
    step = pl.program_id(1)

    @pl.when(step == 0)
    def _():
        st_scr[0] = s0f_ref[...]
        st_scr[1] = s0b_ref[...]

    def per_batch(b, carry):
        chains = [(d, p) for d in range(2) for p in range(PAIRS)]
        mn_refs = (mnf_ref, mnb_ref)
        wt_refs = (wtf_ref, wtb_ref)
        s_refs = (sf_ref, sb_ref)
        s = [st_scr[d, b, p] for (d, p) in chains]
        for i, (d, p) in enumerate(chains):
            s_refs[d][b, 0, p] = s[i]
        low = [_mm(s[i], _block_diag_rhs(mn_refs[d][b, 0, p, 0:HEAD_SIZE, :].astype(BF16)))
               for i, (d, p) in enumerate(chains)]
        for i, (d, p) in enumerate(chains):
            w_row = wt_refs[d][b, 0:1, p * LANES:(p + 1) * LANES]
            st_scr[d, b, p] = s[i] * w_row - low[i] + mn_refs[d][b, 0, p, HEAD_SIZE:2 * HEAD_SIZE, :]
        return carry

    lax.fori_loop(0, nb, per_batch, 0)

    @pl.when(step == pl.num_programs(1) - 1)
    def _():
        finf_ref[...] = st_scr[0]
        finb_ref[...] = st_scr[1]


def _phase2(s0f, s0b, mnf, mnb, wtf, wtb):
    bsz, nc = mnf.shape[0], mnf.shape[1]
    bb = min(bsz, SCAN_BATCH)
    assert bsz % bb == 0
    st_blk = (bb, PAIRS, HEAD_SIZE, LANES)
    mn_blk = (bb, 1, PAIRS, 2 * HEAD_SIZE, LANES)
    s_blk = (bb, 1, PAIRS, HEAD_SIZE, LANES)
    wt_blk = (bb, SUBLANES, RWKV_WIDTH)
    fwd = lambda g, i: (g, i, 0, 0, 0)
    bwd = lambda g, i: (g, nc - 1 - i, 0, 0, 0)
    full = pl.BlockSpec(st_blk, lambda g, i: (g, 0, 0, 0))
    s_all = jax.ShapeDtypeStruct((bsz, nc, PAIRS, HEAD_SIZE, LANES), F32)
    s_fin = jax.ShapeDtypeStruct((bsz, PAIRS, HEAD_SIZE, LANES), F32)
    return pl.pallas_call(
        functools.partial(_phase2_body, bb),
        grid=(bsz // bb, nc),
        in_specs=[full, full, pl.BlockSpec(mn_blk, fwd), pl.BlockSpec(mn_blk, bwd),
                  pl.BlockSpec(wt_blk, lambda g, i: (g, i, 0)),
                  pl.BlockSpec(wt_blk, lambda g, i: (g, nc - 1 - i, 0))],
        out_specs=[pl.BlockSpec(s_blk, fwd), pl.BlockSpec(s_blk, bwd), full, full],
        out_shape=[s_all, s_all, s_fin, s_fin],
        scratch_shapes=[pltpu.VMEM((2,) + st_blk, F32)],
        compiler_params=pltpu.CompilerParams(dimension_semantics=("parallel", "arbitrary"),
                                             vmem_limit_bytes=VMEM_LIMIT),
        name="phase2_state_scan",
    )(s0f, s0b, mnf, mnb, wtf, wtb)


def _phase3_body(x_ref, mod_ref, y0_ref, rqf_ref, rqb_ref, bonus_ref, gg_ref, oconv_ref, sf_ref, sb_ref,
                 lnw_ref, lnb_ref, bones_ref, wout_ref, g2_ref, wff1_ref, wff2_ref, gfin_ref,
                 o_ref, y_scr):
    W = RWKV_WIDTH
    for ci in range(CHUNKS_PER_TILE):
        rows = slice(ci * CHUNK, (ci + 1) * CHUNK)
        for p in range(PAIRS):
            ls = slice(p * LANES, (p + 1) * LANES)
            y_scr[rows, ls] = (y0_ref[0, rows, ls]
                               + _mm_nt(rqf_ref[0, rows, ls], _block_diag_rhs(sf_ref[0, ci, p].astype(BF16)))
                               + _mm_nt(rqb_ref[0, rows, ls], _block_diag_rhs(sb_ref[0, ci, p].astype(BF16))))

    y = y_scr[...]
    bones = bones_ref[...]
    inv_n = 1.0 / HEAD_SIZE
    y_mu = _head_sum(y, bones) * inv_n
    yc = y - y_mu
    y_var = _head_sum(yc * yc, bones) * inv_n
    yn = yc * lax.rsqrt(y_var + GN_EPS) * lnw_ref[...] + lnb_ref[...]
    o_rwkv = (yn + bonus_ref[0]) * gg_ref[0]

    mod = mod_ref[0]
    gate_a = mod[:, 2 * D_MODEL:3 * D_MODEL]
    shift_f = mod[:, 3 * D_MODEL:4 * D_MODEL]
    scale_f = mod[:, 4 * D_MODEL:5 * D_MODEL]
    gate_f = mod[:, 5 * D_MODEL:6 * D_MODEL]
    mix = (_dot(o_rwkv.astype(BF16), wout_ref[0:W, :])
           + _dot(oconv_ref[0].astype(BF16), wout_ref[W:D_MODEL, :]))
    x1 = x_ref[0] + gate_a * mix
    ms = jnp.mean(x1 * x1, axis=-1, keepdims=True)
    h2 = (x1 * lax.rsqrt(ms + NORM_EPS)) * g2_ref[...] * (1.0 + scale_f) + shift_f
    f1 = jnp.maximum(_dot(h2.astype(BF16), wff1_ref[...]), 0.0)
    f2 = _dot((f1 * f1).astype(BF16), wff2_ref[...])
    x2 = x1 + gate_f * f2
    ms2 = jnp.mean(x2 * x2, axis=-1, keepdims=True)
    o_ref[0] = (x2 * lax.rsqrt(ms2 + NORM_EPS)) * gfin_ref[...]


def _phase3(x, mod, p1, sf, sb, lw, final_g):
    bsz, seq, _ = x.shape
    nt = seq // TILE
    W = RWKV_WIDTH
    mod_map = (lambda b, t: (b, 0, 0)) if mod.shape[0] == bsz else (lambda b, t: (0, 0, 0))
    tok = lambda n: pl.BlockSpec((1, TILE, n), lambda b, t: (b, t, 0))
    s_spec = pl.BlockSpec((1, CHUNKS_PER_TILE, PAIRS, HEAD_SIZE, LANES), lambda b, t: (b, t, 0, 0, 0))
    consts = (lw["ln_x_w"], lw["ln_x_b"], lw["bones"], lw["w_out"], lw["norm2_g"], lw["w_ff1"], lw["w_ff2"],
              final_g)
    return pl.pallas_call(
        _phase3_body,
        grid=(bsz, nt),
        in_specs=[tok(D_MODEL), pl.BlockSpec((1, 1, 6 * D_MODEL), mod_map)] + [tok(W)] * 6 + [s_spec, s_spec]
                 + [_const_spec(a.shape) for a in consts],
        out_specs=tok(D_MODEL),
        out_shape=jax.ShapeDtypeStruct((bsz, seq, D_MODEL), F32),
        scratch_shapes=[pltpu.VMEM((TILE, W), F32)],
        compiler_params=pltpu.CompilerParams(dimension_semantics=("parallel", "parallel"),
                                             vmem_limit_bytes=VMEM_LIMIT),
        name="phase3_mix_mlp",
    )(x, mod, *p1, sf, sb, *consts)


def _block_diag(blocks):
    rows = sum(b.shape[0] for b in blocks)
    cols = sum(b.shape[1] for b in blocks)
    out = jnp.zeros((rows, cols), blocks[0].dtype)
    r = c = 0
    for b in blocks:
        out = lax.dynamic_update_slice(out, b, (r, c))
        r += b.shape[0]
        c += b.shape[1]
    return out


def _tri_constants():
    t = jnp.arange(TILE)
    same = (t[:, None] // CHUNK) == (t[None, :] // CHUNK)
    tri_f = (same & (t[None, :] <= t[:, None])).astype(BF16)
    tri_b = (same & (t[None, :] >= t[:, None])).astype(BF16)
    lane = jnp.arange(RWKV_WIDTH)
    bones = ((lane[:, None] // HEAD_SIZE) == (lane[None, :] // HEAD_SIZE)).astype(BF16)
    return tri_f, tri_b, bones


def _pair_compact(s):
    b = s.shape[0]
    s = s.reshape(b, PAIRS, 2, HEAD_SIZE, HEAD_SIZE)
    return jnp.swapaxes(s, 2, 3).reshape(b, PAIRS, HEAD_SIZE, LANES)


def _pair_expand(sc):
    b = sc.shape[0]
    s = sc.reshape(b, PAIRS, HEAD_SIZE, 2, HEAD_SIZE)
    return jnp.swapaxes(s, 2, 3).reshape(b, RWKV_HEADS, HEAD_SIZE, HEAD_SIZE)


def kernel(x_prompt, x_sample, c, state_rwkv_fwd, state_rwkv_bwd, c_ctx, w_mod, b_mod, norm1_g, w_in, mu_shift, w0_f, w_w2_f, w0_b, w_w2_b, a0_f, w_a2_f, a0_b, w_a2_b, w_g2, k_k, k_a, r_k, ln_x_w, ln_x_b, conv_w, conv_gain, w_out, norm2_g, w_ff1, w_ff2, final_g):
    depth = w_in.shape[0]
    b_ctx, seq_ctx, _ = x_prompt.shape
    b_lat, seq_lat, _ = x_sample.shape
    assert depth == 1
    assert seq_ctx == TILE and seq_lat % TILE == 0 and TILE % GRID_W == 0
    tri_f, tri_b, bones = _tri_constants()
    row = lambda a: a.reshape(1, -1)
    n_pad = (-(b_lat + 1)) % SUBLANES
    cvec = jnp.concatenate([c, c_ctx[None, :], jnp.zeros((n_pad, D_MODEL), F32)], axis=0)
    zero_state = jnp.zeros((b_ctx, PAIRS, HEAD_SIZE, LANES), F32)

    i = 0
    lw = dict(
        norm1_g=row(norm1_g[i]), w_in=w_in[i].astype(BF16), mu=row(mu_shift[i]),
        w_lora=_block_diag([w_w2_f[i], w_w2_b[i], w_a2_f[i], w_a2_b[i]]),
        b_lora=jnp.concatenate([w0_f[i], w0_b[i], a0_f[i], a0_b[i]]).reshape(1, -1),
        w_g2=w_g2[i], k_k=row(k_k[i]), k_a=row(k_a[i]), r_k=row(r_k[i]),
        conv_w=conv_w[i], conv_gain=row(conv_gain[i]), bones=bones, tri_f=tri_f, tri_b=tri_b,
        ln_x_w=row(ln_x_w[i]), ln_x_b=row(ln_x_b[i]), w_out=w_out[i].astype(BF16),
        norm2_g=row(norm2_g[i]), w_ff1=w_ff1[i].astype(BF16), w_ff2=w_ff2[i].astype(BF16))
    mod = _modulation(cvec, w_mod[i], row(b_mod[i]))
    mod_lat = mod[:b_lat].reshape(b_lat, 1, 6 * D_MODEL)
    mod_ctx = mod[b_lat:b_lat + 1].reshape(1, 1, 6 * D_MODEL)

    def run(x, mod_s, roww, s0f, s0b):
        *p1, mnf, mnb, wtf, wtb = _phase1(x, mod_s, roww, lw)
        sf, sb, fin_f, fin_b = _phase2(s0f, s0b, mnf, mnb, wtf, wtb)
        return _phase3(x, mod_s, p1, sf, sb, lw, row(final_g)), fin_f, fin_b

    y_ctx, s_f, s_b = run(x_prompt, mod_ctx, seq_ctx, zero_state, zero_state)
    y_lat, _, _ = run(x_sample, mod_lat, GRID_W, _pair_compact(state_rwkv_fwd[:, i]),
                      _pair_compact(state_rwkv_bwd[:, i]))
    return (y_ctx, y_lat, _pair_expand(s_f)[:, None], _pair_expand(s_b)[:, None])
```

```python
import functools

import jax
import jax.numpy as jnp
from jax import lax
from jax.experimental import pallas as pl
from jax.experimental.pallas import tpu as pltpu

D_MODEL = 1024
GRID_W = 64
RWKV_WIDTH = D_MODEL // 2
HEAD_SIZE = 64
RWKV_HEADS = RWKV_WIDTH // HEAD_SIZE
CONV_WIDTH = D_MODEL - RWKV_WIDTH
LORA_COLS = 256
GATE_LORA = 128
RWKV_COLS = 3 * RWKV_WIDTH + LORA_COLS + GATE_LORA
IN_COLS = RWKV_COLS + 3 * CONV_WIDTH
D_FF = 4 * D_MODEL
NORM_EPS = 1e-6
GN_EPS = HEAD_SIZE * 1e-5

LANES = 128
SUBLANES = 8
PAIRS = RWKV_WIDTH // LANES
CHUNK = 64
TILE = 256
CHUNKS_PER_TILE = TILE // CHUNK
GROUP = 4
SCAN_BATCH = 8
VMEM_LIMIT = 56 * 1024 * 1024

assert 2 * HEAD_SIZE == LANES and CHUNK == HEAD_SIZE

F32 = jnp.float32
BF16 = jnp.bfloat16
HI = lax.Precision.HIGHEST


def _dot(a, b):
    return jnp.dot(a, b, preferred_element_type=F32)


def _dot_hi(a, b):
    return jnp.dot(a, b, precision=HI, preferred_element_type=F32)


def _mm(a, b):
    return jnp.dot(a.astype(BF16), b.astype(BF16), preferred_element_type=F32)


def _mm_nt(a, b):
    return lax.dot_general(a.astype(BF16), b.astype(BF16), (((1,), (1,)), ((), ())),
                           preferred_element_type=F32)


def _mm_tn(a, b):
    return lax.dot_general(a.astype(BF16), b.astype(BF16), (((0,), (0,)), ((), ())),
                           preferred_element_type=F32)


def _split2(x):
    hi = x.astype(BF16)
    lo = (x - hi.astype(F32)).astype(BF16)
    return hi, lo


def _head_sum(x, bones):
    return _dot(x.astype(BF16), bones)


def _sigmoid(x):
    return 1.0 / (1.0 + jnp.exp(-x))


def _softplus(x):
    return jnp.maximum(x, 0.0) + jnp.log(1.0 + jnp.exp(-jnp.abs(x)))


def _block_diag_rhs(xc):
    first = lax.broadcasted_iota(jnp.int32, xc.shape, 1) < HEAD_SIZE
    zero = jnp.zeros_like(xc)
    return jnp.concatenate([jnp.where(first, xc, zero), jnp.where(first, zero, xc)], axis=0)


def _compact(xbd):
    first = lax.broadcasted_iota(jnp.int32, (HEAD_SIZE, LANES), 1) < HEAD_SIZE
    return jnp.where(first, xbd[0:HEAD_SIZE], xbd[HEAD_SIZE:2 * HEAD_SIZE])


def _const_spec(shape):
    nd = len(shape)
    return pl.BlockSpec(shape, lambda *_: (0,) * nd, pipeline_mode=pl.Buffered(1))


def _mod_body(c_ref, w_ref, b_ref, o_ref):
    cv = c_ref[...]
    s = cv * _sigmoid(cv)
    o_ref[...] = _dot_hi(s, w_ref[...]) + b_ref[...]


def _modulation(cvec, w_mod, b_mod):
    rows = cvec.shape[0]
    n = w_mod.shape[1]
    bn = 1536
    return pl.pallas_call(
        _mod_body,
        grid=(n // bn,),
        in_specs=[pl.BlockSpec((rows, D_MODEL), lambda j: (0, 0)),
                  pl.BlockSpec((D_MODEL, bn), lambda j: (0, j)),
                  pl.BlockSpec((1, bn), lambda j: (0, j))],
        out_specs=pl.BlockSpec((rows, bn), lambda j: (0, j)),
        out_shape=jax.ShapeDtypeStruct((rows, n), F32),
        compiler_params=pltpu.CompilerParams(dimension_semantics=("arbitrary",),
                                             vmem_limit_bytes=VMEM_LIMIT),
        name="modulation",
    )(cvec, w_mod, b_mod)


def _phase1_body(roww, x_ref, mod_ref, g1_ref, win_ref, mu_ref, wl_ref, bl_ref, wg2_ref, kk_ref,
                 ka_ref, rk_ref, cw_ref, cg_ref, bones_ref, trif_ref, trib_ref,
                 y0_ref, rqf_ref, rqb_ref, bonus_ref, gg_ref, oconv_ref, mnf_ref, mnb_ref,
                 wtf_ref, wtb_ref,
                 z_scr, ops_scr, v_scr, row_scr):
    W = RWKV_WIDTH
    x = x_ref[0]
    mod = mod_ref[0]
    shift_a = mod[:, 0:D_MODEL]
    scale_a = mod[:, D_MODEL:2 * D_MODEL]
    ms = jnp.mean(x * x, axis=-1, keepdims=True)
    h = (x * lax.rsqrt(ms + NORM_EPS)) * g1_ref[...] * (1.0 + scale_a) + shift_a
    h_bf = h.astype(BF16)
    z_scr[:, 0:RWKV_COLS] = _dot(h_bf, win_ref[:, 0:RWKV_COLS])

    t_idx = lax.broadcasted_iota(jnp.int32, (TILE, 1), 0)
    pos = lax.rem(t_idx, roww)
    is_first = pos == 0
    is_last = pos == roww - 1

    def neighbours(zz):
        prev = jnp.where(is_first, 0.0, pltpu.roll(zz, 1, 0))
        nxt = jnp.where(is_last, 0.0, pltpu.roll(zz, TILE - 1, 0))
        return prev, nxt

    def shifted(lo, hi):
        zz = z_scr[:, lo:hi]
        prev, nxt = neighbours(zz)
        mu = mu_ref[:, lo:hi]
        return (1.0 - mu) * zz + (0.5 * mu) * (prev + nxt)

    r = shifted(0, W)
    k = shifted(W, 2 * W)
    v = shifted(2 * W, 3 * W)
    lora = shifted(3 * W, 3 * W + LORA_COLS)
    v_scr[...] = v

    lora_in = jnp.concatenate([jnp.tanh(lora[:, :LANES]), lora[:, LANES:]], axis=1)
    pre = _mm(lora_in, wl_ref[...]) + bl_ref[...]
    z_scr[:, RWKV_COLS:IN_COLS] = _dot(h_bf, win_ref[:, RWKV_COLS:IN_COLS])

    bones = bones_ref[...]
    kraw = k * kk_ref[...]
    kk = kraw * lax.rsqrt(_head_sum(kraw * kraw, bones) + 1e-12)

    kd_sum = None
    for d in range(2):
        wlog = -_softplus(-pre[:, d * W:(d + 1) * W]) - 0.5
        lw = -jnp.exp(wlog)
        a = _sigmoid(pre[:, (2 + d) * W:(3 + d) * W])
        kd = k * (1.0 + (a - 1.0) * ka_ref[...])
        kd_sum = kd if kd_sum is None else kd_sum + kd
        b = kk * a
        tri = trif_ref[...] if d == 0 else trib_ref[...]
        l_hi, l_lo = _split2(lw)
        c = _dot(tri, l_hi) + _dot(tri, l_lo)
        c3 = c.reshape(CHUNKS_PER_TILE, CHUNK, W)
        mid = c3[:, CHUNK // 2:CHUNK // 2 + 1, :]
        end_row = CHUNK - 1 if d == 0 else 0
        end = c3[:, end_row:end_row + 1, :]
        cm = (c3 - mid).reshape(TILE, W)
        e_r = jnp.exp(cm)
        e_kap = jnp.exp(cm - lw)
        e_inv = jnp.exp(-cm)
        ops_scr[d, 0] = kk * e_kap
        ops_scr[d, 1] = r * e_r
        ops_scr[d, 2] = kd * e_inv
        ops_scr[d, 3] = b * e_inv
        for j, rv in enumerate((jnp.exp(end - mid), jnp.exp(end), jnp.exp(mid))):
            row_scr[d, j] = jnp.broadcast_to(rv, (CHUNKS_PER_TILE, SUBLANES, W)).reshape(
                CHUNKS_PER_TILE * SUBLANES, W)

    row = lax.broadcasted_iota(jnp.int32, (CHUNK, LANES), 0)
    col = lax.bitwise_and(lax.broadcasted_iota(jnp.int32, (CHUNK, LANES), 1), HEAD_SIZE - 1)
    eye = row == col
    masks = (jnp.concatenate([row > col, row >= col], axis=0),
             jnp.concatenate([row < col, row <= col], axis=0))
    mn_refs = (mnf_ref, mnb_ref)
    rq_refs = (rqf_ref, rqb_ref)

    def chunk_group(g, carry):
        cis = [g * GROUP + j for j in range(GROUP)]
        aligned = (lambda i, m: i * m) if isinstance(g, int) else (lambda i, m: pl.multiple_of(i * m, m))
        rows = [pl.ds(aligned(ci, CHUNK), CHUNK) for ci in cis]
        crow = [pl.ds(aligned(ci, SUBLANES), 1) for ci in cis]
        lsl = [slice(p * LANES, (p + 1) * LANES) for p in range(PAIRS)]
        cp = [(c, p) for c in range(GROUP) for p in range(PAIRS)]
        ch = [(c, p, d) for (c, p) in cp for d in range(2)]
        cp_of = {(c, p): i for i, (c, p) in enumerate(cp)}

        v_pair = [v_scr[rows[c], lsl[p]] for (c, p) in cp]
        vbd = [_block_diag_rhs(vp.astype(BF16)) for vp in v_pair]
        kap = [ops_scr[d, 0, rows[c], lsl[p]] for (c, p, d) in ch]
        rt = [ops_scr[d, 1, rows[c], lsl[p]] for (c, p, d) in ch]
        kt = [ops_scr[d, 2, rows[c], lsl[p]] for (c, p, d) in ch]
        bt = [ops_scr[d, 3, rows[c], lsl[p]] for (c, p, d) in ch]
        to_end = [row_scr[d, 0, crow[c], lsl[p]] for (c, p, d) in ch]
        e_mid = [row_scr[d, 2, crow[c], lsl[p]] for (c, p, d) in ch]
        n = len(ch)
        lhs = [jnp.concatenate([kap[i], rt[i]], axis=0).astype(BF16) for i in range(n)]
        a_all = [_mm_nt(lhs[i], jnp.concatenate([_block_diag_rhs(kt[i].astype(BF16)),
                                                 _block_diag_rhs(bt[i].astype(BF16))], axis=0))
                 for i in range(n)]
        a_k = [jnp.where(masks[ch[i][2]], a_all[i][:, 0:LANES], 0.0) for i in range(n)]
        a_b = [jnp.where(masks[ch[i][2]], a_all[i][:, LANES:2 * LANES], 0.0) for i in range(n)]
        a_kb = [x[0:CHUNK] for x in a_b]
        a_rb = [x[CHUNK:2 * CHUNK] for x in a_b]
        av2 = [_mm(jnp.concatenate([a_k[2 * j], a_k[2 * j + 1]], axis=0), vbd[j]) for j in range(len(cp))]
        av = [av2[i // 2][(i % 2) * 2 * CHUNK:(i % 2 + 1) * 2 * CHUNK] for i in range(n)]
        t_inv = [jnp.where(eye, 1.0, 0.0) - x for x in a_kb]
        pw = [_mm(x, _block_diag_rhs(x.astype(BF16))) for x in a_kb]
        n_fac = 2
        while 2 * n_fac < CHUNK:
            both = [_mm(jnp.concatenate([t, w], axis=0), _block_diag_rhs(w.astype(BF16)))
                    for t, w in zip(t_inv, pw)]
            t_inv = [t + x[0:CHUNK] for t, x in zip(t_inv, both)]
            pw = [x[CHUNK:2 * CHUNK] for x in both]
            n_fac *= 2
        t_inv = [t + _mm(t, _block_diag_rhs(w.astype(BF16))) for t, w in zip(t_inv, pw)]
        rhs = [jnp.concatenate([_block_diag_rhs((kap[i] * e_mid[i]).astype(BF16)),
                                _block_diag_rhs(av[i][0:CHUNK].astype(BF16))], axis=1) for i in range(n)]
        pq = [_mm(t_inv[i], rhs[i]) for i in range(n)]
        rhs2 = [jnp.concatenate([_block_diag_rhs(x[:, 0:LANES].astype(BF16)),
                                 _block_diag_rhs(x[:, LANES:2 * LANES].astype(BF16))], axis=1) for x in pq]
        rb = [_mm(a_rb[i], rhs2[i]) for i in range(n)]
        for i, (c, p, d) in enumerate(ch):
            rq_refs[d][0, rows[c], lsl[p]] = (rt[i] * e_mid[i] - rb[i][:, 0:LANES]).astype(BF16)
        y0d = [av[i][CHUNK:2 * CHUNK] - rb[i][:, LANES:2 * LANES] for i in range(n)]
        for i, (c, p, d) in enumerate(ch):
            if d == 0:
                y0_ref[0, rows[c], lsl[p]] = y0d[i] + y0d[i + 1]
        b_end = [(bt[i] * to_end[i]).astype(BF16) for i in range(n)]
        k_end = [(kt[i] * to_end[i]).astype(BF16) for i in range(n)]
        pq_b = [_mm(pq[i].T, b_end[i]) for i in range(n)]
        v_t = [vp.T for vp in v_pair]
        vk = [_mm(v_t[cp_of[ch[i][:2]]], k_end[i]) for i in range(n)]
        for i, (c, p, d) in enumerate(ch):
            mn_refs[d][0, cis[c], p, 0:HEAD_SIZE, :] = _compact(pq_b[i][0:LANES])
            mn_refs[d][0, cis[c], p, HEAD_SIZE:2 * HEAD_SIZE, :] = (
                _compact(vk[i]) - _compact(pq_b[i][LANES:2 * LANES]))
        return carry

    if CHUNKS_PER_TILE == GROUP:
        chunk_group(0, 0)
    else:
        lax.fori_loop(0, CHUNKS_PER_TILE // GROUP, chunk_group, 0)
    wtf_ref[0] = row_scr[0, 1]
    wtb_ref[0] = row_scr[1, 1]

    gl = shifted(3 * W + LORA_COLS, RWKV_COLS)
    gg_ref[0] = _mm(_sigmoid(gl), wg2_ref[...])
    bonus_ref[0] = _head_sum(r * kd_sum * rk_ref[...], bones) * v

    gate_b = z_scr[:, RWKV_COLS:RWKV_COLS + CONV_WIDTH]
    u = z_scr[:, RWKV_COLS + CONV_WIDTH:RWKV_COLS + 2 * CONV_WIDTH] * z_scr[:, RWKV_COLS + 2 * CONV_WIDTH:IN_COLS]
    up, un = neighbours(u)
    oc = gate_b * (cw_ref[0:1, :] * up + cw_ref[1:2, :] * u + cw_ref[2:3, :] * un)
    oc = oc * lax.rsqrt(jnp.mean(oc * oc, axis=-1, keepdims=True) + NORM_EPS) * cg_ref[...]
    oconv_ref[0] = oc.astype(BF16)


def _phase1(x, mod, roww, lw):
    bsz, seq, _ = x.shape
    nt = seq // TILE
    nc = seq // CHUNK
    mod_map = (lambda b, t: (b, 0, 0)) if mod.shape[0] == bsz else (lambda b, t: (0, 0, 0))
    tok = lambda n: pl.BlockSpec((1, TILE, n), lambda b, t: (b, t, 0))
    mn_spec = pl.BlockSpec((1, CHUNKS_PER_TILE, PAIRS, 2 * HEAD_SIZE, LANES), lambda b, t: (b, t, 0, 0, 0))
    W = RWKV_WIDTH
    consts = (lw["norm1_g"], lw["w_in"], lw["mu"], lw["w_lora"], lw["b_lora"], lw["w_g2"], lw["k_k"],
              lw["k_a"], lw["r_k"], lw["conv_w"], lw["conv_gain"], lw["bones"], lw["tri_f"], lw["tri_b"])
    tok_shape = jax.ShapeDtypeStruct((bsz, seq, W), F32)
    tok_bf16 = jax.ShapeDtypeStruct((bsz, seq, W), BF16)
    mn_shape = jax.ShapeDtypeStruct((bsz, nc, PAIRS, 2 * HEAD_SIZE, LANES), F32)
    wt_spec = pl.BlockSpec((1, CHUNKS_PER_TILE * SUBLANES, W), lambda b, t: (b, t, 0))
    wt_shape = jax.ShapeDtypeStruct((bsz, nc * SUBLANES, W), F32)
    return pl.pallas_call(
        functools.partial(_phase1_body, roww),
        grid=(bsz, nt),
        in_specs=[tok(D_MODEL), pl.BlockSpec((1, 1, 6 * D_MODEL), mod_map)]
                 + [_const_spec(a.shape) for a in consts],
        out_specs=[tok(W)] * 6 + [mn_spec, mn_spec, wt_spec, wt_spec],
        out_shape=[tok_shape, tok_bf16, tok_bf16, tok_shape, tok_shape, tok_bf16,
                   mn_shape, mn_shape, wt_shape, wt_shape],
        scratch_shapes=[pltpu.VMEM((TILE, IN_COLS), F32),
                        pltpu.VMEM((2, 4, TILE, W), F32),
                        pltpu.VMEM((TILE, W), F32),
                        pltpu.VMEM((2, 3, CHUNKS_PER_TILE * SUBLANES, W), F32)],
        compiler_params=pltpu.CompilerParams(dimension_semantics=("parallel", "parallel"),
                                             vmem_limit_bytes=VMEM_LIMIT),
        name="phase1_chunk_summaries",
    )(x, mod, *consts)


def _phase2_body(nb, s0f_ref, s0b_ref, mnf_ref, mnb_ref, wtf_ref, wtb_ref, sf_ref, sb_ref, finf_ref,
                 finb_ref, st_scr):
    step = pl.program_id(1)

    @pl.when(step == 0)
    def _():
        st_scr[0] = s0f_ref[...]
        st_scr[1] = s0b_ref[...]

    def per_batch(b, carry):
        chains = [(d, p) for d in range(2) for p in range(PAIRS)]
        mn_refs = (mnf_ref, mnb_ref)
        wt_refs = (wtf_ref, wtb_ref)
        s_refs = (sf_ref, sb_ref)
        s = [st_scr[d, b, p] for (d, p) in chains]
        for i, (d, p) in enumerate(chains):
            s_refs[d][b, 0, p] = s[i].astype(BF16)
        low = [_mm(s[i], _block_diag_rhs(mn_refs[d][b, 0, p, 0:HEAD_SIZE, :].astype(BF16)))
               for i, (d, p) in enumerate(chains)]
        for i, (d, p) in enumerate(chains):
            w_row = wt_refs[d][b, 0:1, p * LANES:(p + 1) * LANES]
            st_scr[d, b, p] = s[i] * w_row - low[i] + mn_refs[d][b, 0, p, HEAD_SIZE:2 * HEAD_SIZE, :]
        return carry

    lax.fori_loop(0, nb, per_batch, 0)

    @pl.when(step == pl.num_programs(1) - 1)
    def _():
        finf_ref[...] = st_scr[0]
        finb_ref[...] = st_scr[1]


def _phase2(s0f, s0b, mnf, mnb, wtf, wtb):
    bsz, nc = mnf.shape[0], mnf.shape[1]
    bb = min(bsz, SCAN_BATCH)
    assert bsz % bb == 0
    st_blk = (bb, PAIRS, HEAD_SIZE, LANES)
    mn_blk = (bb, 1, PAIRS, 2 * HEAD_SIZE, LANES)
    s_blk = (bb, 1, PAIRS, HEAD_SIZE, LANES)
    wt_blk = (bb, SUBLANES, RWKV_WIDTH)
    fwd = lambda g, i: (g, i, 0, 0, 0)
    bwd = lambda g, i: (g, nc - 1 - i, 0, 0, 0)
    full = pl.BlockSpec(st_blk, lambda g, i: (g, 0, 0, 0))
    s_all = jax.ShapeDtypeStruct((bsz, nc, PAIRS, HEAD_SIZE, LANES), BF16)
    s_fin = jax.ShapeDtypeStruct((bsz, PAIRS, HEAD_SIZE, LANES), F32)
    return pl.pallas_call(
        functools.partial(_phase2_body, bb),
        grid=(bsz // bb, nc),
        in_specs=[full, full, pl.BlockSpec(mn_blk, fwd), pl.BlockSpec(mn_blk, bwd),
                  pl.BlockSpec(wt_blk, lambda g, i: (g, i, 0)),
                  pl.BlockSpec(wt_blk, lambda g, i: (g, nc - 1 - i, 0))],
        out_specs=[pl.BlockSpec(s_blk, fwd), pl.BlockSpec(s_blk, bwd), full, full],
        out_shape=[s_all, s_all, s_fin, s_fin],
        scratch_shapes=[pltpu.VMEM((2,) + st_blk, F32)],
        compiler_params=pltpu.CompilerParams(dimension_semantics=("parallel", "arbitrary"),
                                             vmem_limit_bytes=VMEM_LIMIT),
        name="phase2_state_scan",
    )(s0f, s0b, mnf, mnb, wtf, wtb)


def _phase3_body(x_ref, mod_ref, y0_ref, rqf_ref, rqb_ref, bonus_ref, gg_ref, oconv_ref, sf_ref, sb_ref,
                 lnw_ref, lnb_ref, bones_ref, wout_ref, g2_ref, wff1_ref, wff2_ref, gfin_ref,
                 o_ref, y_scr):
    W = RWKV_WIDTH
    for ci in range(CHUNKS_PER_TILE):
        rows = slice(ci * CHUNK, (ci + 1) * CHUNK)
        for p in range(PAIRS):
            ls = slice(p * LANES, (p + 1) * LANES)
            y_scr[rows, ls] = (y0_ref[0, rows, ls]
                               + _mm_nt(rqf_ref[0, rows, ls], _block_diag_rhs(sf_ref[0, ci, p].astype(BF16)))
                               + _mm_nt(rqb_ref[0, rows, ls], _block_diag_rhs(sb_ref[0, ci, p].astype(BF16))))

    y = y_scr[...]
    bones = bones_ref[...]
    inv_n = 1.0 / HEAD_SIZE
    y_mu = _head_sum(y, bones) * inv_n
    yc = y - y_mu
    y_var = _head_sum(yc * yc, bones) * inv_n
    yn = yc * lax.rsqrt(y_var + GN_EPS) * lnw_ref[...] + lnb_ref[...]
    o_rwkv = (yn + bonus_ref[0]) * gg_ref[0]

    mod = mod_ref[0]
    gate_a = mod[:, 2 * D_MODEL:3 * D_MODEL]
    shift_f = mod[:, 3 * D_MODEL:4 * D_MODEL]
    scale_f = mod[:, 4 * D_MODEL:5 * D_MODEL]
    gate_f = mod[:, 5 * D_MODEL:6 * D_MODEL]
    mix = (_dot(o_rwkv.astype(BF16), wout_ref[0:W, :])
           + _dot(oconv_ref[0].astype(BF16), wout_ref[W:D_MODEL, :]))
    x1 = x_ref[0] + gate_a * mix
    ms = jnp.mean(x1 * x1, axis=-1, keepdims=True)
    h2 = (x1 * lax.rsqrt(ms + NORM_EPS)) * g2_ref[...] * (1.0 + scale_f) + shift_f
    f1 = jnp.maximum(_dot(h2.astype(BF16), wff1_ref[...]), 0.0)
    f2 = _dot((f1 * f1).astype(BF16), wff2_ref[...])
    x2 = x1 + gate_f * f2
    ms2 = jnp.mean(x2 * x2, axis=-1, keepdims=True)
    o_ref[0] = (x2 * lax.rsqrt(ms2 + NORM_EPS)) * gfin_ref[...]


def _phase3(x, mod, p1, sf, sb, lw, final_g):
    bsz, seq, _ = x.shape
    nt = seq // TILE
    W = RWKV_WIDTH
    mod_map = (lambda b, t: (b, 0, 0)) if mod.shape[0] == bsz else (lambda b, t: (0, 0, 0))
    tok = lambda n: pl.BlockSpec((1, TILE, n), lambda b, t: (b, t, 0))
    s_spec = pl.BlockSpec((1, CHUNKS_PER_TILE, PAIRS, HEAD_SIZE, LANES), lambda b, t: (b, t, 0, 0, 0))
    consts = (lw["ln_x_w"], lw["ln_x_b"], lw["bones"], lw["w_out"], lw["norm2_g"], lw["w_ff1"], lw["w_ff2"],
              final_g)
    return pl.pallas_call(
        _phase3_body,
        grid=(bsz, nt),
        in_specs=[tok(D_MODEL), pl.BlockSpec((1, 1, 6 * D_MODEL), mod_map)] + [tok(W)] * 6 + [s_spec, s_spec]
                 + [_const_spec(a.shape) for a in consts],
        out_specs=tok(D_MODEL),
        out_shape=jax.ShapeDtypeStruct((bsz, seq, D_MODEL), F32),
        scratch_shapes=[pltpu.VMEM((TILE, W), F32)],
        compiler_params=pltpu.CompilerParams(dimension_semantics=("parallel", "parallel"),
                                             vmem_limit_bytes=VMEM_LIMIT),
        name="phase3_mix_mlp",
    )(x, mod, *p1, sf, sb, *consts)


def _block_diag(blocks):
    rows = sum(b.shape[0] for b in blocks)
    cols = sum(b.shape[1] for b in blocks)
    out = jnp.zeros((rows, cols), blocks[0].dtype)
    r = c = 0
    for b in blocks:
        out = lax.dynamic_update_slice(out, b, (r, c))
        r += b.shape[0]
        c += b.shape[1]
    return out


def _tri_constants():
    t = jnp.arange(TILE)
    same = (t[:, None] // CHUNK) == (t[None, :] // CHUNK)
    tri_f = (same & (t[None, :] <= t[:, None])).astype(BF16)
    tri_b = (same & (t[None, :] >= t[:, None])).astype(BF16)
    lane = jnp.arange(RWKV_WIDTH)
    bones = ((lane[:, None] // HEAD_SIZE) == (lane[None, :] // HEAD_SIZE)).astype(BF16)
    return tri_f, tri_b, bones


def _pair_compact(s):
    b = s.shape[0]
    s = s.reshape(b, PAIRS, 2, HEAD_SIZE, HEAD_SIZE)
    return jnp.swapaxes(s, 2, 3).reshape(b, PAIRS, HEAD_SIZE, LANES)


def _pair_expand(sc):
    b = sc.shape[0]
    s = sc.reshape(b, PAIRS, HEAD_SIZE, 2, HEAD_SIZE)
    return jnp.swapaxes(s, 2, 3).reshape(b, RWKV_HEADS, HEAD_SIZE, HEAD_SIZE)


def kernel(x_prompt, x_sample, c, state_rwkv_fwd, state_rwkv_bwd, c_ctx, w_mod, b_mod, norm1_g, w_in, mu_shift, w0_f, w_w2_f, w0_b, w_w2_b, a0_f, w_a2_f, a0_b, w_a2_b, w_g2, k_k, k_a, r_k, ln_x_w, ln_x_b, conv_w, conv_gain, w_out, norm2_g, w_ff1, w_ff2, final_g):
    depth = w_in.shape[0]
    b_ctx, seq_ctx, _ = x_prompt.shape
    b_lat, seq_lat, _ = x_sample.shape
    assert depth == 1
    assert seq_ctx == TILE and seq_lat % TILE == 0 and TILE % GRID_W == 0
    tri_f, tri_b, bones = _tri_constants()
    row = lambda a: a.reshape(1, -1)
    n_pad = (-(b_lat + 1)) % SUBLANES
    cvec = jnp.concatenate([c, c_ctx[None, :], jnp.zeros((n_pad, D_MODEL), F32)], axis=0)
    zero_state = jnp.zeros((b_ctx, PAIRS, HEAD_SIZE, LANES), F32)

    i = 0
    lw = dict(
        norm1_g=row(norm1_g[i]), w_in=w_in[i].astype(BF16), mu=row(mu_shift[i]),
        w_lora=_block_diag([w_w2_f[i], w_w2_b[i], w_a2_f[i], w_a2_b[i]]),
        b_lora=jnp.concatenate([w0_f[i], w0_b[i], a0_f[i], a0_b[i]]).reshape(1, -1),
        w_g2=w_g2[i], k_k=row(k_k[i]), k_a=row(k_a[i]), r_k=row(r_k[i]),
        conv_w=conv_w[i], conv_gain=row(conv_gain[i]), bones=bones, tri_f=tri_f, tri_b=tri_b,
        ln_x_w=row(ln_x_w[i]), ln_x_b=row(ln_x_b[i]), w_out=w_out[i].astype(BF16),
        norm2_g=row(norm2_g[i]), w_ff1=w_ff1[i].astype(BF16), w_ff2=w_ff2[i].astype(BF16))
    mod = _modulation(cvec, w_mod[i], row(b_mod[i]))
    mod_lat = mod[:b_lat].reshape(b_lat, 1, 6 * D_MODEL)
    mod_ctx = mod[b_lat:b_lat + 1].reshape(1, 1, 6 * D_MODEL)

    def run(x, mod_s, roww, s0f, s0b):
        *p1, mnf, mnb, wtf, wtb = _phase1(x, mod_s, roww, lw)
        sf, sb, fin_f, fin_b = _phase2(s0f, s0b, mnf, mnb, wtf, wtb)
        return _phase3(x, mod_s, p1, sf, sb, lw, row(final_g)), fin_f, fin_b

    y_ctx, s_f, s_b = run(x_prompt, mod_ctx, seq_ctx, zero_state, zero_state)
    y_lat, _, _ = run(x_sample, mod_lat, GRID_W, _pair_compact(state_rwkv_fwd[:, i]),
                      _pair_compact(state_rwkv_bwd[:, i]))
    return (y_ctx, y_lat, _pair_expand(s_f)[:, None], _pair_expand(s_b)[:, None])
```

```python
import functools
import math

import jax
import jax.numpy as jnp
from jax import lax
from jax.experimental import pallas as pl
from jax.experimental.pallas import tpu as pltpu

D_MODEL = 1024
GRID_W = 64
RWKV_WIDTH = D_MODEL // 2
HEAD_SIZE = 64
RWKV_HEADS = RWKV_WIDTH // HEAD_SIZE
CONV_WIDTH = D_MODEL - RWKV_WIDTH
LORA_COLS = 256
GATE_LORA = 128
RWKV_COLS = 3 * RWKV_WIDTH + LORA_COLS + GATE_LORA
IN_COLS = RWKV_COLS + 3 * CONV_WIDTH
D_FF = 4 * D_MODEL
NORM_EPS = 1e-6
GN_EPS = HEAD_SIZE * 1e-5

LANES = 128
SUBLANES = 8
PAIRS = RWKV_WIDTH // LANES
CHUNK = 64
TILE = 256
CHUNKS_PER_TILE = TILE // CHUNK
DECAY_SCALE = math.exp(-0.5)
SCAN_BATCH = 8
VMEM_LIMIT = 56 * 1024 * 1024

assert 2 * HEAD_SIZE == LANES and CHUNK == HEAD_SIZE

F32 = jnp.float32
BF16 = jnp.bfloat16
HI = lax.Precision.HIGHEST


def _dot(a, b):
    return jnp.dot(a, b, preferred_element_type=F32)


def _dot_hi(a, b):
    return jnp.dot(a, b, precision=HI, preferred_element_type=F32)


def _mm(a, b):
    return jnp.dot(a.astype(BF16), b.astype(BF16), preferred_element_type=F32)


def _mm_nt(a, b):
    return lax.dot_general(a.astype(BF16), b.astype(BF16), (((1,), (1,)), ((), ())),
                           preferred_element_type=F32)


def _mm_tn(a, b):
    return lax.dot_general(a.astype(BF16), b.astype(BF16), (((0,), (0,)), ((), ())),
                           preferred_element_type=F32)


def _split2(x):
    hi = x.astype(BF16)
    lo = (x - hi.astype(F32)).astype(BF16)
    return hi, lo


def _head_sum(x, bones):
    return _dot(x.astype(BF16), bones)


def _sigmoid(x):
    return 1.0 / (1.0 + jnp.exp(-x))


def _block_diag_rhs(xc):
    first = lax.broadcasted_iota(jnp.int32, xc.shape, 1) < HEAD_SIZE
    zero = jnp.zeros_like(xc)
    return jnp.concatenate([jnp.where(first, xc, zero), jnp.where(first, zero, xc)], axis=0)


def _compact(xbd):
    first = lax.broadcasted_iota(jnp.int32, (HEAD_SIZE, LANES), 1) < HEAD_SIZE
    return jnp.where(first, xbd[0:HEAD_SIZE], xbd[HEAD_SIZE:2 * HEAD_SIZE])


def _const_spec(shape):
    nd = len(shape)
    return pl.BlockSpec(shape, lambda *_: (0,) * nd, pipeline_mode=pl.Buffered(1))


def _mod_body(c_ref, w_ref, b_ref, o_ref):
    cv = c_ref[...]
    s = cv * _sigmoid(cv)
    o_ref[...] = _dot_hi(s, w_ref[...]) + b_ref[...]


def _modulation(cvec, w_mod, b_mod):
    rows = cvec.shape[0]
    n = w_mod.shape[1]
    bn = 1536
    return pl.pallas_call(
        _mod_body,
        grid=(n // bn,),
        in_specs=[pl.BlockSpec((rows, D_MODEL), lambda j: (0, 0)),
                  pl.BlockSpec((D_MODEL, bn), lambda j: (0, j)),
                  pl.BlockSpec((1, bn), lambda j: (0, j))],
        out_specs=pl.BlockSpec((rows, bn), lambda j: (0, j)),
        out_shape=jax.ShapeDtypeStruct((rows, n), F32),
        compiler_params=pltpu.CompilerParams(dimension_semantics=("arbitrary",),
                                             vmem_limit_bytes=VMEM_LIMIT),
        name="modulation",
    )(cvec, w_mod, b_mod)


def _phase1_body(roww, x_ref, mod_ref, g1_ref, win_ref, mu_ref, wl_ref, bl_ref, wg2_ref, kk_ref,
                 ka_ref, rk_ref, cw_ref, cg_ref, bones_ref, trif_ref, trib_ref,
                 y0_ref, rqf_ref, rqb_ref, bonus_ref, gg_ref, oconv_ref, mlf_ref, mlb_ref,
                 nnf_ref, nnb_ref, wtf_ref, wtb_ref,
                 z_scr, opsf_scr, opsb_scr, v_scr, rowf_scr, rowb_scr):
    W = RWKV_WIDTH
    x = x_ref[0]
    mod = mod_ref[0]
    shift_a = mod[:, 0:D_MODEL]
    scale_a = mod[:, D_MODEL:2 * D_MODEL]
    ms = jnp.mean(x * x, axis=-1, keepdims=True)
    h = (x * lax.rsqrt(ms + NORM_EPS)) * g1_ref[...] * (1.0 + scale_a) + shift_a
    h_bf = h.astype(BF16)
    z_scr[:, 0:RWKV_COLS] = _dot(h_bf, win_ref[:, 0:RWKV_COLS])

    t_idx = lax.broadcasted_iota(jnp.int32, (TILE, 1), 0)
    pos = lax.rem(t_idx, roww)
    is_first = pos == 0
    is_last = pos == roww - 1

    def neighbours(zz):
        prev = jnp.where(is_first, 0.0, pltpu.roll(zz, 1, 0))
        nxt = jnp.where(is_last, 0.0, pltpu.roll(zz, TILE - 1, 0))
        return prev, nxt

    def shifted(lo, hi):
        zz = z_scr[:, lo:hi]
        prev, nxt = neighbours(zz)
        mu = mu_ref[:, lo:hi]
        return (1.0 - mu) * zz + (0.5 * mu) * (prev + nxt)

    r = shifted(0, W)
    k = shifted(W, 2 * W)
    v = shifted(2 * W, 3 * W)
    lora = shifted(3 * W, 3 * W + LORA_COLS)
    v_scr[...] = v

    lora_in = jnp.concatenate([jnp.tanh(lora[:, :LANES]), lora[:, LANES:]], axis=1)
    pre = _mm(lora_in, wl_ref[...]) + bl_ref[...]
    z_scr[:, RWKV_COLS:IN_COLS] = _dot(h_bf, win_ref[:, RWKV_COLS:IN_COLS])

    bones = bones_ref[...]
    kraw = k * kk_ref[...]
    kk = kraw * lax.rsqrt(_head_sum(kraw * kraw, bones) + 1e-12)

    log_decay, cum = [], []
    for d in range(2):
        lw = -DECAY_SCALE * _sigmoid(pre[:, d * W:(d + 1) * W])
        tri = trif_ref[...] if d == 0 else trib_ref[...]
        l_hi, l_lo = _split2(lw)
        log_decay.append(lw)
        cum.append(_dot(tri, l_hi) + _dot(tri, l_lo))

    def prepare(d):
        ops_scr, row_scr = (opsf_scr, rowf_scr) if d == 0 else (opsb_scr, rowb_scr)
        lw = log_decay[d]
        a = _sigmoid(pre[:, (2 + d) * W:(3 + d) * W])
        kd = k * (1.0 + (a - 1.0) * ka_ref[...])
        b = kk * a
        c3 = cum[d].reshape(CHUNKS_PER_TILE, CHUNK, W)
        mid = c3[:, CHUNK // 2:CHUNK // 2 + 1, :]
        end_row = CHUNK - 1 if d == 0 else 0
        end = c3[:, end_row:end_row + 1, :]
        cm = (c3 - mid).reshape(TILE, W)
        e_r = jnp.exp(cm)
        e_kap = jnp.exp(cm - lw)
        e_inv = jnp.exp(-cm)
        ops_scr[0] = kk * e_kap
        ops_scr[1] = r * e_r
        ops_scr[2] = kd * e_inv
        ops_scr[3] = b * e_inv
        for j, rv in enumerate((jnp.exp(end - mid), jnp.exp(end), jnp.exp(mid))):
            row_scr[j] = jnp.broadcast_to(rv, (CHUNKS_PER_TILE, SUBLANES, W)).reshape(
                CHUNKS_PER_TILE * SUBLANES, W)
        return kd

    row = lax.broadcasted_iota(jnp.int32, (CHUNK, LANES), 0)
    col = lax.bitwise_and(lax.broadcasted_iota(jnp.int32, (CHUNK, LANES), 1), HEAD_SIZE - 1)
    eye = row == col
    masks = (jnp.concatenate([row > col, row >= col], axis=0),
             jnp.concatenate([row < col, row <= col], axis=0))
    ml_refs = (mlf_ref, mlb_ref)
    nn_refs = (nnf_ref, nnb_ref)
    rq_refs = (rqf_ref, rqb_ref)

    def chains(d):
        ops_scr, row_scr = (opsf_scr, rowf_scr) if d == 0 else (opsb_scr, rowb_scr)
        rows = [pl.ds(ci * CHUNK, CHUNK) for ci in range(CHUNKS_PER_TILE)]
        crow = [pl.ds(ci * SUBLANES, 1) for ci in range(CHUNKS_PER_TILE)]
        lsl = [slice(p * LANES, (p + 1) * LANES) for p in range(PAIRS)]
        ch = [(c, p) for c in range(CHUNKS_PER_TILE) for p in range(PAIRS)]
        n = len(ch)

        v_pair = [v_scr[rows[c], lsl[p]] for (c, p) in ch]
        vbd = [_block_diag_rhs(vp.astype(BF16)) for vp in v_pair]
        kap = [ops_scr[0, rows[c], lsl[p]] for (c, p) in ch]
        rt = [ops_scr[1, rows[c], lsl[p]] for (c, p) in ch]
        kt = [ops_scr[2, rows[c], lsl[p]] for (c, p) in ch]
        bt = [ops_scr[3, rows[c], lsl[p]] for (c, p) in ch]
        to_end = [row_scr[0, crow[c], lsl[p]] for (c, p) in ch]
        e_mid = [row_scr[2, crow[c], lsl[p]] for (c, p) in ch]
        lhs = [jnp.concatenate([kap[i], rt[i]], axis=0).astype(BF16) for i in range(n)]
        a_all = [_mm_nt(lhs[i], jnp.concatenate([_block_diag_rhs(kt[i].astype(BF16)),
                                                 _block_diag_rhs(bt[i].astype(BF16))], axis=0))
                 for i in range(n)]
        a_k = [jnp.where(masks[d], a_all[i][:, 0:LANES], 0.0) for i in range(n)]
        a_b = [jnp.where(masks[d], a_all[i][:, LANES:2 * LANES], 0.0) for i in range(n)]
        a_kb = [x[0:CHUNK] for x in a_b]
        a_rb = [x[CHUNK:2 * CHUNK] for x in a_b]
        av = [_mm(a_k[i], vbd[i]) for i in range(n)]
        t_inv = [jnp.where(eye, 1.0, 0.0) - x for x in a_kb]
        pw = [_mm(x, _block_diag_rhs(x.astype(BF16))) for x in a_kb]
        n_fac = 2
        while 2 * n_fac < CHUNK:
            both = [_mm(jnp.concatenate([t, w], axis=0), _block_diag_rhs(w.astype(BF16)))
                    for t, w in zip(t_inv, pw)]
            t_inv = [t + x[0:CHUNK] for t, x in zip(t_inv, both)]
            pw = [x[CHUNK:2 * CHUNK] for x in both]
            n_fac *= 2
        t_inv = [t + _mm(t, _block_diag_rhs(w.astype(BF16))) for t, w in zip(t_inv, pw)]
        rhs = [jnp.concatenate([_block_diag_rhs((kap[i] * e_mid[i]).astype(BF16)),
                                _block_diag_rhs(av[i][0:CHUNK].astype(BF16))], axis=1) for i in range(n)]
        pq = [_mm(t_inv[i], rhs[i]) for i in range(n)]
        rhs2 = [jnp.concatenate([_block_diag_rhs(x[:, 0:LANES].astype(BF16)),
                                 _block_diag_rhs(x[:, LANES:2 * LANES].astype(BF16))], axis=1) for x in pq]
        rb = [_mm(a_rb[i], rhs2[i]) for i in range(n)]
        for i, (c, p) in enumerate(ch):
            rq_refs[d][0, rows[c], lsl[p]] = (rt[i] * e_mid[i] - rb[i][:, 0:LANES]).astype(BF16)
        y0d = [av[i][CHUNK:2 * CHUNK] - rb[i][:, LANES:2 * LANES] for i in range(n)]
        b_end = [(bt[i] * to_end[i]).astype(BF16) for i in range(n)]
        k_end = [(kt[i] * to_end[i]).astype(BF16) for i in range(n)]
        pq_b = [_mm(pq[i].T, b_end[i]) for i in range(n)]
        vk = [_mm(v_pair[i].T, k_end[i]) for i in range(n)]
        for i, (c, p) in enumerate(ch):
            ml_refs[d][0, c, p] = _compact(pq_b[i][0:LANES]).astype(BF16)
            nn_refs[d][0, c, p] = _compact(vk[i]) - _compact(pq_b[i][LANES:2 * LANES])
        return y0d

    kd_f = prepare(0)
    y0_f = chains(0)
    kd_b = prepare(1)
    y0_b = chains(1)
    for i in range(len(y0_f)):
        ci, p = divmod(i, PAIRS)
        y0_ref[0, ci * CHUNK:(ci + 1) * CHUNK, p * LANES:(p + 1) * LANES] = y0_f[i] + y0_b[i]
    wtf_ref[0] = rowf_scr[1]
    wtb_ref[0] = rowb_scr[1]

    gl = shifted(3 * W + LORA_COLS, RWKV_COLS)
    gg_ref[0] = _mm(_sigmoid(gl), wg2_ref[...])
    bonus_ref[0] = _head_sum(r * (kd_f + kd_b) * rk_ref[...], bones) * v

    gate_b = z_scr[:, RWKV_COLS:RWKV_COLS + CONV_WIDTH]
    u = z_scr[:, RWKV_COLS + CONV_WIDTH:RWKV_COLS + 2 * CONV_WIDTH] * z_scr[:, RWKV_COLS + 2 * CONV_WIDTH:IN_COLS]
    up, un = neighbours(u)
    oc = gate_b * (cw_ref[0:1, :] * up + cw_ref[1:2, :] * u + cw_ref[2:3, :] * un)
    oc = oc * lax.rsqrt(jnp.mean(oc * oc, axis=-1, keepdims=True) + NORM_EPS) * cg_ref[...]
    oconv_ref[0] = oc.astype(BF16)


def _phase1(x, mod, roww, lw):
    bsz, seq, _ = x.shape
    nt = seq // TILE
    nc = seq // CHUNK
    mod_map = (lambda b, t: (b, 0, 0)) if mod.shape[0] == bsz else (lambda b, t: (0, 0, 0))
    tok = lambda n: pl.BlockSpec((1, TILE, n), lambda b, t: (b, t, 0))
    mn_spec = pl.BlockSpec((1, CHUNKS_PER_TILE, PAIRS, HEAD_SIZE, LANES), lambda b, t: (b, t, 0, 0, 0))
    W = RWKV_WIDTH
    consts = (lw["norm1_g"], lw["w_in"], lw["mu"], lw["w_lora"], lw["b_lora"], lw["w_g2"], lw["k_k"],
              lw["k_a"], lw["r_k"], lw["conv_w"], lw["conv_gain"], lw["bones"], lw["tri_f"], lw["tri_b"])
    tok_shape = jax.ShapeDtypeStruct((bsz, seq, W), F32)
    tok_bf16 = jax.ShapeDtypeStruct((bsz, seq, W), BF16)
    ml_shape = jax.ShapeDtypeStruct((bsz, nc, PAIRS, HEAD_SIZE, LANES), BF16)
    nn_shape = jax.ShapeDtypeStruct((bsz, nc, PAIRS, HEAD_SIZE, LANES), F32)
    wt_spec = pl.BlockSpec((1, CHUNKS_PER_TILE * SUBLANES, W), lambda b, t: (b, t, 0))
    wt_shape = jax.ShapeDtypeStruct((bsz, nc * SUBLANES, W), F32)
    return pl.pallas_call(
        functools.partial(_phase1_body, roww),
        grid=(bsz, nt),
        in_specs=[tok(D_MODEL), pl.BlockSpec((1, 1, 6 * D_MODEL), mod_map)]
                 + [_const_spec(a.shape) for a in consts],
        out_specs=[tok(W)] * 6 + [mn_spec] * 4 + [wt_spec, wt_spec],
        out_shape=[tok_shape, tok_bf16, tok_bf16, tok_shape, tok_shape, tok_bf16,
                   ml_shape, ml_shape, nn_shape, nn_shape, wt_shape, wt_shape],
        scratch_shapes=[pltpu.VMEM((TILE, IN_COLS), F32),
                        pltpu.VMEM((4, TILE, W), F32),
                        pltpu.VMEM((4, TILE, W), F32),
                        pltpu.VMEM((TILE, W), F32),
                        pltpu.VMEM((3, CHUNKS_PER_TILE * SUBLANES, W), F32),
                        pltpu.VMEM((3, CHUNKS_PER_TILE * SUBLANES, W), F32)],
        compiler_params=pltpu.CompilerParams(dimension_semantics=("parallel", "parallel"),
                                             vmem_limit_bytes=VMEM_LIMIT),
        name="phase1_chunk_summaries",
    )(x, mod, *consts)


def _phase2_body(nb, s0f_ref, s0b_ref, mlf_ref, mlb_ref, nnf_ref, nnb_ref, wtf_ref, wtb_ref,
                 sf_ref, sb_ref, finf_ref, finb_ref, st_scr):
    step = pl.program_id(1)

    @pl.when(step == 0)
    def _():
        st_scr[0] = s0f_ref[...]
        st_scr[1] = s0b_ref[...]

    chains = [(b, d, p) for b in range(nb) for d in range(2) for p in range(PAIRS)]
    ml_refs = (mlf_ref, mlb_ref)
    nn_refs = (nnf_ref, nnb_ref)
    wt_refs = (wtf_ref, wtb_ref)
    s_refs = (sf_ref, sb_ref)
    s = [st_scr[d, b, p] for (b, d, p) in chains]
    for i, (b, d, p) in enumerate(chains):
        s_refs[d][b, 0, p] = s[i].astype(BF16)
    low = [_mm(s[i], _block_diag_rhs(ml_refs[d][b, 0, p])) for i, (b, d, p) in enumerate(chains)]
    for i, (b, d, p) in enumerate(chains):
        w_row = wt_refs[d][b, 0:1, p * LANES:(p + 1) * LANES]
        st_scr[d, b, p] = s[i] * w_row - low[i] + nn_refs[d][b, 0, p]

    @pl.when(step == pl.num_programs(1) - 1)
    def _():
        finf_ref[...] = st_scr[0]
        finb_ref[...] = st_scr[1]


def _phase2(s0f, s0b, mlf, mlb, nnf, nnb, wtf, wtb):
    bsz, nc = mlf.shape[0], mlf.shape[1]
    bb = min(bsz, SCAN_BATCH)
    assert bsz % bb == 0
    st_blk = (bb, PAIRS, HEAD_SIZE, LANES)
    mn_blk = (bb, 1, PAIRS, HEAD_SIZE, LANES)
    s_blk = (bb, 1, PAIRS, HEAD_SIZE, LANES)
    wt_blk = (bb, SUBLANES, RWKV_WIDTH)
    fwd = lambda g, i: (g, i, 0, 0, 0)
    bwd = lambda g, i: (g, nc - 1 - i, 0, 0, 0)
    full = pl.BlockSpec(st_blk, lambda g, i: (g, 0, 0, 0))
    s_all = jax.ShapeDtypeStruct((bsz, nc, PAIRS, HEAD_SIZE, LANES), BF16)
    s_fin = jax.ShapeDtypeStruct((bsz, PAIRS, HEAD_SIZE, LANES), F32)
    return pl.pallas_call(
        functools.partial(_phase2_body, bb),
        grid=(bsz // bb, nc),
        in_specs=[full, full, pl.BlockSpec(mn_blk, fwd), pl.BlockSpec(mn_blk, bwd),
                  pl.BlockSpec(mn_blk, fwd), pl.BlockSpec(mn_blk, bwd),
                  pl.BlockSpec(wt_blk, lambda g, i: (g, i, 0)),
                  pl.BlockSpec(wt_blk, lambda g, i: (g, nc - 1 - i, 0))],
        out_specs=[pl.BlockSpec(s_blk, fwd), pl.BlockSpec(s_blk, bwd), full, full],
        out_shape=[s_all, s_all, s_fin, s_fin],
        scratch_shapes=[pltpu.VMEM((2,) + st_blk, F32)],
        compiler_params=pltpu.CompilerParams(dimension_semantics=("parallel", "arbitrary"),
                                             vmem_limit_bytes=VMEM_LIMIT),
        name="phase2_state_scan",
    )(s0f, s0b, mlf, mlb, nnf, nnb, wtf, wtb)


def _phase3_body(x_ref, mod_ref, y0_ref, rqf_ref, rqb_ref, bonus_ref, gg_ref, oconv_ref, sf_ref, sb_ref,
                 lnw_ref, lnb_ref, bones_ref, wout_ref, g2_ref, wff1_ref, wff2_ref, gfin_ref,
                 o_ref, y_scr):
    W = RWKV_WIDTH
    for ci in range(CHUNKS_PER_TILE):
        rows = slice(ci * CHUNK, (ci + 1) * CHUNK)
        for p in range(PAIRS):
            ls = slice(p * LANES, (p + 1) * LANES)
            y_scr[rows, ls] = (y0_ref[0, rows, ls]
                               + _mm_nt(rqf_ref[0, rows, ls], _block_diag_rhs(sf_ref[0, ci, p].astype(BF16)))
                               + _mm_nt(rqb_ref[0, rows, ls], _block_diag_rhs(sb_ref[0, ci, p].astype(BF16))))

    y = y_scr[...]
    bones = bones_ref[...]
    inv_n = 1.0 / HEAD_SIZE
    y_mu = _head_sum(y, bones) * inv_n
    yc = y - y_mu
    y_var = _head_sum(yc * yc, bones) * inv_n
    yn = yc * lax.rsqrt(y_var + GN_EPS) * lnw_ref[...] + lnb_ref[...]
    o_rwkv = (yn + bonus_ref[0]) * gg_ref[0]

    mod = mod_ref[0]
    gate_a = mod[:, 2 * D_MODEL:3 * D_MODEL]
    shift_f = mod[:, 3 * D_MODEL:4 * D_MODEL]
    scale_f = mod[:, 4 * D_MODEL:5 * D_MODEL]
    gate_f = mod[:, 5 * D_MODEL:6 * D_MODEL]
    mix = (_dot(o_rwkv.astype(BF16), wout_ref[0:W, :])
           + _dot(oconv_ref[0].astype(BF16), wout_ref[W:D_MODEL, :]))
    x1 = x_ref[0] + gate_a * mix
    ms = jnp.mean(x1 * x1, axis=-1, keepdims=True)
    h2 = (x1 * lax.rsqrt(ms + NORM_EPS)) * g2_ref[...] * (1.0 + scale_f) + shift_f
    f1 = jnp.maximum(_dot(h2.astype(BF16), wff1_ref[...]), 0.0)
    f2 = _dot((f1 * f1).astype(BF16), wff2_ref[...])
    x2 = x1 + gate_f * f2
    ms2 = jnp.mean(x2 * x2, axis=-1, keepdims=True)
    o_ref[0] = (x2 * lax.rsqrt(ms2 + NORM_EPS)) * gfin_ref[...]


def _phase3(x, mod, p1, sf, sb, lw, final_g):
    bsz, seq, _ = x.shape
    nt = seq // TILE
    W = RWKV_WIDTH
    mod_map = (lambda b, t: (b, 0, 0)) if mod.shape[0] == bsz else (lambda b, t: (0, 0, 0))
    tok = lambda n: pl.BlockSpec((1, TILE, n), lambda b, t: (b, t, 0))
    s_spec = pl.BlockSpec((1, CHUNKS_PER_TILE, PAIRS, HEAD_SIZE, LANES), lambda b, t: (b, t, 0, 0, 0))
    consts = (lw["ln_x_w"], lw["ln_x_b"], lw["bones"], lw["w_out"], lw["norm2_g"], lw["w_ff1"], lw["w_ff2"],
              final_g)
    return pl.pallas_call(
        _phase3_body,
        grid=(bsz, nt),
        in_specs=[tok(D_MODEL), pl.BlockSpec((1, 1, 6 * D_MODEL), mod_map)] + [tok(W)] * 6 + [s_spec, s_spec]
                 + [_const_spec(a.shape) for a in consts],
        out_specs=tok(D_MODEL),
        out_shape=jax.ShapeDtypeStruct((bsz, seq, D_MODEL), F32),
        scratch_shapes=[pltpu.VMEM((TILE, W), F32)],
        compiler_params=pltpu.CompilerParams(dimension_semantics=("parallel", "parallel"),
                                             vmem_limit_bytes=VMEM_LIMIT),
        name="phase3_mix_mlp",
    )(x, mod, *p1, sf, sb, *consts)


def _block_diag(blocks):
    rows = sum(b.shape[0] for b in blocks)
    cols = sum(b.shape[1] for b in blocks)
    out = jnp.zeros((rows, cols), blocks[0].dtype)
    r = c = 0
    for b in blocks:
        out = lax.dynamic_update_slice(out, b, (r, c))
        r += b.shape[0]
        c += b.shape[1]
    return out


def _tri_constants():
    t = jnp.arange(TILE)
    same = (t[:, None] // CHUNK) == (t[None, :] // CHUNK)
    tri_f = (same & (t[None, :] <= t[:, None])).astype(BF16)
    tri_b = (same & (t[None, :] >= t[:, None])).astype(BF16)
    lane = jnp.arange(RWKV_WIDTH)
    bones = ((lane[:, None] // HEAD_SIZE) == (lane[None, :] // HEAD_SIZE)).astype(BF16)
    return tri_f, tri_b, bones


def _pair_compact(s):
    b = s.shape[0]
    s = s.reshape(b, PAIRS, 2, HEAD_SIZE, HEAD_SIZE)
    return jnp.swapaxes(s, 2, 3).reshape(b, PAIRS, HEAD_SIZE, LANES)


def _pair_expand(sc):
    b = sc.shape[0]
    s = sc.reshape(b, PAIRS, HEAD_SIZE, 2, HEAD_SIZE)
    return jnp.swapaxes(s, 2, 3).reshape(b, RWKV_HEADS, HEAD_SIZE, HEAD_SIZE)


def kernel(x_prompt, x_sample, c, state_rwkv_fwd, state_rwkv_bwd, c_ctx, w_mod, b_mod, norm1_g, w_in, mu_shift, w0_f, w_w2_f, w0_b, w_w2_b, a0_f, w_a2_f, a0_b, w_a2_b, w_g2, k_k, k_a, r_k, ln_x_w, ln_x_b, conv_w, conv_gain, w_out, norm2_g, w_ff1, w_ff2, final_g):
    depth = w_in.shape[0]
    b_ctx, seq_ctx, _ = x_prompt.shape
    b_lat, seq_lat, _ = x_sample.shape
    assert depth == 1
    assert seq_ctx == TILE and seq_lat % TILE == 0 and TILE % GRID_W == 0
    tri_f, tri_b, bones = _tri_constants()
    row = lambda a: a.reshape(1, -1)
    n_pad = (-(b_lat + 1)) % SUBLANES
    cvec = jnp.concatenate([c, c_ctx[None, :], jnp.zeros((n_pad, D_MODEL), F32)], axis=0)
    zero_state = jnp.zeros((b_ctx, PAIRS, HEAD_SIZE, LANES), F32)

    i = 0
    lw = dict(
        norm1_g=row(norm1_g[i]), w_in=w_in[i].astype(BF16), mu=row(mu_shift[i]),
        w_lora=_block_diag([w_w2_f[i], w_w2_b[i], w_a2_f[i], w_a2_b[i]]),
        b_lora=jnp.concatenate([w0_f[i], w0_b[i], a0_f[i], a0_b[i]]).reshape(1, -1),
        w_g2=w_g2[i], k_k=row(k_k[i]), k_a=row(k_a[i]), r_k=row(r_k[i]),
        conv_w=conv_w[i], conv_gain=row(conv_gain[i]), bones=bones, tri_f=tri_f, tri_b=tri_b,
        ln_x_w=row(ln_x_w[i]), ln_x_b=row(ln_x_b[i]), w_out=w_out[i].astype(BF16),
        norm2_g=row(norm2_g[i]), w_ff1=w_ff1[i].astype(BF16), w_ff2=w_ff2[i].astype(BF16))
    mod = _modulation(cvec, w_mod[i], row(b_mod[i]))
    mod_lat = mod[:b_lat].reshape(b_lat, 1, 6 * D_MODEL)
    mod_ctx = mod[b_lat:b_lat + 1].reshape(1, 1, 6 * D_MODEL)

    def run(x, mod_s, roww, s0f, s0b):
        *p1, mlf, mlb, nnf, nnb, wtf, wtb = _phase1(x, mod_s, roww, lw)
        sf, sb, fin_f, fin_b = _phase2(s0f, s0b, mlf, mlb, nnf, nnb, wtf, wtb)
        return _phase3(x, mod_s, p1, sf, sb, lw, row(final_g)), fin_f, fin_b

    y_ctx, s_f, s_b = run(x_prompt, mod_ctx, seq_ctx, zero_state, zero_state)
    y_lat, _, _ = run(x_sample, mod_lat, GRID_W, _pair_compact(state_rwkv_fwd[:, i]),
                      _pair_compact(state_rwkv_bwd[:, i]))
    return (y_ctx, y_lat, _pair_expand(s_f)[:, None], _pair_expand(s_b)[:, None])
```

```python
import functools
import math

import jax
import jax.numpy as jnp
from jax import lax
from jax.experimental import pallas as pl
from jax.experimental.pallas import tpu as pltpu

D_MODEL = 1024
GRID_W = 64
RWKV_WIDTH = D_MODEL // 2
HEAD_SIZE = 64
RWKV_HEADS = RWKV_WIDTH // HEAD_SIZE
CONV_WIDTH = D_MODEL - RWKV_WIDTH
LORA_COLS = 256
GATE_LORA = 128
RWKV_COLS = 3 * RWKV_WIDTH + LORA_COLS + GATE_LORA
IN_COLS = RWKV_COLS + 3 * CONV_WIDTH
D_FF = 4 * D_MODEL
NORM_EPS = 1e-6
GN_EPS = HEAD_SIZE * 1e-5

LANES = 128
SUBLANES = 8
PAIRS = RWKV_WIDTH // LANES
CHUNK = 64
TILE = 256
CHUNKS_PER_TILE = TILE // CHUNK
DECAY_SCALE = math.exp(-0.5)
P3_TILE = 512
P3_ROWS = 256
FF_CHUNK = 1024
SCAN_CHUNKS = 4
SCAN_BATCH = 8
VMEM_LIMIT = 56 * 1024 * 1024

assert 2 * HEAD_SIZE == LANES and CHUNK == HEAD_SIZE

F32 = jnp.float32
BF16 = jnp.bfloat16
HI = lax.Precision.HIGHEST


def _dot(a, b):
    return jnp.dot(a, b, preferred_element_type=F32)


def _dot_hi(a, b):
    return jnp.dot(a, b, precision=HI, preferred_element_type=F32)


def _mm(a, b):
    return jnp.dot(a.astype(BF16), b.astype(BF16), preferred_element_type=F32)


def _mm_nt(a, b):
    return lax.dot_general(a.astype(BF16), b.astype(BF16), (((1,), (1,)), ((), ())),
                           preferred_element_type=F32)


def _mm_tn(a, b):
    return lax.dot_general(a.astype(BF16), b.astype(BF16), (((0,), (0,)), ((), ())),
                           preferred_element_type=F32)


def _split2(x):
    hi = x.astype(BF16)
    lo = (x - hi.astype(F32)).astype(BF16)
    return hi, lo


def _head_sum(x, bones):
    return _dot(x.astype(BF16), bones)


def _sigmoid(x):
    return 1.0 / (1.0 + jnp.exp(-x))


def _block_diag_rhs(xc):
    first = lax.broadcasted_iota(jnp.int32, xc.shape, 1) < HEAD_SIZE
    zero = jnp.zeros_like(xc)
    return jnp.concatenate([jnp.where(first, xc, zero), jnp.where(first, zero, xc)], axis=0)


def _compact(xbd):
    first = lax.broadcasted_iota(jnp.int32, (HEAD_SIZE, LANES), 1) < HEAD_SIZE
    return jnp.where(first, xbd[0:HEAD_SIZE], xbd[HEAD_SIZE:2 * HEAD_SIZE])


def _const_spec(shape):
    nd = len(shape)
    return pl.BlockSpec(shape, lambda *_: (0,) * nd, pipeline_mode=pl.Buffered(1))


def _mod_body(c_ref, w_ref, b_ref, o_ref):
    cv = c_ref[...]
    s = cv * _sigmoid(cv)
    o_ref[...] = _dot_hi(s, w_ref[...]) + b_ref[...]


def _modulation(cvec, w_mod, b_mod):
    rows = cvec.shape[0]
    n = w_mod.shape[1]
    bn = 1536
    return pl.pallas_call(
        _mod_body,
        grid=(n // bn,),
        in_specs=[pl.BlockSpec((rows, D_MODEL), lambda j: (0, 0)),
                  pl.BlockSpec((D_MODEL, bn), lambda j: (0, j)),
                  pl.BlockSpec((1, bn), lambda j: (0, j))],
        out_specs=pl.BlockSpec((rows, bn), lambda j: (0, j)),
        out_shape=jax.ShapeDtypeStruct((rows, n), F32),
        compiler_params=pltpu.CompilerParams(dimension_semantics=("arbitrary",),
                                             vmem_limit_bytes=VMEM_LIMIT),
        name="modulation",
    )(cvec, w_mod, b_mod)


def _phase1_body(roww, x_ref, mod_ref, g1_ref, win_ref, mu_ref, wl_ref, bl_ref, wg2_ref, kk_ref,
                 ka_ref, rk_ref, cw_ref, cg_ref, bones_ref, trif_ref, trib_ref,
                 y0_ref, rqf_ref, rqb_ref, bonus_ref, gg_ref, oconv_ref, mlf_ref, mlb_ref,
                 nnf_ref, nnb_ref, wtf_ref, wtb_ref,
                 z_scr, opsf_scr, opsb_scr, v_scr, rowf_scr, rowb_scr):
    W = RWKV_WIDTH
    x = x_ref[0]
    mod = mod_ref[0]
    shift_a = mod[:, 0:D_MODEL]
    scale_a = mod[:, D_MODEL:2 * D_MODEL]
    ms = jnp.mean(x * x, axis=-1, keepdims=True)
    h = (x * lax.rsqrt(ms + NORM_EPS)) * g1_ref[...] * (1.0 + scale_a) + shift_a
    h_bf = h.astype(BF16)
    z_scr[:, 0:RWKV_COLS] = _dot(h_bf, win_ref[:, 0:RWKV_COLS])

    t_idx = lax.broadcasted_iota(jnp.int32, (TILE, 1), 0)
    pos = lax.rem(t_idx, roww)
    is_first = pos == 0
    is_last = pos == roww - 1

    def neighbours(zz):
        prev = jnp.where(is_first, 0.0, pltpu.roll(zz, 1, 0))
        nxt = jnp.where(is_last, 0.0, pltpu.roll(zz, TILE - 1, 0))
        return prev, nxt

    def shifted(lo, hi):
        zz = z_scr[:, lo:hi]
        prev, nxt = neighbours(zz)
        mu = mu_ref[:, lo:hi]
        return (1.0 - mu) * zz + (0.5 * mu) * (prev + nxt)

    r = shifted(0, W)
    k = shifted(W, 2 * W)
    v = shifted(2 * W, 3 * W)
    lora = shifted(3 * W, 3 * W + LORA_COLS)
    v_scr[...] = v

    lora_in = jnp.concatenate([jnp.tanh(lora[:, :LANES]), lora[:, LANES:]], axis=1)
    pre = _mm(lora_in, wl_ref[...]) + bl_ref[...]
    z_scr[:, RWKV_COLS:IN_COLS] = _dot(h_bf, win_ref[:, RWKV_COLS:IN_COLS])

    bones = bones_ref[...]
    kraw = k * kk_ref[...]
    kk = kraw * lax.rsqrt(_head_sum(kraw * kraw, bones) + 1e-12)

    log_decay, cum = [], []
    for d in range(2):
        lw = -DECAY_SCALE * _sigmoid(pre[:, d * W:(d + 1) * W])
        tri = trif_ref[...] if d == 0 else trib_ref[...]
        l_hi, l_lo = _split2(lw)
        log_decay.append(lw)
        cum.append(_dot(tri, l_hi) + _dot(tri, l_lo))

    def prepare(d):
        ops_scr, row_scr = (opsf_scr, rowf_scr) if d == 0 else (opsb_scr, rowb_scr)
        lw = log_decay[d]
        a = _sigmoid(pre[:, (2 + d) * W:(3 + d) * W])
        kd = k * (1.0 + (a - 1.0) * ka_ref[...])
        b = kk * a
        c3 = cum[d].reshape(CHUNKS_PER_TILE, CHUNK, W)
        mid = c3[:, CHUNK // 2:CHUNK // 2 + 1, :]
        end_row = CHUNK - 1 if d == 0 else 0
        end = c3[:, end_row:end_row + 1, :]
        cm = (c3 - mid).reshape(TILE, W)
        e_r = jnp.exp(cm)
        e_kap = jnp.exp(cm - lw)
        e_inv = jnp.exp(-cm)
        ops_scr[0] = kk * e_kap
        ops_scr[1] = r * e_r
        ops_scr[2] = kd * e_inv
        ops_scr[3] = b * e_inv
        for j, rv in enumerate((jnp.exp(end - mid), jnp.exp(end), jnp.exp(mid))):
            row_scr[j] = jnp.broadcast_to(rv, (CHUNKS_PER_TILE, SUBLANES, W)).reshape(
                CHUNKS_PER_TILE * SUBLANES, W)
        return kd

    row = lax.broadcasted_iota(jnp.int32, (CHUNK, LANES), 0)
    col = lax.bitwise_and(lax.broadcasted_iota(jnp.int32, (CHUNK, LANES), 1), HEAD_SIZE - 1)
    eye = row == col
    masks = (jnp.concatenate([row > col, row >= col], axis=0),
             jnp.concatenate([row < col, row <= col], axis=0))
    ml_refs = (mlf_ref, mlb_ref)
    nn_refs = (nnf_ref, nnb_ref)
    rq_refs = (rqf_ref, rqb_ref)

    def chains(d):
        ops_scr, row_scr = (opsf_scr, rowf_scr) if d == 0 else (opsb_scr, rowb_scr)
        rows = [pl.ds(ci * CHUNK, CHUNK) for ci in range(CHUNKS_PER_TILE)]
        crow = [pl.ds(ci * SUBLANES, 1) for ci in range(CHUNKS_PER_TILE)]
        lsl = [slice(p * LANES, (p + 1) * LANES) for p in range(PAIRS)]
        ch = [(c, p) for c in range(CHUNKS_PER_TILE) for p in range(PAIRS)]
        n = len(ch)

        v_pair = [v_scr[rows[c], lsl[p]] for (c, p) in ch]
        vbd = [_block_diag_rhs(vp.astype(BF16)) for vp in v_pair]
        kap = [ops_scr[0, rows[c], lsl[p]] for (c, p) in ch]
        rt = [ops_scr[1, rows[c], lsl[p]] for (c, p) in ch]
        kt = [ops_scr[2, rows[c], lsl[p]] for (c, p) in ch]
        bt = [ops_scr[3, rows[c], lsl[p]] for (c, p) in ch]
        to_end = [row_scr[0, crow[c], lsl[p]] for (c, p) in ch]
        e_mid = [row_scr[2, crow[c], lsl[p]] for (c, p) in ch]
        lhs = [jnp.concatenate([kap[i], rt[i]], axis=0).astype(BF16) for i in range(n)]
        a_all = [_mm_nt(lhs[i], jnp.concatenate([_block_diag_rhs(kt[i].astype(BF16)),
                                                 _block_diag_rhs(bt[i].astype(BF16))], axis=0))
                 for i in range(n)]
        a_k = [jnp.where(masks[d], a_all[i][:, 0:LANES], 0.0) for i in range(n)]
        a_b = [jnp.where(masks[d], a_all[i][:, LANES:2 * LANES], 0.0) for i in range(n)]
        a_kb = [x[0:CHUNK] for x in a_b]
        a_rb = [x[CHUNK:2 * CHUNK] for x in a_b]
        av = [_mm(a_k[i], vbd[i]) for i in range(n)]
        t_inv = [jnp.where(eye, 1.0, 0.0) - x for x in a_kb]
        pw = [_mm(x, _block_diag_rhs(x.astype(BF16))) for x in a_kb]
        n_fac = 2
        while 2 * n_fac < CHUNK:
            both = [_mm(jnp.concatenate([t, w], axis=0), _block_diag_rhs(w.astype(BF16)))
                    for t, w in zip(t_inv, pw)]
            t_inv = [t + x[0:CHUNK] for t, x in zip(t_inv, both)]
            pw = [x[CHUNK:2 * CHUNK] for x in both]
            n_fac *= 2
        t_inv = [t + _mm(t, _block_diag_rhs(w.astype(BF16))) for t, w in zip(t_inv, pw)]
        rhs = [jnp.concatenate([_block_diag_rhs((kap[i] * e_mid[i]).astype(BF16)),
                                _block_diag_rhs(av[i][0:CHUNK].astype(BF16))], axis=1) for i in range(n)]
        pq = [_mm(t_inv[i], rhs[i]) for i in range(n)]
        rhs2 = [jnp.concatenate([_block_diag_rhs(x[:, 0:LANES].astype(BF16)),
                                 _block_diag_rhs(x[:, LANES:2 * LANES].astype(BF16))], axis=1) for x in pq]
        rb = [_mm(a_rb[i], rhs2[i]) for i in range(n)]
        for i, (c, p) in enumerate(ch):
            rq_refs[d][0, rows[c], lsl[p]] = (rt[i] * e_mid[i] - rb[i][:, 0:LANES]).astype(BF16)
        y0d = [av[i][CHUNK:2 * CHUNK] - rb[i][:, LANES:2 * LANES] for i in range(n)]
        b_end = [(bt[i] * to_end[i]).astype(BF16) for i in range(n)]
        k_end = [(kt[i] * to_end[i]).astype(BF16) for i in range(n)]
        pq_b = [_mm(pq[i].T, b_end[i]) for i in range(n)]
        vk = [_mm(v_pair[i].T, k_end[i]) for i in range(n)]
        for i, (c, p) in enumerate(ch):
            ml_refs[d][0, c, p] = _compact(pq_b[i][0:LANES]).astype(BF16)
            nn_refs[d][0, c, p] = _compact(vk[i]) - _compact(pq_b[i][LANES:2 * LANES])
        return y0d

    kd_f = prepare(0)
    y0_f = chains(0)
    kd_b = prepare(1)

    gl = shifted(3 * W + LORA_COLS, RWKV_COLS)
    gg_ref[0] = _mm(_sigmoid(gl), wg2_ref[...])
    bonus_ref[0] = _head_sum(r * (kd_f + kd_b) * rk_ref[...], bones) * v
    gate_b = z_scr[:, RWKV_COLS:RWKV_COLS + CONV_WIDTH]
    u = z_scr[:, RWKV_COLS + CONV_WIDTH:RWKV_COLS + 2 * CONV_WIDTH] * z_scr[:, RWKV_COLS + 2 * CONV_WIDTH:IN_COLS]
    up, un = neighbours(u)
    oc = gate_b * (cw_ref[0:1, :] * up + cw_ref[1:2, :] * u + cw_ref[2:3, :] * un)
    oc = oc * lax.rsqrt(jnp.mean(oc * oc, axis=-1, keepdims=True) + NORM_EPS) * cg_ref[...]
    oconv_ref[0] = oc.astype(BF16)

    y0_b = chains(1)
    for i in range(len(y0_f)):
        ci, p = divmod(i, PAIRS)
        y0_ref[0, ci * CHUNK:(ci + 1) * CHUNK, p * LANES:(p + 1) * LANES] = y0_f[i] + y0_b[i]
    wtf_ref[0] = rowf_scr[1]
    wtb_ref[0] = rowb_scr[1]


def _phase1(x, mod, roww, lw):
    bsz, seq, _ = x.shape
    nt = seq // TILE
    nc = seq // CHUNK
    mod_map = (lambda b, t: (b, 0, 0)) if mod.shape[0] == bsz else (lambda b, t: (0, 0, 0))
    tok = lambda n: pl.BlockSpec((1, TILE, n), lambda b, t: (b, t, 0))
    mn_spec = pl.BlockSpec((1, CHUNKS_PER_TILE, PAIRS, HEAD_SIZE, LANES), lambda b, t: (b, t, 0, 0, 0))
    W = RWKV_WIDTH
    consts = (lw["norm1_g"], lw["w_in"], lw["mu"], lw["w_lora"], lw["b_lora"], lw["w_g2"], lw["k_k"],
              lw["k_a"], lw["r_k"], lw["conv_w"], lw["conv_gain"], lw["bones"], lw["tri_f"], lw["tri_b"])
    tok_shape = jax.ShapeDtypeStruct((bsz, seq, W), F32)
    tok_bf16 = jax.ShapeDtypeStruct((bsz, seq, W), BF16)
    ml_shape = jax.ShapeDtypeStruct((bsz, nc, PAIRS, HEAD_SIZE, LANES), BF16)
    nn_shape = jax.ShapeDtypeStruct((bsz, nc, PAIRS, HEAD_SIZE, LANES), F32)
    wt_spec = pl.BlockSpec((1, CHUNKS_PER_TILE * SUBLANES, W), lambda b, t: (b, t, 0))
    wt_shape = jax.ShapeDtypeStruct((bsz, nc * SUBLANES, W), F32)
    return pl.pallas_call(
        functools.partial(_phase1_body, roww),
        grid=(bsz, nt),
        in_specs=[tok(D_MODEL), pl.BlockSpec((1, 1, 6 * D_MODEL), mod_map)]
                 + [_const_spec(a.shape) for a in consts],
        out_specs=[tok(W)] * 6 + [mn_spec] * 4 + [wt_spec, wt_spec],
        out_shape=[tok_shape, tok_bf16, tok_bf16, tok_shape, tok_shape, tok_bf16,
                   ml_shape, ml_shape, nn_shape, nn_shape, wt_shape, wt_shape],
        scratch_shapes=[pltpu.VMEM((TILE, IN_COLS), F32),
                        pltpu.VMEM((4, TILE, W), F32),
                        pltpu.VMEM((4, TILE, W), F32),
                        pltpu.VMEM((TILE, W), F32),
                        pltpu.VMEM((3, CHUNKS_PER_TILE * SUBLANES, W), F32),
                        pltpu.VMEM((3, CHUNKS_PER_TILE * SUBLANES, W), F32)],
        compiler_params=pltpu.CompilerParams(dimension_semantics=("parallel", "parallel"),
                                             vmem_limit_bytes=VMEM_LIMIT),
        name="phase1_chunk_summaries",
    )(x, mod, *consts)


def _phase2_body(nb, s0f_ref, s0b_ref, mlf_ref, mlb_ref, nnf_ref, nnb_ref, wtf_ref, wtb_ref,
                 sf_ref, sb_ref, finf_ref, finb_ref, st_scr):
    step = pl.program_id(1)

    @pl.when(step == 0)
    def _():
        st_scr[0] = s0f_ref[...]
        st_scr[1] = s0b_ref[...]

    chains = [(b, d, p) for b in range(nb) for d in range(2) for p in range(PAIRS)]
    ml_refs = (mlf_ref, mlb_ref)
    nn_refs = (nnf_ref, nnb_ref)
    wt_refs = (wtf_ref, wtb_ref)
    s_refs = (sf_ref, sb_ref)
    s = [st_scr[d, b, p] for (b, d, p) in chains]
    for j in range(SCAN_CHUNKS):
        cj = (j, SCAN_CHUNKS - 1 - j)
        for i, (b, d, p) in enumerate(chains):
            s_refs[d][b, cj[d], p] = s[i].astype(BF16)
        low = [_mm(s[i], _block_diag_rhs(ml_refs[d][b, cj[d], p])) for i, (b, d, p) in enumerate(chains)]
        for i, (b, d, p) in enumerate(chains):
            w_row = wt_refs[d][b, cj[d] * SUBLANES:cj[d] * SUBLANES + 1, p * LANES:(p + 1) * LANES]
            s[i] = s[i] * w_row - low[i] + nn_refs[d][b, cj[d], p]
    for i, (b, d, p) in enumerate(chains):
        st_scr[d, b, p] = s[i]

    @pl.when(step == pl.num_programs(1) - 1)
    def _():
        finf_ref[...] = st_scr[0]
        finb_ref[...] = st_scr[1]


def _phase2(s0f, s0b, mlf, mlb, nnf, nnb, wtf, wtb):
    bsz, nc = mlf.shape[0], mlf.shape[1]
    bb = min(bsz, SCAN_BATCH)
    assert bsz % bb == 0 and nc % SCAN_CHUNKS == 0
    ns = nc // SCAN_CHUNKS
    st_blk = (bb, PAIRS, HEAD_SIZE, LANES)
    mn_blk = (bb, SCAN_CHUNKS, PAIRS, HEAD_SIZE, LANES)
    s_blk = (bb, SCAN_CHUNKS, PAIRS, HEAD_SIZE, LANES)
    wt_blk = (bb, SCAN_CHUNKS * SUBLANES, RWKV_WIDTH)
    fwd = lambda g, i: (g, i, 0, 0, 0)
    bwd = lambda g, i: (g, ns - 1 - i, 0, 0, 0)
    full = pl.BlockSpec(st_blk, lambda g, i: (g, 0, 0, 0))
    s_all = jax.ShapeDtypeStruct((bsz, nc, PAIRS, HEAD_SIZE, LANES), BF16)
    s_fin = jax.ShapeDtypeStruct((bsz, PAIRS, HEAD_SIZE, LANES), F32)
    return pl.pallas_call(
        functools.partial(_phase2_body, bb),
        grid=(bsz // bb, ns),
        in_specs=[full, full, pl.BlockSpec(mn_blk, fwd), pl.BlockSpec(mn_blk, bwd),
                  pl.BlockSpec(mn_blk, fwd), pl.BlockSpec(mn_blk, bwd),
                  pl.BlockSpec(wt_blk, lambda g, i: (g, i, 0)),
                  pl.BlockSpec(wt_blk, lambda g, i: (g, ns - 1 - i, 0))],
        out_specs=[pl.BlockSpec(s_blk, fwd), pl.BlockSpec(s_blk, bwd), full, full],
        out_shape=[s_all, s_all, s_fin, s_fin],
        scratch_shapes=[pltpu.VMEM((2,) + st_blk, F32)],
        compiler_params=pltpu.CompilerParams(dimension_semantics=("parallel", "arbitrary"),
                                             vmem_limit_bytes=VMEM_LIMIT),
        name="phase2_state_scan",
    )(s0f, s0b, mlf, mlb, nnf, nnb, wtf, wtb)


def _phase3_body(x_ref, mod_ref, y0_ref, rqf_ref, rqb_ref, bonus_ref, gg_ref, oconv_ref, sf_ref, sb_ref,
                 lnw_ref, lnb_ref, bones_ref, wout_ref, g2_ref, wff1_ref, wff2_ref, gfin_ref,
                 o_ref, y_scr):
    W = RWKV_WIDTH
    mod = mod_ref[0]
    gate_a = mod[:, 2 * D_MODEL:3 * D_MODEL]
    shift_f = mod[:, 3 * D_MODEL:4 * D_MODEL]
    scale_f = mod[:, 4 * D_MODEL:5 * D_MODEL]
    gate_f = mod[:, 5 * D_MODEL:6 * D_MODEL]
    inv_n = 1.0 / HEAD_SIZE
    carried = {}

    def mix(blk):
        r0 = blk * P3_ROWS
        rsl = slice(r0, r0 + P3_ROWS)
        for ci in range(P3_ROWS // CHUNK):
            cg = blk * (P3_ROWS // CHUNK) + ci
            rows = slice(cg * CHUNK, (cg + 1) * CHUNK)
            for p in range(PAIRS):
                ls = slice(p * LANES, (p + 1) * LANES)
                y_scr[rows, ls] = (y0_ref[rows, ls]
                                   + _mm_nt(rqf_ref[rows, ls], _block_diag_rhs(sf_ref[cg, p]))
                                   + _mm_nt(rqb_ref[rows, ls], _block_diag_rhs(sb_ref[cg, p])))
        yield
        y = y_scr[rsl, :]
        yc = y - _head_sum(y, bones_ref[...]) * inv_n
        yield
        y_var = _head_sum(yc * yc, bones_ref[...]) * inv_n
        yn = yc * lax.rsqrt(y_var + GN_EPS) * lnw_ref[...] + lnb_ref[...]
        o_rwkv = (yn + bonus_ref[rsl, :]) * gg_ref[rsl, :]
        yield
        mixed = (_dot(o_rwkv.astype(BF16), wout_ref[0:W, :])
                 + _dot(oconv_ref[rsl, :], wout_ref[W:D_MODEL, :]))
        x1 = x_ref[rsl, :] + gate_a * mixed
        ms = jnp.mean(x1 * x1, axis=-1, keepdims=True)
        h2 = (x1 * lax.rsqrt(ms + NORM_EPS)) * g2_ref[...] * (1.0 + scale_f) + shift_f
        carried[blk] = (x1, h2.astype(BF16))

    def mlp(blk):
        rsl = slice(blk * P3_ROWS, (blk + 1) * P3_ROWS)
        x1, h2 = carried.pop(blk)
        acc = None
        for j in range(D_FF // FF_CHUNK):
            cols = slice(j * FF_CHUNK, (j + 1) * FF_CHUNK)
            f1 = jnp.maximum(_dot(h2, wff1_ref[:, cols]), 0.0)
            f2 = _dot((f1 * f1).astype(BF16), wff2_ref[cols, :])
            acc = f2 if acc is None else acc + f2
            yield
        x2 = x1 + gate_f * acc
        ms2 = jnp.mean(x2 * x2, axis=-1, keepdims=True)
        o_ref[rsl, :] = (x2 * lax.rsqrt(ms2 + NORM_EPS)) * gfin_ref[...]

    def interleave(*gens):
        live = list(gens)
        while live:
            for g in list(live):
                try:
                    next(g)
                except StopIteration:
                    live.remove(g)

    n_blk = P3_TILE // P3_ROWS
    interleave(mix(0))
    for blk in range(1, n_blk):
        interleave(mlp(blk - 1), mix(blk))
    interleave(mlp(n_blk - 1))


def _phase3(x, mod, p1, sf, sb, lw, final_g):
    bsz, seq, _ = x.shape
    W = RWKV_WIDTH
    n_tok = bsz * seq
    nt = n_tok // P3_TILE
    assert n_tok % P3_TILE == 0 and (seq % P3_TILE == 0 or mod.shape[0] == 1)
    tiles_per_batch = max(seq // P3_TILE, 1)
    mod_map = (lambda t: (t // tiles_per_batch, 0, 0)) if mod.shape[0] == bsz else (lambda t: (0, 0, 0))
    tok = lambda n: pl.BlockSpec((P3_TILE, n), lambda t: (t, 0))
    cpt = P3_TILE // CHUNK
    s_spec = pl.BlockSpec((cpt, PAIRS, HEAD_SIZE, LANES), lambda t: (t, 0, 0, 0))
    flat = lambda a: a.reshape((n_tok,) + a.shape[2:])
    flat_s = lambda a: a.reshape((a.shape[0] * a.shape[1],) + a.shape[2:])
    consts = (lw["ln_x_w"], lw["ln_x_b"], lw["bones"], lw["w_out"], lw["norm2_g"], lw["w_ff1"], lw["w_ff2"],
              final_g)
    out = pl.pallas_call(
        _phase3_body,
        grid=(nt,),
        in_specs=[tok(D_MODEL), pl.BlockSpec((1, 1, 6 * D_MODEL), mod_map)] + [tok(W)] * 6 + [s_spec, s_spec]
                 + [_const_spec(a.shape) for a in consts],
        out_specs=tok(D_MODEL),
        out_shape=jax.ShapeDtypeStruct((n_tok, D_MODEL), F32),
        scratch_shapes=[pltpu.VMEM((P3_TILE, W), F32)],
        compiler_params=pltpu.CompilerParams(dimension_semantics=("parallel",),
                                             vmem_limit_bytes=VMEM_LIMIT),
        name="phase3_mix_mlp",
    )(flat(x), mod, *[flat(a) for a in p1], flat_s(sf), flat_s(sb), *consts)
    return out.reshape(bsz, seq, D_MODEL)


def _block_diag(blocks):
    rows = sum(b.shape[0] for b in blocks)
    cols = sum(b.shape[1] for b in blocks)
    out = jnp.zeros((rows, cols), blocks[0].dtype)
    r = c = 0
    for b in blocks:
        out = lax.dynamic_update_slice(out, b, (r, c))
        r += b.shape[0]
        c += b.shape[1]
    return out


def _tri_constants():
    t = jnp.arange(TILE)
    same = (t[:, None] // CHUNK) == (t[None, :] // CHUNK)
    tri_f = (same & (t[None, :] <= t[:, None])).astype(BF16)
    tri_b = (same & (t[None, :] >= t[:, None])).astype(BF16)
    lane = jnp.arange(RWKV_WIDTH)
    bones = ((lane[:, None] // HEAD_SIZE) == (lane[None, :] // HEAD_SIZE)).astype(BF16)
    return tri_f, tri_b, bones


def _pair_compact(s):
    b = s.shape[0]
    s = s.reshape(b, PAIRS, 2, HEAD_SIZE, HEAD_SIZE)
    return jnp.swapaxes(s, 2, 3).reshape(b, PAIRS, HEAD_SIZE, LANES)


def _pair_expand(sc):
    b = sc.shape[0]
    s = sc.reshape(b, PAIRS, HEAD_SIZE, 2, HEAD_SIZE)
    return jnp.swapaxes(s, 2, 3).reshape(b, RWKV_HEADS, HEAD_SIZE, HEAD_SIZE)


def kernel(x_prompt, x_sample, c, state_rwkv_fwd, state_rwkv_bwd, c_ctx, w_mod, b_mod, norm1_g, w_in, mu_shift, w0_f, w_w2_f, w0_b, w_w2_b, a0_f, w_a2_f, a0_b, w_a2_b, w_g2, k_k, k_a, r_k, ln_x_w, ln_x_b, conv_w, conv_gain, w_out, norm2_g, w_ff1, w_ff2, final_g):
    depth = w_in.shape[0]
    b_ctx, seq_ctx, _ = x_prompt.shape
    b_lat, seq_lat, _ = x_sample.shape
    assert depth == 1
    assert seq_ctx == TILE and seq_lat % TILE == 0 and TILE % GRID_W == 0
    tri_f, tri_b, bones = _tri_constants()
    row = lambda a: a.reshape(1, -1)
    n_pad = (-(b_lat + 1)) % SUBLANES
    cvec = jnp.concatenate([c, c_ctx[None, :], jnp.zeros((n_pad, D_MODEL), F32)], axis=0)
    zero_state = jnp.zeros((b_ctx, PAIRS, HEAD_SIZE, LANES), F32)

    i = 0
    lw = dict(
        norm1_g=row(norm1_g[i]), w_in=w_in[i].astype(BF16), mu=row(mu_shift[i]),
        w_lora=_block_diag([w_w2_f[i], w_w2_b[i], w_a2_f[i], w_a2_b[i]]),
        b_lora=jnp.concatenate([w0_f[i], w0_b[i], a0_f[i], a0_b[i]]).reshape(1, -1),
        w_g2=w_g2[i], k_k=row(k_k[i]), k_a=row(k_a[i]), r_k=row(r_k[i]),
        conv_w=conv_w[i], conv_gain=row(conv_gain[i]), bones=bones, tri_f=tri_f, tri_b=tri_b,
        ln_x_w=row(ln_x_w[i]), ln_x_b=row(ln_x_b[i]), w_out=w_out[i].astype(BF16),
        norm2_g=row(norm2_g[i]), w_ff1=w_ff1[i].astype(BF16), w_ff2=w_ff2[i].astype(BF16))
    mod = _modulation(cvec, w_mod[i], row(b_mod[i]))
    mod_lat = mod[:b_lat].reshape(b_lat, 1, 6 * D_MODEL)
    mod_ctx = mod[b_lat:b_lat + 1].reshape(1, 1, 6 * D_MODEL)

    def run(x, mod_s, roww, s0f, s0b):
        *p1, mlf, mlb, nnf, nnb, wtf, wtb = _phase1(x, mod_s, roww, lw)
        sf, sb, fin_f, fin_b = _phase2(s0f, s0b, mlf, mlb, nnf, nnb, wtf, wtb)
        return _phase3(x, mod_s, p1, sf, sb, lw, row(final_g)), fin_f, fin_b

    y_ctx, s_f, s_b = run(x_prompt, mod_ctx, seq_ctx, zero_state, zero_state)
    y_lat, _, _ = run(x_sample, mod_lat, GRID_W, _pair_compact(state_rwkv_fwd[:, i]),
                      _pair_compact(state_rwkv_bwd[:, i]))
    return (y_ctx, y_lat, _pair_expand(s_f)[:, None], _pair_expand(s_b)[:, None])
```

```python
import functools
import math

import jax
import jax.numpy as jnp
from jax import lax
from jax.experimental import pallas as pl
from jax.experimental.pallas import tpu as pltpu

D_MODEL = 1024
GRID_W = 64
RWKV_WIDTH = D_MODEL // 2
HEAD_SIZE = 64
RWKV_HEADS = RWKV_WIDTH // HEAD_SIZE
CONV_WIDTH = D_MODEL - RWKV_WIDTH
LORA_COLS = 256
GATE_LORA = 128
RWKV_COLS = 3 * RWKV_WIDTH + LORA_COLS + GATE_LORA
IN_COLS = RWKV_COLS + 3 * CONV_WIDTH
D_FF = 4 * D_MODEL
NORM_EPS = 1e-6
GN_EPS = HEAD_SIZE * 1e-5

LANES = 128
SUBLANES = 8
PAIRS = RWKV_WIDTH // LANES
CHUNK = 64
TILE = 256
CHUNKS_PER_TILE = TILE // CHUNK
INV_BASE = 8
DECAY_SCALE = math.exp(-0.5)
P3_TILE = 512
P3_ROWS = 256
FF_CHUNK = 1024
SCAN_CHUNKS = 4
SCAN_BATCH = 8
VMEM_LIMIT = 56 * 1024 * 1024

assert 2 * HEAD_SIZE == LANES and CHUNK == HEAD_SIZE

F32 = jnp.float32
BF16 = jnp.bfloat16
HI = lax.Precision.HIGHEST


def _dot(a, b):
    return jnp.dot(a, b, preferred_element_type=F32)


def _dot_hi(a, b):
    return jnp.dot(a, b, precision=HI, preferred_element_type=F32)


def _mm(a, b):
    return jnp.dot(a.astype(BF16), b.astype(BF16), preferred_element_type=F32)


def _mm_nt(a, b):
    return lax.dot_general(a.astype(BF16), b.astype(BF16), (((1,), (1,)), ((), ())),
                           preferred_element_type=F32)


def _mm_tn(a, b):
    return lax.dot_general(a.astype(BF16), b.astype(BF16), (((0,), (0,)), ((), ())),
                           preferred_element_type=F32)


def _split2(x):
    hi = x.astype(BF16)
    lo = (x - hi.astype(F32)).astype(BF16)
    return hi, lo


def _head_sum(x, bones):
    return _dot(x.astype(BF16), bones)


def _sigmoid(x):
    return 1.0 / (1.0 + jnp.exp(-x))


def _block_diag_rhs(xc):
    first = lax.broadcasted_iota(jnp.int32, xc.shape, 1) < HEAD_SIZE
    zero = jnp.zeros_like(xc)
    return jnp.concatenate([jnp.where(first, xc, zero), jnp.where(first, zero, xc)], axis=0)


def _compact(xbd):
    first = lax.broadcasted_iota(jnp.int32, (HEAD_SIZE, LANES), 1) < HEAD_SIZE
    return jnp.where(first, xbd[0:HEAD_SIZE], xbd[HEAD_SIZE:2 * HEAD_SIZE])


def _const_spec(shape):
    nd = len(shape)
    return pl.BlockSpec(shape, lambda *_: (0,) * nd, pipeline_mode=pl.Buffered(1))


def _mod_body(c_ref, w_ref, b_ref, o_ref):
    cv = c_ref[...]
    s = cv * _sigmoid(cv)
    o_ref[...] = _dot_hi(s, w_ref[...]) + b_ref[...]


def _modulation(cvec, w_mod, b_mod):
    rows = cvec.shape[0]
    n = w_mod.shape[1]
    bn = 1536
    return pl.pallas_call(
        _mod_body,
        grid=(n // bn,),
        in_specs=[pl.BlockSpec((rows, D_MODEL), lambda j: (0, 0)),
                  pl.BlockSpec((D_MODEL, bn), lambda j: (0, j)),
                  pl.BlockSpec((1, bn), lambda j: (0, j))],
        out_specs=pl.BlockSpec((rows, bn), lambda j: (0, j)),
        out_shape=jax.ShapeDtypeStruct((rows, n), F32),
        compiler_params=pltpu.CompilerParams(dimension_semantics=("arbitrary",),
                                             vmem_limit_bytes=VMEM_LIMIT),
        name="modulation",
    )(cvec, w_mod, b_mod)


def _phase1_body(roww, x_ref, mod_ref, g1_ref, win_ref, mu_ref, wl_ref, bl_ref, wg2_ref, kk_ref,
                 ka_ref, rk_ref, cw_ref, cg_ref, bones_ref, trif_ref, trib_ref,
                 y0_ref, rqf_ref, rqb_ref, bonus_ref, gg_ref, oconv_ref, mlf_ref, mlb_ref,
                 nnf_ref, nnb_ref, wtf_ref, wtb_ref,
                 z_scr, opsf_scr, opsb_scr, v_scr, rowf_scr, rowb_scr):
    W = RWKV_WIDTH
    x = x_ref[0]
    mod = mod_ref[0]
    shift_a = mod[:, 0:D_MODEL]
    scale_a = mod[:, D_MODEL:2 * D_MODEL]
    ms = jnp.mean(x * x, axis=-1, keepdims=True)
    h = (x * lax.rsqrt(ms + NORM_EPS)) * g1_ref[...] * (1.0 + scale_a) + shift_a
    h_bf = h.astype(BF16)
    z_scr[:, 3 * W:RWKV_COLS] = _dot(h_bf, win_ref[:, 3 * W:RWKV_COLS])
    z_scr[:, 0:3 * W] = _dot(h_bf, win_ref[:, 0:3 * W])

    t_idx = lax.broadcasted_iota(jnp.int32, (TILE, 1), 0)
    pos = lax.rem(t_idx, roww)
    is_first = pos == 0
    is_last = pos == roww - 1

    def neighbours(zz):
        prev = jnp.where(is_first, 0.0, pltpu.roll(zz, 1, 0))
        nxt = jnp.where(is_last, 0.0, pltpu.roll(zz, TILE - 1, 0))
        return prev, nxt

    def shifted(lo, hi):
        zz = z_scr[:, lo:hi]
        prev, nxt = neighbours(zz)
        mu = mu_ref[:, lo:hi]
        return (1.0 - mu) * zz + (0.5 * mu) * (prev + nxt)

    lora = shifted(3 * W, 3 * W + LORA_COLS)
    lora_in = jnp.concatenate([jnp.tanh(lora[:, :LANES]), lora[:, LANES:]], axis=1)
    pre = _mm(lora_in, wl_ref[...]) + bl_ref[...]
    z_scr[:, RWKV_COLS:IN_COLS] = _dot(h_bf, win_ref[:, RWKV_COLS:IN_COLS])

    r = shifted(0, W)
    k = shifted(W, 2 * W)
    v = shifted(2 * W, 3 * W)
    v_scr[...] = v

    bones = bones_ref[...]
    kraw = k * kk_ref[...]
    kk = kraw * lax.rsqrt(_head_sum(kraw * kraw, bones) + 1e-12)

    log_decay, cum = [], []
    for d in range(2):
        lw = -DECAY_SCALE * _sigmoid(pre[:, d * W:(d + 1) * W])
        tri = trif_ref[...] if d == 0 else trib_ref[...]
        l_hi, l_lo = _split2(lw)
        log_decay.append(lw)
        cum.append(_dot(tri, l_hi) + _dot(tri, l_lo))

    def prepare(d):
        ops_scr, row_scr = (opsf_scr, rowf_scr) if d == 0 else (opsb_scr, rowb_scr)
        lw = log_decay[d]
        a = _sigmoid(pre[:, (2 + d) * W:(3 + d) * W])
        kd = k * (1.0 + (a - 1.0) * ka_ref[...])
        b = kk * a
        c3 = cum[d].reshape(CHUNKS_PER_TILE, CHUNK, W)
        mid = c3[:, CHUNK // 2:CHUNK // 2 + 1, :]
        end_row = CHUNK - 1 if d == 0 else 0
        end = c3[:, end_row:end_row + 1, :]
        cm = (c3 - mid).reshape(TILE, W)
        e_r = jnp.exp(cm)
        e_kap = jnp.exp(cm - lw)
        e_inv = jnp.exp(-cm)
        ops_scr[0] = kk * e_kap
        ops_scr[1] = r * e_r
        ops_scr[2] = kd * e_inv
        ops_scr[3] = b * e_inv
        for j, rv in enumerate((jnp.exp(end - mid), jnp.exp(end), jnp.exp(mid))):
            row_scr[j] = jnp.broadcast_to(rv, (CHUNKS_PER_TILE, SUBLANES, W)).reshape(
                CHUNKS_PER_TILE * SUBLANES, W)
        return kd

    row = lax.broadcasted_iota(jnp.int32, (CHUNK, LANES), 0)
    col = lax.bitwise_and(lax.broadcasted_iota(jnp.int32, (CHUNK, LANES), 1), HEAD_SIZE - 1)
    eye = row == col
    same_block = {}
    size = INV_BASE
    while size <= CHUNK:
        same_block[size] = (row // size) == (col // size)
        size *= 2
    masks = (jnp.concatenate([row > col, row >= col], axis=0),
             jnp.concatenate([row < col, row <= col], axis=0))
    ml_refs = (mlf_ref, mlb_ref)
    nn_refs = (nnf_ref, nnb_ref)
    rq_refs = (rqf_ref, rqb_ref)

    def chains(d):
        ops_scr, row_scr = (opsf_scr, rowf_scr) if d == 0 else (opsb_scr, rowb_scr)
        rows = [pl.ds(ci * CHUNK, CHUNK) for ci in range(CHUNKS_PER_TILE)]
        crow = [pl.ds(ci * SUBLANES, 1) for ci in range(CHUNKS_PER_TILE)]
        lsl = [slice(p * LANES, (p + 1) * LANES) for p in range(PAIRS)]
        ch = [(c, p) for c in range(CHUNKS_PER_TILE) for p in range(PAIRS)]
        n = len(ch)

        v_pair = [v_scr[rows[c], lsl[p]] for (c, p) in ch]
        vbd = [_block_diag_rhs(vp.astype(BF16)) for vp in v_pair]
        kap = [ops_scr[0, rows[c], lsl[p]] for (c, p) in ch]
        rt = [ops_scr[1, rows[c], lsl[p]] for (c, p) in ch]
        kt = [ops_scr[2, rows[c], lsl[p]] for (c, p) in ch]
        bt = [ops_scr[3, rows[c], lsl[p]] for (c, p) in ch]
        to_end = [row_scr[0, crow[c], lsl[p]] for (c, p) in ch]
        e_mid = [row_scr[2, crow[c], lsl[p]] for (c, p) in ch]
        lhs = [jnp.concatenate([kap[i], rt[i]], axis=0).astype(BF16) for i in range(n)]
        a_all = [_mm_nt(lhs[i], jnp.concatenate([_block_diag_rhs(kt[i].astype(BF16)),
                                                 _block_diag_rhs(bt[i].astype(BF16))], axis=0))
                 for i in range(n)]
        a_k = [jnp.where(masks[d], a_all[i][:, 0:LANES], 0.0) for i in range(n)]
        a_b = [jnp.where(masks[d], a_all[i][:, LANES:2 * LANES], 0.0) for i in range(n)]
        a_kb = [x[0:CHUNK] for x in a_b]
        a_rb = [x[CHUNK:2 * CHUNK] for x in a_b]
        av = [_mm(a_k[i], vbd[i]) for i in range(n)]
        l_diag = [jnp.where(same_block[INV_BASE], x, 0.0) for x in a_kb]
        t_inv = [jnp.where(eye, 1.0, 0.0) - x for x in l_diag]
        pw = [_mm(x, _block_diag_rhs(x.astype(BF16))) for x in l_diag]
        n_fac = 2
        while 2 * n_fac < INV_BASE:
            both = [_mm(jnp.concatenate([t, w], axis=0), _block_diag_rhs(w.astype(BF16)))
                    for t, w in zip(t_inv, pw)]
            t_inv = [t + x[0:CHUNK] for t, x in zip(t_inv, both)]
            pw = [x[CHUNK:2 * CHUNK] for x in both]
            n_fac *= 2
        t_inv = [t + _mm(t, _block_diag_rhs(w.astype(BF16))) for t, w in zip(t_inv, pw)]
        size = INV_BASE
        while size < CHUNK:
            off = [jnp.where(same_block[2 * size] & ~same_block[size], x, 0.0) for x in a_kb]
            cross = [_mm(o, _block_diag_rhs(t.astype(BF16))) for o, t in zip(off, t_inv)]
            t_inv = [t - _mm(t, _block_diag_rhs(x.astype(BF16))) for t, x in zip(t_inv, cross)]
            size *= 2
        rhs = [jnp.concatenate([_block_diag_rhs((kap[i] * e_mid[i]).astype(BF16)),
                                _block_diag_rhs(av[i][0:CHUNK].astype(BF16))], axis=1) for i in range(n)]
        pq = [_mm(t_inv[i], rhs[i]) for i in range(n)]
        rhs2 = [jnp.concatenate([_block_diag_rhs(x[:, 0:LANES].astype(BF16)),
                                 _block_diag_rhs(x[:, LANES:2 * LANES].astype(BF16))], axis=1) for x in pq]
        rb = [_mm(a_rb[i], rhs2[i]) for i in range(n)]
        for i, (c, p) in enumerate(ch):
            rq_refs[d][0, rows[c], lsl[p]] = (rt[i] * e_mid[i] - rb[i][:, 0:LANES]).astype(BF16)
        y0d = [av[i][CHUNK:2 * CHUNK] - rb[i][:, LANES:2 * LANES] for i in range(n)]
        b_end = [(bt[i] * to_end[i]).astype(BF16) for i in range(n)]
        k_end = [(kt[i] * to_end[i]).astype(BF16) for i in range(n)]
        pq_b = [_mm(pq[i].T, b_end[i]) for i in range(n)]
        vk = [_mm(v_pair[i].T, k_end[i]) for i in range(n)]
        for i, (c, p) in enumerate(ch):
            ml_refs[d][0, c, p] = _compact(pq_b[i][0:LANES]).astype(BF16)
            nn_refs[d][0, c, p] = _compact(vk[i]) - _compact(pq_b[i][LANES:2 * LANES])
        return y0d

    kd_f = prepare(0)
    y0_f = chains(0)
    kd_b = prepare(1)

    gl = shifted(3 * W + LORA_COLS, RWKV_COLS)
    gg_ref[0] = _mm(_sigmoid(gl), wg2_ref[...])
    bonus_ref[0] = _head_sum(r * (kd_f + kd_b) * rk_ref[...], bones) * v
    gate_b = z_scr[:, RWKV_COLS:RWKV_COLS + CONV_WIDTH]
    u = z_scr[:, RWKV_COLS + CONV_WIDTH:RWKV_COLS + 2 * CONV_WIDTH] * z_scr[:, RWKV_COLS + 2 * CONV_WIDTH:IN_COLS]
    up, un = neighbours(u)
    oc = gate_b * (cw_ref[0:1, :] * up + cw_ref[1:2, :] * u + cw_ref[2:3, :] * un)
    oc = oc * lax.rsqrt(jnp.mean(oc * oc, axis=-1, keepdims=True) + NORM_EPS) * cg_ref[...]
    oconv_ref[0] = oc.astype(BF16)

    y0_b = chains(1)
    for i in range(len(y0_f)):
        ci, p = divmod(i, PAIRS)
        y0_ref[0, ci * CHUNK:(ci + 1) * CHUNK, p * LANES:(p + 1) * LANES] = y0_f[i] + y0_b[i]
    wtf_ref[0] = rowf_scr[1]
    wtb_ref[0] = rowb_scr[1]


def _phase1(x, mod, roww, lw):
    bsz, seq, _ = x.shape
    nt = seq // TILE
    nc = seq // CHUNK
    mod_map = (lambda b, t: (b, 0, 0)) if mod.shape[0] == bsz else (lambda b, t: (0, 0, 0))
    tok = lambda n: pl.BlockSpec((1, TILE, n), lambda b, t: (b, t, 0))
    mn_spec = pl.BlockSpec((1, CHUNKS_PER_TILE, PAIRS, HEAD_SIZE, LANES), lambda b, t: (b, t, 0, 0, 0))
    W = RWKV_WIDTH
    consts = (lw["norm1_g"], lw["w_in"], lw["mu"], lw["w_lora"], lw["b_lora"], lw["w_g2"], lw["k_k"],
              lw["k_a"], lw["r_k"], lw["conv_w"], lw["conv_gain"], lw["bones"], lw["tri_f"], lw["tri_b"])
    tok_shape = jax.ShapeDtypeStruct((bsz, seq, W), F32)
    tok_bf16 = jax.ShapeDtypeStruct((bsz, seq, W), BF16)
    ml_shape = jax.ShapeDtypeStruct((bsz, nc, PAIRS, HEAD_SIZE, LANES), BF16)
    nn_shape = jax.ShapeDtypeStruct((bsz, nc, PAIRS, HEAD_SIZE, LANES), F32)
    wt_spec = pl.BlockSpec((1, CHUNKS_PER_TILE * SUBLANES, W), lambda b, t: (b, t, 0))
    wt_shape = jax.ShapeDtypeStruct((bsz, nc * SUBLANES, W), F32)
    return pl.pallas_call(
        functools.partial(_phase1_body, roww),
        grid=(bsz, nt),
        in_specs=[tok(D_MODEL), pl.BlockSpec((1, 1, 6 * D_MODEL), mod_map)]
                 + [_const_spec(a.shape) for a in consts],
        out_specs=[tok(W)] * 6 + [mn_spec] * 4 + [wt_spec, wt_spec],
        out_shape=[tok_shape, tok_bf16, tok_bf16, tok_shape, tok_shape, tok_bf16,
                   ml_shape, ml_shape, nn_shape, nn_shape, wt_shape, wt_shape],
        scratch_shapes=[pltpu.VMEM((TILE, IN_COLS), F32),
                        pltpu.VMEM((4, TILE, W), F32),
                        pltpu.VMEM((4, TILE, W), F32),
                        pltpu.VMEM((TILE, W), F32),
                        pltpu.VMEM((3, CHUNKS_PER_TILE * SUBLANES, W), F32),
                        pltpu.VMEM((3, CHUNKS_PER_TILE * SUBLANES, W), F32)],
        compiler_params=pltpu.CompilerParams(dimension_semantics=("parallel", "parallel"),
                                             vmem_limit_bytes=VMEM_LIMIT),
        name="phase1_chunk_summaries",
    )(x, mod, *consts)


def _phase2_body(nb, s0f_ref, s0b_ref, mlf_ref, mlb_ref, nnf_ref, nnb_ref, wtf_ref, wtb_ref,
                 sf_ref, sb_ref, finf_ref, finb_ref, st_scr):
    step = pl.program_id(1)

    @pl.when(step == 0)
    def _():
        st_scr[0] = s0f_ref[...]
        st_scr[1] = s0b_ref[...]

    chains = [(b, d, p) for b in range(nb) for d in range(2) for p in range(PAIRS)]
    ml_refs = (mlf_ref, mlb_ref)
    nn_refs = (nnf_ref, nnb_ref)
    wt_refs = (wtf_ref, wtb_ref)
    s_refs = (sf_ref, sb_ref)
    s = [st_scr[d, b, p] for (b, d, p) in chains]
    for j in range(SCAN_CHUNKS):
        cj = (j, SCAN_CHUNKS - 1 - j)
        for i, (b, d, p) in enumerate(chains):
            s_refs[d][b, cj[d], p] = s[i].astype(BF16)
        low = [_mm(s[i], _block_diag_rhs(ml_refs[d][b, cj[d], p])) for i, (b, d, p) in enumerate(chains)]
        for i, (b, d, p) in enumerate(chains):
            w_row = wt_refs[d][b, cj[d] * SUBLANES:cj[d] * SUBLANES + 1, p * LANES:(p + 1) * LANES]
            s[i] = s[i] * w_row - low[i] + nn_refs[d][b, cj[d], p]
    for i, (b, d, p) in enumerate(chains):
        st_scr[d, b, p] = s[i]

    @pl.when(step == pl.num_programs(1) - 1)
    def _():
        finf_ref[...] = st_scr[0]
        finb_ref[...] = st_scr[1]


def _phase2(s0f, s0b, mlf, mlb, nnf, nnb, wtf, wtb):
    bsz, nc = mlf.shape[0], mlf.shape[1]
    bb = min(bsz, SCAN_BATCH)
    assert bsz % bb == 0 and nc % SCAN_CHUNKS == 0
    ns = nc // SCAN_CHUNKS
    st_blk = (bb, PAIRS, HEAD_SIZE, LANES)
    mn_blk = (bb, SCAN_CHUNKS, PAIRS, HEAD_SIZE, LANES)
    s_blk = (bb, SCAN_CHUNKS, PAIRS, HEAD_SIZE, LANES)
    wt_blk = (bb, SCAN_CHUNKS * SUBLANES, RWKV_WIDTH)
    fwd = lambda g, i: (g, i, 0, 0, 0)
    bwd = lambda g, i: (g, ns - 1 - i, 0, 0, 0)
    full = pl.BlockSpec(st_blk, lambda g, i: (g, 0, 0, 0))
    s_all = jax.ShapeDtypeStruct((bsz, nc, PAIRS, HEAD_SIZE, LANES), BF16)
    s_fin = jax.ShapeDtypeStruct((bsz, PAIRS, HEAD_SIZE, LANES), F32)
    return pl.pallas_call(
        functools.partial(_phase2_body, bb),
        grid=(bsz // bb, ns),
        in_specs=[full, full, pl.BlockSpec(mn_blk, fwd), pl.BlockSpec(mn_blk, bwd),
                  pl.BlockSpec(mn_blk, fwd), pl.BlockSpec(mn_blk, bwd),
                  pl.BlockSpec(wt_blk, lambda g, i: (g, i, 0)),
                  pl.BlockSpec(wt_blk, lambda g, i: (g, ns - 1 - i, 0))],
        out_specs=[pl.BlockSpec(s_blk, fwd), pl.BlockSpec(s_blk, bwd), full, full],
        out_shape=[s_all, s_all, s_fin, s_fin],
        scratch_shapes=[pltpu.VMEM((2,) + st_blk, F32)],
        compiler_params=pltpu.CompilerParams(dimension_semantics=("parallel", "arbitrary"),
                                             vmem_limit_bytes=VMEM_LIMIT),
        name="phase2_state_scan",
    )(s0f, s0b, mlf, mlb, nnf, nnb, wtf, wtb)


def _phase3_body(x_ref, mod_ref, y0_ref, rqf_ref, rqb_ref, bonus_ref, gg_ref, oconv_ref, sf_ref, sb_ref,
                 lnw_ref, lnb_ref, bones_ref, wout_ref, g2_ref, wff1_ref, wff2_ref, gfin_ref,
                 o_ref, y_scr):
    W = RWKV_WIDTH
    mod = mod_ref[0]
    gate_a = mod[:, 2 * D_MODEL:3 * D_MODEL]
    shift_f = mod[:, 3 * D_MODEL:4 * D_MODEL]
    scale_f = mod[:, 4 * D_MODEL:5 * D_MODEL]
    gate_f = mod[:, 5 * D_MODEL:6 * D_MODEL]
    inv_n = 1.0 / HEAD_SIZE
    carried = {}

    def mix(blk):
        r0 = blk * P3_ROWS
        rsl = slice(r0, r0 + P3_ROWS)
        for ci in range(P3_ROWS // CHUNK):
            cg = blk * (P3_ROWS // CHUNK) + ci
            rows = slice(cg * CHUNK, (cg + 1) * CHUNK)
            for p in range(PAIRS):
                ls = slice(p * LANES, (p + 1) * LANES)
                y_scr[rows, ls] = (y0_ref[rows, ls]
                                   + _mm_nt(rqf_ref[rows, ls], _block_diag_rhs(sf_ref[cg, p]))
                                   + _mm_nt(rqb_ref[rows, ls], _block_diag_rhs(sb_ref[cg, p])))
        yield
        y = y_scr[rsl, :]
        yc = y - _head_sum(y, bones_ref[...]) * inv_n
        yield
        y_var = _head_sum(yc * yc, bones_ref[...]) * inv_n
        yn = yc * lax.rsqrt(y_var + GN_EPS) * lnw_ref[...] + lnb_ref[...]
        o_rwkv = (yn + bonus_ref[rsl, :]) * gg_ref[rsl, :]
        yield
        mixed = (_dot(o_rwkv.astype(BF16), wout_ref[0:W, :])
                 + _dot(oconv_ref[rsl, :], wout_ref[W:D_MODEL, :]))
        x1 = x_ref[rsl, :] + gate_a * mixed
        ms = jnp.mean(x1 * x1, axis=-1, keepdims=True)
        h2 = (x1 * lax.rsqrt(ms + NORM_EPS)) * g2_ref[...] * (1.0 + scale_f) + shift_f
        carried[blk] = (x1, h2.astype(BF16))

    def mlp(blk):
        rsl = slice(blk * P3_ROWS, (blk + 1) * P3_ROWS)
        x1, h2 = carried.pop(blk)
        acc = None
        for j in range(D_FF // FF_CHUNK):
            cols = slice(j * FF_CHUNK, (j + 1) * FF_CHUNK)
            f1 = jnp.maximum(_dot(h2, wff1_ref[:, cols]), 0.0)
            f2 = _dot((f1 * f1).astype(BF16), wff2_ref[cols, :])
            acc = f2 if acc is None else acc + f2
            yield
        x2 = x1 + gate_f * acc
        ms2 = jnp.mean(x2 * x2, axis=-1, keepdims=True)
        o_ref[rsl, :] = (x2 * lax.rsqrt(ms2 + NORM_EPS)) * gfin_ref[...]

    def interleave(*gens):
        live = list(gens)
        while live:
            for g in list(live):
                try:
                    next(g)
                except StopIteration:
                    live.remove(g)

    n_blk = P3_TILE // P3_ROWS
    interleave(mix(0))
    for blk in range(1, n_blk):
        interleave(mlp(blk - 1), mix(blk))
    interleave(mlp(n_blk - 1))


def _phase3(x, mod, p1, sf, sb, lw, final_g):
    bsz, seq, _ = x.shape
    W = RWKV_WIDTH
    n_tok = bsz * seq
    nt = n_tok // P3_TILE
    assert n_tok % P3_TILE == 0 and (seq % P3_TILE == 0 or mod.shape[0] == 1)
    tiles_per_batch = max(seq // P3_TILE, 1)
    mod_map = (lambda t: (t // tiles_per_batch, 0, 0)) if mod.shape[0] == bsz else (lambda t: (0, 0, 0))
    tok = lambda n: pl.BlockSpec((P3_TILE, n), lambda t: (t, 0))
    cpt = P3_TILE // CHUNK
    s_spec = pl.BlockSpec((cpt, PAIRS, HEAD_SIZE, LANES), lambda t: (t, 0, 0, 0))
    flat = lambda a: a.reshape((n_tok,) + a.shape[2:])
    flat_s = lambda a: a.reshape((a.shape[0] * a.shape[1],) + a.shape[2:])
    consts = (lw["ln_x_w"], lw["ln_x_b"], lw["bones"], lw["w_out"], lw["norm2_g"], lw["w_ff1"], lw["w_ff2"],
              final_g)
    out = pl.pallas_call(
        _phase3_body,
        grid=(nt,),
        in_specs=[tok(D_MODEL), pl.BlockSpec((1, 1, 6 * D_MODEL), mod_map)] + [tok(W)] * 6 + [s_spec, s_spec]
                 + [_const_spec(a.shape) for a in consts],
        out_specs=tok(D_MODEL),
        out_shape=jax.ShapeDtypeStruct((n_tok, D_MODEL), F32),
        scratch_shapes=[pltpu.VMEM((P3_TILE, W), F32)],
        compiler_params=pltpu.CompilerParams(dimension_semantics=("parallel",),
                                             vmem_limit_bytes=VMEM_LIMIT),
        name="phase3_mix_mlp",
    )(flat(x), mod, *[flat(a) for a in p1], flat_s(sf), flat_s(sb), *consts)
    return out.reshape(bsz, seq, D_MODEL)


def _block_diag(blocks):
    rows = sum(b.shape[0] for b in blocks)
    cols = sum(b.shape[1] for b in blocks)
    out = jnp.zeros((rows, cols), blocks[0].dtype)
    r = c = 0
    for b in blocks:
        out = lax.dynamic_update_slice(out, b, (r, c))
        r += b.shape[0]
        c += b.shape[1]
    return out


def _tri_constants():
    t = jnp.arange(TILE)
    same = (t[:, None] // CHUNK) == (t[None, :] // CHUNK)
    tri_f = (same & (t[None, :] <= t[:, None])).astype(BF16)
    tri_b = (same & (t[None, :] >= t[:, None])).astype(BF16)
    lane = jnp.arange(RWKV_WIDTH)
    bones = ((lane[:, None] // HEAD_SIZE) == (lane[None, :] // HEAD_SIZE)).astype(BF16)
    return tri_f, tri_b, bones


def _pair_compact(s):
    b = s.shape[0]
    s = s.reshape(b, PAIRS, 2, HEAD_SIZE, HEAD_SIZE)
    return jnp.swapaxes(s, 2, 3).reshape(b, PAIRS, HEAD_SIZE, LANES)


def _pair_expand(sc):
    b = sc.shape[0]
    s = sc.reshape(b, PAIRS, HEAD_SIZE, 2, HEAD_SIZE)
    return jnp.swapaxes(s, 2, 3).reshape(b, RWKV_HEADS, HEAD_SIZE, HEAD_SIZE)


def kernel(x_prompt, x_sample, c, state_rwkv_fwd, state_rwkv_bwd, c_ctx, w_mod, b_mod, norm1_g, w_in, mu_shift, w0_f, w_w2_f, w0_b, w_w2_b, a0_f, w_a2_f, a0_b, w_a2_b, w_g2, k_k, k_a, r_k, ln_x_w, ln_x_b, conv_w, conv_gain, w_out, norm2_g, w_ff1, w_ff2, final_g):
    depth = w_in.shape[0]
    b_ctx, seq_ctx, _ = x_prompt.shape
    b_lat, seq_lat, _ = x_sample.shape
    assert depth == 1
    assert seq_ctx == TILE and seq_lat % TILE == 0 and TILE % GRID_W == 0
    tri_f, tri_b, bones = _tri_constants()
    row = lambda a: a.reshape(1, -1)
    n_pad = (-(b_lat + 1)) % SUBLANES
    cvec = jnp.concatenate([c, c_ctx[None, :], jnp.zeros((n_pad, D_MODEL), F32)], axis=0)
    zero_state = jnp.zeros((b_ctx, PAIRS, HEAD_SIZE, LANES), F32)

    i = 0
    lw = dict(
        norm1_g=row(norm1_g[i]), w_in=w_in[i].astype(BF16), mu=row(mu_shift[i]),
        w_lora=_block_diag([w_w2_f[i], w_w2_b[i], w_a2_f[i], w_a2_b[i]]),
        b_lora=jnp.concatenate([w0_f[i], w0_b[i], a0_f[i], a0_b[i]]).reshape(1, -1),
        w_g2=w_g2[i], k_k=row(k_k[i]), k_a=row(k_a[i]), r_k=row(r_k[i]),
        conv_w=conv_w[i], conv_gain=row(conv_gain[i]), bones=bones, tri_f=tri_f, tri_b=tri_b,
        ln_x_w=row(ln_x_w[i]), ln_x_b=row(ln_x_b[i]), w_out=w_out[i].astype(BF16),
        norm2_g=row(norm2_g[i]), w_ff1=w_ff1[i].astype(BF16), w_ff2=w_ff2[i].astype(BF16))
    mod = _modulation(cvec, w_mod[i], row(b_mod[i]))
    mod_lat = mod[:b_lat].reshape(b_lat, 1, 6 * D_MODEL)
    mod_ctx = mod[b_lat:b_lat + 1].reshape(1, 1, 6 * D_MODEL)

    def run(x, mod_s, roww, s0f, s0b):
        *p1, mlf, mlb, nnf, nnb, wtf, wtb = _phase1(x, mod_s, roww, lw)
        sf, sb, fin_f, fin_b = _phase2(s0f, s0b, mlf, mlb, nnf, nnb, wtf, wtb)
        return _phase3(x, mod_s, p1, sf, sb, lw, row(final_g)), fin_f, fin_b

    y_ctx, s_f, s_b = run(x_prompt, mod_ctx, seq_ctx, zero_state, zero_state)
    y_lat, _, _ = run(x_sample, mod_lat, GRID_W, _pair_compact(state_rwkv_fwd[:, i]),
                      _pair_compact(state_rwkv_bwd[:, i]))
    return (y_ctx, y_lat, _pair_expand(s_f)[:, None], _pair_expand(s_b)[:, None])
```

```python
import functools
import math

import jax
import jax.numpy as jnp
from jax import lax
from jax.experimental import pallas as pl
from jax.experimental.pallas import tpu as pltpu

D_MODEL = 1024
GRID_W = 64
RWKV_WIDTH = D_MODEL // 2
HEAD_SIZE = 64
RWKV_HEADS = RWKV_WIDTH // HEAD_SIZE
CONV_WIDTH = D_MODEL - RWKV_WIDTH
LORA_COLS = 256
GATE_LORA = 128
RWKV_COLS = 3 * RWKV_WIDTH + LORA_COLS + GATE_LORA
IN_COLS = RWKV_COLS + 3 * CONV_WIDTH
D_FF = 4 * D_MODEL
NORM_EPS = 1e-6
GN_EPS = HEAD_SIZE * 1e-5

LANES = 128
SUBLANES = 8
PAIRS = RWKV_WIDTH // LANES
CHUNK = 64
TILE = 256
CHUNKS_PER_TILE = TILE // CHUNK
INV_BASE = 8
DECAY_SCALE = math.exp(-0.5)
P3_TILE = 512
P3_ROWS = 256
FF_CHUNK = 1024
SCAN_CHUNKS = 4
SCAN_BATCH = 8
VMEM_LIMIT = 56 * 1024 * 1024

assert 2 * HEAD_SIZE == LANES and CHUNK == HEAD_SIZE

F32 = jnp.float32
BF16 = jnp.bfloat16
HI = lax.Precision.HIGHEST


def _dot(a, b):
    return jnp.dot(a, b, preferred_element_type=F32)


def _dot_hi(a, b):
    return jnp.dot(a, b, precision=HI, preferred_element_type=F32)


def _mm(a, b):
    return jnp.dot(a.astype(BF16), b.astype(BF16), preferred_element_type=F32)


def _mm_nt(a, b):
    return lax.dot_general(a.astype(BF16), b.astype(BF16), (((1,), (1,)), ((), ())),
                           preferred_element_type=F32)


def _mm_tn(a, b):
    return lax.dot_general(a.astype(BF16), b.astype(BF16), (((0,), (0,)), ((), ())),
                           preferred_element_type=F32)


def _split2(x):
    hi = x.astype(BF16)
    lo = (x - hi.astype(F32)).astype(BF16)
    return hi, lo


def _head_sum(x, bones):
    return _dot(x.astype(BF16), bones)


def _sigmoid(x):
    return 1.0 / (1.0 + jnp.exp(-x))


def _block_diag_rhs(xc):
    first = lax.broadcasted_iota(jnp.int32, xc.shape, 1) < HEAD_SIZE
    zero = jnp.zeros_like(xc)
    return jnp.concatenate([jnp.where(first, xc, zero), jnp.where(first, zero, xc)], axis=0)


def _compact(xbd):
    first = lax.broadcasted_iota(jnp.int32, (HEAD_SIZE, LANES), 1) < HEAD_SIZE
    return jnp.where(first, xbd[0:HEAD_SIZE], xbd[HEAD_SIZE:2 * HEAD_SIZE])


def _const_spec(shape):
    nd = len(shape)
    return pl.BlockSpec(shape, lambda *_: (0,) * nd, pipeline_mode=pl.Buffered(1))


def _mod_body(c_ref, w_ref, b_ref, o_ref):
    cv = c_ref[...]
    s = cv * _sigmoid(cv)
    o_ref[...] = _dot_hi(s, w_ref[...]) + b_ref[...]


def _modulation(cvec, w_mod, b_mod):
    rows = cvec.shape[0]
    n = w_mod.shape[1]
    bn = 1536
    return pl.pallas_call(
        _mod_body,
        grid=(n // bn,),
        in_specs=[pl.BlockSpec((rows, D_MODEL), lambda j: (0, 0)),
                  pl.BlockSpec((D_MODEL, bn), lambda j: (0, j)),
                  pl.BlockSpec((1, bn), lambda j: (0, j))],
        out_specs=pl.BlockSpec((rows, bn), lambda j: (0, j)),
        out_shape=jax.ShapeDtypeStruct((rows, n), F32),
        compiler_params=pltpu.CompilerParams(dimension_semantics=("arbitrary",),
                                             vmem_limit_bytes=VMEM_LIMIT),
        name="modulation",
    )(cvec, w_mod, b_mod)


def _phase1_body(roww, x_ref, mod_ref, g1_ref, win_ref, mu_ref, wl_ref, bl_ref, wg2_ref, kk_ref,
                 ka_ref, rk_ref, cw_ref, cg_ref, bones_ref, trif_ref, trib_ref,
                 y0_ref, rqf_ref, rqb_ref, bonus_ref, gg_ref, oconv_ref, mlf_ref, mlb_ref,
                 nnf_ref, nnb_ref, wtf_ref, wtb_ref,
                 z_scr, opsf_scr, opsb_scr, v_scr, rowf_scr, rowb_scr):
    W = RWKV_WIDTH
    x = x_ref[0]
    mod = mod_ref[0]
    shift_a = mod[:, 0:D_MODEL]
    scale_a = mod[:, D_MODEL:2 * D_MODEL]
    ms = jnp.mean(x * x, axis=-1, keepdims=True)
    h = (x * lax.rsqrt(ms + NORM_EPS)) * g1_ref[...] * (1.0 + scale_a) + shift_a
    h_bf = h.astype(BF16)
    z_scr[:, 3 * W:RWKV_COLS] = _dot(h_bf, win_ref[:, 3 * W:RWKV_COLS])
    z_scr[:, 0:3 * W] = _dot(h_bf, win_ref[:, 0:3 * W])

    t_idx = lax.broadcasted_iota(jnp.int32, (TILE, 1), 0)
    pos = lax.rem(t_idx, roww)
    is_first = pos == 0
    is_last = pos == roww - 1

    def neighbours(zz):
        prev = jnp.where(is_first, 0.0, pltpu.roll(zz, 1, 0))
        nxt = jnp.where(is_last, 0.0, pltpu.roll(zz, TILE - 1, 0))
        return prev, nxt

    def shifted(lo, hi):
        zz = z_scr[:, lo:hi]
        prev, nxt = neighbours(zz)
        mu = mu_ref[:, lo:hi]
        return (1.0 - mu) * zz + (0.5 * mu) * (prev + nxt)

    lora = shifted(3 * W, 3 * W + LORA_COLS)
    lora_in = jnp.concatenate([jnp.tanh(lora[:, :LANES]), lora[:, LANES:]], axis=1)
    pre = _mm(lora_in, wl_ref[...]) + bl_ref[...]
    z_scr[:, RWKV_COLS:IN_COLS] = _dot(h_bf, win_ref[:, RWKV_COLS:IN_COLS])

    r = shifted(0, W)
    k = shifted(W, 2 * W)
    v = shifted(2 * W, 3 * W)
    v_scr[...] = v

    bones = bones_ref[...]
    kraw = k * kk_ref[...]
    kk = kraw * lax.rsqrt(_head_sum(kraw * kraw, bones) + 1e-12)

    log_decay, cum = [], []
    for d in range(2):
        lw = -DECAY_SCALE * _sigmoid(pre[:, d * W:(d + 1) * W])
        tri = trif_ref[...] if d == 0 else trib_ref[...]
        l_hi, l_lo = _split2(lw)
        log_decay.append(lw)
        cum.append(_dot(tri, l_hi) + _dot(tri, l_lo))

    def prepare(d):
        ops_scr, row_scr = (opsf_scr, rowf_scr) if d == 0 else (opsb_scr, rowb_scr)
        lw = log_decay[d]
        a = _sigmoid(pre[:, (2 + d) * W:(3 + d) * W])
        kd = k * (1.0 + (a - 1.0) * ka_ref[...])
        b = kk * a
        c3 = cum[d].reshape(CHUNKS_PER_TILE, CHUNK, W)
        mid = c3[:, CHUNK // 2:CHUNK // 2 + 1, :]
        end_row = CHUNK - 1 if d == 0 else 0
        end = c3[:, end_row:end_row + 1, :]
        cm = (c3 - mid).reshape(TILE, W)
        e_r = jnp.exp(cm)
        e_kap = jnp.exp(cm - lw)
        e_inv = jnp.exp(-cm)
        ops_scr[0] = kk * e_kap
        ops_scr[1] = r * e_r
        ops_scr[2] = kd * e_inv
        ops_scr[3] = b * e_inv
        for j, rv in enumerate((jnp.exp(end - mid), jnp.exp(end), jnp.exp(mid))):
            row_scr[j] = jnp.broadcast_to(rv, (CHUNKS_PER_TILE, SUBLANES, W)).reshape(
                CHUNKS_PER_TILE * SUBLANES, W)
        return kd

    row = lax.broadcasted_iota(jnp.int32, (CHUNK, LANES), 0)
    col = lax.bitwise_and(lax.broadcasted_iota(jnp.int32, (CHUNK, LANES), 1), HEAD_SIZE - 1)
    eye = row == col
    same_block = {}
    size = INV_BASE
    while size <= CHUNK:
        same_block[size] = (row // size) == (col // size)
        size *= 2
    masks = (jnp.concatenate([row > col, row >= col], axis=0),
             jnp.concatenate([row < col, row <= col], axis=0))
    ml_refs = (mlf_ref, mlb_ref)
    nn_refs = (nnf_ref, nnb_ref)
    rq_refs = (rqf_ref, rqb_ref)

    def chains(d):
        ops_scr, row_scr = (opsf_scr, rowf_scr) if d == 0 else (opsb_scr, rowb_scr)
        rows = [pl.ds(ci * CHUNK, CHUNK) for ci in range(CHUNKS_PER_TILE)]
        crow = [pl.ds(ci * SUBLANES, 1) for ci in range(CHUNKS_PER_TILE)]
        lsl = [slice(p * LANES, (p + 1) * LANES) for p in range(PAIRS)]
        ch = [(c, p) for c in range(CHUNKS_PER_TILE) for p in range(PAIRS)]
        n = len(ch)

        v_pair = [v_scr[rows[c], lsl[p]] for (c, p) in ch]
        vbd = [_block_diag_rhs(vp.astype(BF16)) for vp in v_pair]
        kap = [ops_scr[0, rows[c], lsl[p]] for (c, p) in ch]
        rt = [ops_scr[1, rows[c], lsl[p]] for (c, p) in ch]
        kt = [ops_scr[2, rows[c], lsl[p]] for (c, p) in ch]
        bt = [ops_scr[3, rows[c], lsl[p]] for (c, p) in ch]
        to_end = [row_scr[0, crow[c], lsl[p]] for (c, p) in ch]
        e_mid = [row_scr[2, crow[c], lsl[p]] for (c, p) in ch]
        lhs = [jnp.concatenate([kap[i], rt[i]], axis=0).astype(BF16) for i in range(n)]
        a_all = [_mm_nt(lhs[i], jnp.concatenate([_block_diag_rhs(kt[i].astype(BF16)),
                                                 _block_diag_rhs(bt[i].astype(BF16))], axis=0))
                 for i in range(n)]
        a_k = [jnp.where(masks[d], a_all[i][:, 0:LANES], 0.0) for i in range(n)]
        a_b = [jnp.where(masks[d], a_all[i][:, LANES:2 * LANES], 0.0) for i in range(n)]
        a_kb = [x[0:CHUNK] for x in a_b]
        a_rb = [x[CHUNK:2 * CHUNK] for x in a_b]
        av = [_mm(a_k[i], vbd[i]) for i in range(n)]
        l_diag = [jnp.where(same_block[INV_BASE], x, 0.0) for x in a_kb]
        t_inv = [jnp.where(eye, 1.0, 0.0) - x for x in l_diag]
        pw = [_mm(x, _block_diag_rhs(x.astype(BF16))) for x in l_diag]
        n_fac = 2
        while 2 * n_fac < INV_BASE:
            both = [_mm(jnp.concatenate([t, w], axis=0), _block_diag_rhs(w.astype(BF16)))
                    for t, w in zip(t_inv, pw)]
            t_inv = [t + x[0:CHUNK] for t, x in zip(t_inv, both)]
            pw = [x[CHUNK:2 * CHUNK] for x in both]
            n_fac *= 2
        t_inv = [t + _mm(t, _block_diag_rhs(w.astype(BF16))) for t, w in zip(t_inv, pw)]
        sizes = []
        size = INV_BASE
        while size < CHUNK:
            sizes.append(size)
            size *= 2
        half = CHUNK // 2

        def hit(s):
            return [j for j in range(CHUNK // s) if j % 2 == (1 if d == 0 else 0)]

        def gather(x, s):
            return jnp.concatenate([x[j * s:(j + 1) * s] for j in hit(s)], axis=0)

        def scatter(base, upd, s):
            rows_of = {j: i for i, j in enumerate(hit(s))}
            parts = [upd[rows_of[j] * s:(rows_of[j] + 1) * s] if j in rows_of
                     else (jnp.zeros((s, LANES), F32) if base is None else base[j * s:(j + 1) * s])
                     for j in range(CHUNK // s)]
            return jnp.concatenate(parts, axis=0)

        off = [[gather(jnp.where(same_block[2 * s] & ~same_block[s], x, 0.0), s) for s in sizes]
               for x in a_kb]
        y_all = [_mm(jnp.concatenate(o, axis=0), _block_diag_rhs(t.astype(BF16)))
                 for o, t in zip(off, t_inv)]
        y_lvl = [[y[j * half:(j + 1) * half] for j in range(len(sizes))] for y in y_all]
        for li, s in enumerate(sizes):
            lhs_m = [jnp.concatenate([gather(t, s)] + y[li + 1:], axis=0) for t, y in zip(t_inv, y_lvl)]
            prod = [_mm(l, _block_diag_rhs(scatter(None, y[li], s).astype(BF16)))
                    for l, y in zip(lhs_m, y_lvl)]
            t_inv = [scatter(t, gather(t, s) - x[0:half], s) for t, x in zip(t_inv, prod)]
            y_lvl = [y[:li + 1] + [y[j] - x[(j - li) * half:(j - li + 1) * half]
                                   for j in range(li + 1, len(sizes))]
                     for y, x in zip(y_lvl, prod)]
        rhs = [jnp.concatenate([_block_diag_rhs((kap[i] * e_mid[i]).astype(BF16)),
                                _block_diag_rhs(av[i][0:CHUNK].astype(BF16))], axis=1) for i in range(n)]
        pq = [_mm(t_inv[i], rhs[i]) for i in range(n)]
        rhs2 = [jnp.concatenate([_block_diag_rhs(x[:, 0:LANES].astype(BF16)),
                                 _block_diag_rhs(x[:, LANES:2 * LANES].astype(BF16))], axis=1) for x in pq]
        rb = [_mm(a_rb[i], rhs2[i]) for i in range(n)]
        for i, (c, p) in enumerate(ch):
            rq_refs[d][0, rows[c], lsl[p]] = (rt[i] * e_mid[i] - rb[i][:, 0:LANES]).astype(BF16)
        y0d = [av[i][CHUNK:2 * CHUNK] - rb[i][:, LANES:2 * LANES] for i in range(n)]
        b_end = [(bt[i] * to_end[i]).astype(BF16) for i in range(n)]
        k_end = [(kt[i] * to_end[i]).astype(BF16) for i in range(n)]
        pq_b = [_mm(pq[i].T, b_end[i]) for i in range(n)]
        vk = [_mm(v_pair[i].T, k_end[i]) for i in range(n)]
        for i, (c, p) in enumerate(ch):
            ml_refs[d][0, c, p] = _compact(pq_b[i][0:LANES]).astype(BF16)
            nn_refs[d][0, c, p] = _compact(vk[i]) - _compact(pq_b[i][LANES:2 * LANES])
        return y0d

    kd_f = prepare(0)
    y0_f = chains(0)
    kd_b = prepare(1)

    gl = shifted(3 * W + LORA_COLS, RWKV_COLS)
    gg_ref[0] = _mm(_sigmoid(gl), wg2_ref[...])
    bonus_ref[0] = _head_sum(r * (kd_f + kd_b) * rk_ref[...], bones) * v
    gate_b = z_scr[:, RWKV_COLS:RWKV_COLS + CONV_WIDTH]
    u = z_scr[:, RWKV_COLS + CONV_WIDTH:RWKV_COLS + 2 * CONV_WIDTH] * z_scr[:, RWKV_COLS + 2 * CONV_WIDTH:IN_COLS]
    up, un = neighbours(u)
    oc = gate_b * (cw_ref[0:1, :] * up + cw_ref[1:2, :] * u + cw_ref[2:3, :] * un)
    oc = oc * lax.rsqrt(jnp.mean(oc * oc, axis=-1, keepdims=True) + NORM_EPS) * cg_ref[...]
    oconv_ref[0] = oc.astype(BF16)

    y0_b = chains(1)
    for i in range(len(y0_f)):
        ci, p = divmod(i, PAIRS)
        y0_ref[0, ci * CHUNK:(ci + 1) * CHUNK, p * LANES:(p + 1) * LANES] = y0_f[i] + y0_b[i]
    wtf_ref[0] = rowf_scr[1]
    wtb_ref[0] = rowb_scr[1]


def _phase1(x, mod, roww, lw):
    bsz, seq, _ = x.shape
    nt = seq // TILE
    nc = seq // CHUNK
    mod_map = (lambda b, t: (b, 0, 0)) if mod.shape[0] == bsz else (lambda b, t: (0, 0, 0))
    tok = lambda n: pl.BlockSpec((1, TILE, n), lambda b, t: (b, t, 0))
    mn_spec = pl.BlockSpec((1, CHUNKS_PER_TILE, PAIRS, HEAD_SIZE, LANES), lambda b, t: (b, t, 0, 0, 0))
    W = RWKV_WIDTH
    consts = (lw["norm1_g"], lw["w_in"], lw["mu"], lw["w_lora"], lw["b_lora"], lw["w_g2"], lw["k_k"],
              lw["k_a"], lw["r_k"], lw["conv_w"], lw["conv_gain"], lw["bones"], lw["tri_f"], lw["tri_b"])
    tok_shape = jax.ShapeDtypeStruct((bsz, seq, W), F32)
    tok_bf16 = jax.ShapeDtypeStruct((bsz, seq, W), BF16)
    ml_shape = jax.ShapeDtypeStruct((bsz, nc, PAIRS, HEAD_SIZE, LANES), BF16)
    nn_shape = jax.ShapeDtypeStruct((bsz, nc, PAIRS, HEAD_SIZE, LANES), F32)
    wt_spec = pl.BlockSpec((1, CHUNKS_PER_TILE * SUBLANES, W), lambda b, t: (b, t, 0))
    wt_shape = jax.ShapeDtypeStruct((bsz, nc * SUBLANES, W), F32)
    return pl.pallas_call(
        functools.partial(_phase1_body, roww),
        grid=(bsz, nt),
        in_specs=[tok(D_MODEL), pl.BlockSpec((1, 1, 6 * D_MODEL), mod_map)]
                 + [_const_spec(a.shape) for a in consts],
        out_specs=[tok(W)] * 6 + [mn_spec] * 4 + [wt_spec, wt_spec],
        out_shape=[tok_shape, tok_bf16, tok_bf16, tok_shape, tok_shape, tok_bf16,
                   ml_shape, ml_shape, nn_shape, nn_shape, wt_shape, wt_shape],
        scratch_shapes=[pltpu.VMEM((TILE, IN_COLS), F32),
                        pltpu.VMEM((4, TILE, W), F32),
                        pltpu.VMEM((4, TILE, W), F32),
                        pltpu.VMEM((TILE, W), F32),
                        pltpu.VMEM((3, CHUNKS_PER_TILE * SUBLANES, W), F32),
                        pltpu.VMEM((3, CHUNKS_PER_TILE * SUBLANES, W), F32)],
        compiler_params=pltpu.CompilerParams(dimension_semantics=("parallel", "parallel"),
                                             vmem_limit_bytes=VMEM_LIMIT),
        name="phase1_chunk_summaries",
    )(x, mod, *consts)


def _phase2_body(nb, s0f_ref, s0b_ref, mlf_ref, mlb_ref, nnf_ref, nnb_ref, wtf_ref, wtb_ref,
                 sf_ref, sb_ref, finf_ref, finb_ref, st_scr):
    step = pl.program_id(1)

    @pl.when(step == 0)
    def _():
        st_scr[0] = s0f_ref[...]
        st_scr[1] = s0b_ref[...]

    chains = [(b, d, p) for b in range(nb) for d in range(2) for p in range(PAIRS)]
    ml_refs = (mlf_ref, mlb_ref)
    nn_refs = (nnf_ref, nnb_ref)
    wt_refs = (wtf_ref, wtb_ref)
    s_refs = (sf_ref, sb_ref)
    s = [st_scr[d, b, p] for (b, d, p) in chains]
    for j in range(SCAN_CHUNKS):
        cj = (j, SCAN_CHUNKS - 1 - j)
        for i, (b, d, p) in enumerate(chains):
            s_refs[d][b, cj[d], p] = s[i].astype(BF16)
        low = [_mm(s[i], _block_diag_rhs(ml_refs[d][b, cj[d], p])) for i, (b, d, p) in enumerate(chains)]
        for i, (b, d, p) in enumerate(chains):
            w_row = wt_refs[d][b, cj[d] * SUBLANES:cj[d] * SUBLANES + 1, p * LANES:(p + 1) * LANES]
            s[i] = s[i] * w_row - low[i] + nn_refs[d][b, cj[d], p]
    for i, (b, d, p) in enumerate(chains):
        st_scr[d, b, p] = s[i]

    @pl.when(step == pl.num_programs(1) - 1)
    def _():
        finf_ref[...] = st_scr[0]
        finb_ref[...] = st_scr[1]


def _phase2(s0f, s0b, mlf, mlb, nnf, nnb, wtf, wtb):
    bsz, nc = mlf.shape[0], mlf.shape[1]
    bb = min(bsz, SCAN_BATCH)
    assert bsz % bb == 0 and nc % SCAN_CHUNKS == 0
    ns = nc // SCAN_CHUNKS
    st_blk = (bb, PAIRS, HEAD_SIZE, LANES)
    mn_blk = (bb, SCAN_CHUNKS, PAIRS, HEAD_SIZE, LANES)
    s_blk = (bb, SCAN_CHUNKS, PAIRS, HEAD_SIZE, LANES)
    wt_blk = (bb, SCAN_CHUNKS * SUBLANES, RWKV_WIDTH)
    fwd = lambda g, i: (g, i, 0, 0, 0)
    bwd = lambda g, i: (g, ns - 1 - i, 0, 0, 0)
    full = pl.BlockSpec(st_blk, lambda g, i: (g, 0, 0, 0))
    s_all = jax.ShapeDtypeStruct((bsz, nc, PAIRS, HEAD_SIZE, LANES), BF16)
    s_fin = jax.ShapeDtypeStruct((bsz, PAIRS, HEAD_SIZE, LANES), F32)
    return pl.pallas_call(
        functools.partial(_phase2_body, bb),
        grid=(bsz // bb, ns),
        in_specs=[full, full, pl.BlockSpec(mn_blk, fwd), pl.BlockSpec(mn_blk, bwd),
                  pl.BlockSpec(mn_blk, fwd), pl.BlockSpec(mn_blk, bwd),
                  pl.BlockSpec(wt_blk, lambda g, i: (g, i, 0)),
                  pl.BlockSpec(wt_blk, lambda g, i: (g, ns - 1 - i, 0))],
        out_specs=[pl.BlockSpec(s_blk, fwd), pl.BlockSpec(s_blk, bwd), full, full],
        out_shape=[s_all, s_all, s_fin, s_fin],
        scratch_shapes=[pltpu.VMEM((2,) + st_blk, F32)],
        compiler_params=pltpu.CompilerParams(dimension_semantics=("parallel", "arbitrary"),
                                             vmem_limit_bytes=VMEM_LIMIT),
        name="phase2_state_scan",
    )(s0f, s0b, mlf, mlb, nnf, nnb, wtf, wtb)


def _phase3_body(x_ref, mod_ref, y0_ref, rqf_ref, rqb_ref, bonus_ref, gg_ref, oconv_ref, sf_ref, sb_ref,
                 lnw_ref, lnb_ref, bones_ref, wout_ref, g2_ref, wff1_ref, wff2_ref, gfin_ref,
                 o_ref, y_scr):
    W = RWKV_WIDTH
    mod = mod_ref[0]
    gate_a = mod[:, 2 * D_MODEL:3 * D_MODEL]
    shift_f = mod[:, 3 * D_MODEL:4 * D_MODEL]
    scale_f = mod[:, 4 * D_MODEL:5 * D_MODEL]
    gate_f = mod[:, 5 * D_MODEL:6 * D_MODEL]
    inv_n = 1.0 / HEAD_SIZE
    carried = {}

    def mix(blk):
        r0 = blk * P3_ROWS
        rsl = slice(r0, r0 + P3_ROWS)
        for ci in range(P3_ROWS // CHUNK):
            cg = blk * (P3_ROWS // CHUNK) + ci
            rows = slice(cg * CHUNK, (cg + 1) * CHUNK)
            for p in range(PAIRS):
                ls = slice(p * LANES, (p + 1) * LANES)
                y_scr[rows, ls] = (y0_ref[rows, ls]
                                   + _mm_nt(rqf_ref[rows, ls], _block_diag_rhs(sf_ref[cg, p]))
                                   + _mm_nt(rqb_ref[rows, ls], _block_diag_rhs(sb_ref[cg, p])))
        yield
        y = y_scr[rsl, :]
        yc = y - _head_sum(y, bones_ref[...]) * inv_n
        yield
        y_var = _head_sum(yc * yc, bones_ref[...]) * inv_n
        yn = yc * lax.rsqrt(y_var + GN_EPS) * lnw_ref[...] + lnb_ref[...]
        o_rwkv = (yn + bonus_ref[rsl, :]) * gg_ref[rsl, :]
        yield
        mixed = (_dot(o_rwkv.astype(BF16), wout_ref[0:W, :])
                 + _dot(oconv_ref[rsl, :], wout_ref[W:D_MODEL, :]))
        x1 = x_ref[rsl, :] + gate_a * mixed
        ms = jnp.mean(x1 * x1, axis=-1, keepdims=True)
        h2 = (x1 * lax.rsqrt(ms + NORM_EPS)) * g2_ref[...] * (1.0 + scale_f) + shift_f
        carried[blk] = (x1, h2.astype(BF16))

    def mlp(blk):
        rsl = slice(blk * P3_ROWS, (blk + 1) * P3_ROWS)
        x1, h2 = carried.pop(blk)
        acc = None
        for j in range(D_FF // FF_CHUNK):
            cols = slice(j * FF_CHUNK, (j + 1) * FF_CHUNK)
            f1 = jnp.maximum(_dot(h2, wff1_ref[:, cols]), 0.0)
            f2 = _dot((f1 * f1).astype(BF16), wff2_ref[cols, :])
            acc = f2 if acc is None else acc + f2
            yield
        x2 = x1 + gate_f * acc
        ms2 = jnp.mean(x2 * x2, axis=-1, keepdims=True)
        o_ref[rsl, :] = (x2 * lax.rsqrt(ms2 + NORM_EPS)) * gfin_ref[...]

    def interleave(*gens):
        live = list(gens)
        while live:
            for g in list(live):
                try:
                    next(g)
                except StopIteration:
                    live.remove(g)

    n_blk = P3_TILE // P3_ROWS
    interleave(mix(0))
    for blk in range(1, n_blk):
        interleave(mlp(blk - 1), mix(blk))
    interleave(mlp(n_blk - 1))


def _phase3(x, mod, p1, sf, sb, lw, final_g):
    bsz, seq, _ = x.shape
    W = RWKV_WIDTH
    n_tok = bsz * seq
    nt = n_tok // P3_TILE
    assert n_tok % P3_TILE == 0 and (seq % P3_TILE == 0 or mod.shape[0] == 1)
    tiles_per_batch = max(seq // P3_TILE, 1)
    mod_map = (lambda t: (t // tiles_per_batch, 0, 0)) if mod.shape[0] == bsz else (lambda t: (0, 0, 0))
    tok = lambda n: pl.BlockSpec((P3_TILE, n), lambda t: (t, 0))
    cpt = P3_TILE // CHUNK
    s_spec = pl.BlockSpec((cpt, PAIRS, HEAD_SIZE, LANES), lambda t: (t, 0, 0, 0))
    flat = lambda a: a.reshape((n_tok,) + a.shape[2:])
    flat_s = lambda a: a.reshape((a.shape[0] * a.shape[1],) + a.shape[2:])
    consts = (lw["ln_x_w"], lw["ln_x_b"], lw["bones"], lw["w_out"], lw["norm2_g"], lw["w_ff1"], lw["w_ff2"],
              final_g)
    out = pl.pallas_call(
        _phase3_body,
        grid=(nt,),
        in_specs=[tok(D_MODEL), pl.BlockSpec((1, 1, 6 * D_MODEL), mod_map)] + [tok(W)] * 6 + [s_spec, s_spec]
                 + [_const_spec(a.shape) for a in consts],
        out_specs=tok(D_MODEL),
        out_shape=jax.ShapeDtypeStruct((n_tok, D_MODEL), F32),
        scratch_shapes=[pltpu.VMEM((P3_TILE, W), F32)],
        compiler_params=pltpu.CompilerParams(dimension_semantics=("parallel",),
                                             vmem_limit_bytes=VMEM_LIMIT),
        name="phase3_mix_mlp",
    )(flat(x), mod, *[flat(a) for a in p1], flat_s(sf), flat_s(sb), *consts)
    return out.reshape(bsz, seq, D_MODEL)


def _block_diag(blocks):
    rows = sum(b.shape[0] for b in blocks)
    cols = sum(b.shape[1] for b in blocks)
    out = jnp.zeros((rows, cols), blocks[0].dtype)
    r = c = 0
    for b in blocks:
        out = lax.dynamic_update_slice(out, b, (r, c))
        r += b.shape[0]
        c += b.shape[1]
    return out


def _tri_constants():
    t = jnp.arange(TILE)
    same = (t[:, None] // CHUNK) == (t[None, :] // CHUNK)
    tri_f = (same & (t[None, :] <= t[:, None])).astype(BF16)
    tri_b = (same & (t[None, :] >= t[:, None])).astype(BF16)
    lane = jnp.arange(RWKV_WIDTH)
    bones = ((lane[:, None] // HEAD_SIZE) == (lane[None, :] // HEAD_SIZE)).astype(BF16)
    return tri_f, tri_b, bones


def _pair_compact(s):
    b = s.shape[0]
    s = s.reshape(b, PAIRS, 2, HEAD_SIZE, HEAD_SIZE)
    return jnp.swapaxes(s, 2, 3).reshape(b, PAIRS, HEAD_SIZE, LANES)


def _pair_expand(sc):
    b = sc.shape[0]
    s = sc.reshape(b, PAIRS, HEAD_SIZE, 2, HEAD_SIZE)
    return jnp.swapaxes(s, 2, 3).reshape(b, RWKV_HEADS, HEAD_SIZE, HEAD_SIZE)


def kernel(x_prompt, x_sample, c, state_rwkv_fwd, state_rwkv_bwd, c_ctx, w_mod, b_mod, norm1_g, w_in, mu_shift, w0_f, w_w2_f, w0_b, w_w2_b, a0_f, w_a2_f, a0_b, w_a2_b, w_g2, k_k, k_a, r_k, ln_x_w, ln_x_b, conv_w, conv_gain, w_out, norm2_g, w_ff1, w_ff2, final_g):
    depth = w_in.shape[0]
    b_ctx, seq_ctx, _ = x_prompt.shape
    b_lat, seq_lat, _ = x_sample.shape
    assert depth == 1
    assert seq_ctx == TILE and seq_lat % TILE == 0 and TILE % GRID_W == 0
    tri_f, tri_b, bones = _tri_constants()
    row = lambda a: a.reshape(1, -1)
    n_pad = (-(b_lat + 1)) % SUBLANES
    cvec = jnp.concatenate([c, c_ctx[None, :], jnp.zeros((n_pad, D_MODEL), F32)], axis=0)
    zero_state = jnp.zeros((b_ctx, PAIRS, HEAD_SIZE, LANES), F32)

    i = 0
    lw = dict(
        norm1_g=row(norm1_g[i]), w_in=w_in[i].astype(BF16), mu=row(mu_shift[i]),
        w_lora=_block_diag([w_w2_f[i], w_w2_b[i], w_a2_f[i], w_a2_b[i]]),
        b_lora=jnp.concatenate([w0_f[i], w0_b[i], a0_f[i], a0_b[i]]).reshape(1, -1),
        w_g2=w_g2[i], k_k=row(k_k[i]), k_a=row(k_a[i]), r_k=row(r_k[i]),
        conv_w=conv_w[i], conv_gain=row(conv_gain[i]), bones=bones, tri_f=tri_f, tri_b=tri_b,
        ln_x_w=row(ln_x_w[i]), ln_x_b=row(ln_x_b[i]), w_out=w_out[i].astype(BF16),
        norm2_g=row(norm2_g[i]), w_ff1=w_ff1[i].astype(BF16), w_ff2=w_ff2[i].astype(BF16))
    mod = _modulation(cvec, w_mod[i], row(b_mod[i]))
    mod_lat = mod[:b_lat].reshape(b_lat, 1, 6 * D_MODEL)
    mod_ctx = mod[b_lat:b_lat + 1].reshape(1, 1, 6 * D_MODEL)

    def run(x, mod_s, roww, s0f, s0b):
        *p1, mlf, mlb, nnf, nnb, wtf, wtb = _phase1(x, mod_s, roww, lw)
        sf, sb, fin_f, fin_b = _phase2(s0f, s0b, mlf, mlb, nnf, nnb, wtf, wtb)
        return _phase3(x, mod_s, p1, sf, sb, lw, row(final_g)), fin_f, fin_b

    y_ctx, s_f, s_b = run(x_prompt, mod_ctx, seq_ctx, zero_state, zero_state)
    y_lat, _, _ = run(x_sample, mod_lat, GRID_W, _pair_compact(state_rwkv_fwd[:, i]),
                      _pair_compact(state_rwkv_bwd[:, i]))
    return (y_ctx, y_lat, _pair_expand(s_f)[:, None], _pair_expand(s_b)[:, None])
```

```python
import functools
import math

import jax
import jax.numpy as jnp
from jax import lax
from jax.experimental import pallas as pl
from jax.experimental.pallas import tpu as pltpu

D_MODEL = 1024
GRID_W = 64
RWKV_WIDTH = D_MODEL // 2
HEAD_SIZE = 64
RWKV_HEADS = RWKV_WIDTH // HEAD_SIZE
CONV_WIDTH = D_MODEL - RWKV_WIDTH
LORA_COLS = 256
GATE_LORA = 128
RWKV_COLS = 3 * RWKV_WIDTH + LORA_COLS + GATE_LORA
IN_COLS = RWKV_COLS + 3 * CONV_WIDTH
D_FF = 4 * D_MODEL
NORM_EPS = 1e-6
GN_EPS = HEAD_SIZE * 1e-5

LANES = 128
SUBLANES = 8
PAIRS = RWKV_WIDTH // LANES
CHUNK = 64
TILE = 256
CHUNKS_PER_TILE = TILE // CHUNK
INV_BASE = 8
DECAY_SCALE = math.exp(-0.5)
P3_TILE = 512
P3_ROWS = 256
FF_CHUNK = 1024
SCAN_CHUNKS = 4
SCAN_BATCH = 8
VMEM_LIMIT = 56 * 1024 * 1024

assert 2 * HEAD_SIZE == LANES and CHUNK == HEAD_SIZE

F32 = jnp.float32
BF16 = jnp.bfloat16
HI = lax.Precision.HIGHEST


def _dot(a, b):
    return jnp.dot(a, b, preferred_element_type=F32)


def _dot_hi(a, b):
    return jnp.dot(a, b, precision=HI, preferred_element_type=F32)


def _mm(a, b):
    return jnp.dot(a.astype(BF16), b.astype(BF16), preferred_element_type=F32)


def _mm_nt(a, b):
    return lax.dot_general(a.astype(BF16), b.astype(BF16), (((1,), (1,)), ((), ())),
                           preferred_element_type=F32)


def _mm_tn(a, b):
    return lax.dot_general(a.astype(BF16), b.astype(BF16), (((0,), (0,)), ((), ())),
                           preferred_element_type=F32)


def _split2(x):
    hi = x.astype(BF16)
    lo = (x - hi.astype(F32)).astype(BF16)
    return hi, lo


def _head_sum(x, bones):
    return _dot(x.astype(BF16), bones)


def _sigmoid(x):
    return 1.0 / (1.0 + jnp.exp(-x))


def _block_diag_rhs(xc):
    first = lax.broadcasted_iota(jnp.int32, xc.shape, 1) < HEAD_SIZE
    zero = jnp.zeros_like(xc)
    return jnp.concatenate([jnp.where(first, xc, zero), jnp.where(first, zero, xc)], axis=0)


def _compact(xbd):
    first = lax.broadcasted_iota(jnp.int32, (HEAD_SIZE, LANES), 1) < HEAD_SIZE
    return jnp.where(first, xbd[0:HEAD_SIZE], xbd[HEAD_SIZE:2 * HEAD_SIZE])


def _const_spec(shape):
    nd = len(shape)
    return pl.BlockSpec(shape, lambda *_: (0,) * nd, pipeline_mode=pl.Buffered(1))


def _mod_body(c_ref, w_ref, b_ref, o_ref):
    cv = c_ref[...]
    s = cv * _sigmoid(cv)
    o_ref[...] = _dot_hi(s, w_ref[...]) + b_ref[...]


def _modulation(cvec, w_mod, b_mod):
    rows = cvec.shape[0]
    n = w_mod.shape[1]
    bn = 1536
    return pl.pallas_call(
        _mod_body,
        grid=(n // bn,),
        in_specs=[pl.BlockSpec((rows, D_MODEL), lambda j: (0, 0)),
                  pl.BlockSpec((D_MODEL, bn), lambda j: (0, j)),
                  pl.BlockSpec((1, bn), lambda j: (0, j))],
        out_specs=pl.BlockSpec((rows, bn), lambda j: (0, j)),
        out_shape=jax.ShapeDtypeStruct((rows, n), F32),
        compiler_params=pltpu.CompilerParams(dimension_semantics=("arbitrary",),
                                             vmem_limit_bytes=VMEM_LIMIT),
        name="modulation",
    )(cvec, w_mod, b_mod)


def _phase1_body(roww, x_ref, mod_ref, g1_ref, win_ref, mu_ref, wl_ref, bl_ref, wg2_ref, kk_ref,
                 ka_ref, rk_ref, cw_ref, cg_ref, bones_ref, trif_ref, trib_ref,
                 y0_ref, rqf_ref, rqb_ref, bonus_ref, gg_ref, oconv_ref, mlf_ref, mlb_ref,
                 nnf_ref, nnb_ref, wtf_ref, wtb_ref,
                 z_scr, opsf_scr, opsb_scr, v_scr, rowf_scr, rowb_scr):
    W = RWKV_WIDTH
    x = x_ref[0]
    mod = mod_ref[0]
    shift_a = mod[:, 0:D_MODEL]
    scale_a = mod[:, D_MODEL:2 * D_MODEL]
    ms = jnp.mean(x * x, axis=-1, keepdims=True)
    h = (x * lax.rsqrt(ms + NORM_EPS)) * g1_ref[...] * (1.0 + scale_a) + shift_a
    h_bf = h.astype(BF16)
    z_scr[:, 3 * W:RWKV_COLS] = _dot(h_bf, win_ref[:, 3 * W:RWKV_COLS])
    z_scr[:, 0:3 * W] = _dot(h_bf, win_ref[:, 0:3 * W])

    t_idx = lax.broadcasted_iota(jnp.int32, (TILE, 1), 0)
    pos = lax.rem(t_idx, roww)
    is_first = pos == 0
    is_last = pos == roww - 1

    def neighbours(zz):
        prev = jnp.where(is_first, 0.0, pltpu.roll(zz, 1, 0))
        nxt = jnp.where(is_last, 0.0, pltpu.roll(zz, TILE - 1, 0))
        return prev, nxt

    def shifted(lo, hi):
        zz = z_scr[:, lo:hi]
        prev, nxt = neighbours(zz)
        mu = mu_ref[:, lo:hi]
        return (1.0 - mu) * zz + (0.5 * mu) * (prev + nxt)

    lora = shifted(3 * W, 3 * W + LORA_COLS)
    lora_in = jnp.concatenate([jnp.tanh(lora[:, :LANES]), lora[:, LANES:]], axis=1)
    pre = _mm(lora_in, wl_ref[...]) + bl_ref[...]
    z_scr[:, RWKV_COLS:IN_COLS] = _dot(h_bf, win_ref[:, RWKV_COLS:IN_COLS])

    r = shifted(0, W)
    k = shifted(W, 2 * W)
    v = shifted(2 * W, 3 * W)
    v_scr[...] = v

    bones = bones_ref[...]
    kraw = k * kk_ref[...]
    kk = kraw * lax.rsqrt(_head_sum(kraw * kraw, bones) + 1e-12)

    log_decay, cum = [], []
    for d in range(2):
        lw = -DECAY_SCALE * _sigmoid(pre[:, d * W:(d + 1) * W])
        tri = trif_ref[...] if d == 0 else trib_ref[...]
        l_hi, l_lo = _split2(lw)
        log_decay.append(lw)
        cum.append(_dot(tri, l_hi) + _dot(tri, l_lo))

    def prepare(d):
        ops_scr, row_scr = (opsf_scr, rowf_scr) if d == 0 else (opsb_scr, rowb_scr)
        lw = log_decay[d]
        a = _sigmoid(pre[:, (2 + d) * W:(3 + d) * W])
        kd = k * (1.0 + (a - 1.0) * ka_ref[...])
        b = kk * a
        c3 = cum[d].reshape(CHUNKS_PER_TILE, CHUNK, W)
        mid = c3[:, CHUNK // 2:CHUNK // 2 + 1, :]
        end_row = CHUNK - 1 if d == 0 else 0
        end = c3[:, end_row:end_row + 1, :]
        cm = (c3 - mid).reshape(TILE, W)
        e_r = jnp.exp(cm)
        e_kap = jnp.exp(cm - lw)
        e_inv = jnp.exp(-cm)
        ops_scr[0] = kk * e_kap
        ops_scr[1] = r * e_r
        ops_scr[2] = kd * e_inv
        ops_scr[3] = b * e_inv
        for j, rv in enumerate((jnp.exp(end - mid), jnp.exp(end), jnp.exp(mid))):
            row_scr[j] = jnp.broadcast_to(rv, (CHUNKS_PER_TILE, SUBLANES, W)).reshape(
                CHUNKS_PER_TILE * SUBLANES, W)
        return kd

    row = lax.broadcasted_iota(jnp.int32, (CHUNK, LANES), 0)
    col = lax.bitwise_and(lax.broadcasted_iota(jnp.int32, (CHUNK, LANES), 1), HEAD_SIZE - 1)
    eye = row == col
    same_block = {}
    size = INV_BASE
    while size <= CHUNK:
        same_block[size] = (row // size) == (col // size)
        size *= 2
    masks = (jnp.concatenate([row > col, row >= col], axis=0),
             jnp.concatenate([row < col, row <= col], axis=0))
    ml_refs = (mlf_ref, mlb_ref)
    nn_refs = (nnf_ref, nnb_ref)
    rq_refs = (rqf_ref, rqb_ref)

    def chains(d):
        ops_scr, row_scr = (opsf_scr, rowf_scr) if d == 0 else (opsb_scr, rowb_scr)
        rows = [pl.ds(ci * CHUNK, CHUNK) for ci in range(CHUNKS_PER_TILE)]
        crow = [pl.ds(ci * SUBLANES, 1) for ci in range(CHUNKS_PER_TILE)]
        lsl = [slice(p * LANES, (p + 1) * LANES) for p in range(PAIRS)]
        ch = [(c, p) for c in range(CHUNKS_PER_TILE) for p in range(PAIRS)]
        n = len(ch)

        v_pair = [v_scr[rows[c], lsl[p]] for (c, p) in ch]
        vbd = [_block_diag_rhs(vp.astype(BF16)) for vp in v_pair]
        kap = [ops_scr[0, rows[c], lsl[p]] for (c, p) in ch]
        rt = [ops_scr[1, rows[c], lsl[p]] for (c, p) in ch]
        kt = [ops_scr[2, rows[c], lsl[p]] for (c, p) in ch]
        bt = [ops_scr[3, rows[c], lsl[p]] for (c, p) in ch]
        to_end = [row_scr[0, crow[c], lsl[p]] for (c, p) in ch]
        e_mid = [row_scr[2, crow[c], lsl[p]] for (c, p) in ch]
        lhs = [jnp.concatenate([kap[i], rt[i]], axis=0).astype(BF16) for i in range(n)]
        a_all = [_mm_nt(lhs[i], jnp.concatenate([_block_diag_rhs(kt[i].astype(BF16)),
                                                 _block_diag_rhs(bt[i].astype(BF16))], axis=0))
                 for i in range(n)]
        a_k = [jnp.where(masks[d], a_all[i][:, 0:LANES], 0.0) for i in range(n)]
        a_b = [jnp.where(masks[d], a_all[i][:, LANES:2 * LANES], 0.0) for i in range(n)]
        a_kb = [x[0:CHUNK] for x in a_b]
        a_rb = [x[CHUNK:2 * CHUNK] for x in a_b]
        av = [_mm(a_k[i], vbd[i]) for i in range(n)]
        l_diag = [jnp.where(same_block[INV_BASE], x, 0.0) for x in a_kb]
        t_inv = [jnp.where(eye, 1.0, 0.0) - x for x in l_diag]
        pw = [_mm(x, _block_diag_rhs(x.astype(BF16))) for x in l_diag]
        n_fac = 2
        while 2 * n_fac < INV_BASE:
            both = [_mm(jnp.concatenate([t, w], axis=0), _block_diag_rhs(w.astype(BF16)))
                    for t, w in zip(t_inv, pw)]
            t_inv = [t + x[0:CHUNK] for t, x in zip(t_inv, both)]
            pw = [x[CHUNK:2 * CHUNK] for x in both]
            n_fac *= 2
        t_inv = [t + _mm(t, _block_diag_rhs(w.astype(BF16))) for t, w in zip(t_inv, pw)]
        sizes = []
        size = INV_BASE
        while size < CHUNK:
            sizes.append(size)
            size *= 2
        half = CHUNK // 2

        def hit(s):
            return [j for j in range(CHUNK // s) if j % 2 == (1 if d == 0 else 0)]

        def gather(x, s):
            return jnp.concatenate([x[j * s:(j + 1) * s] for j in hit(s)], axis=0)

        def scatter(base, upd, s):
            rows_of = {j: i for i, j in enumerate(hit(s))}
            parts = [upd[rows_of[j] * s:(rows_of[j] + 1) * s] if j in rows_of
                     else (jnp.zeros((s, LANES), F32) if base is None else base[j * s:(j + 1) * s])
                     for j in range(CHUNK // s)]
            return jnp.concatenate(parts, axis=0)

        off = [[gather(jnp.where(same_block[2 * s] & ~same_block[s], x, 0.0), s) for s in sizes]
               for x in a_kb]
        y_all = [_mm(jnp.concatenate(o, axis=0), _block_diag_rhs(t.astype(BF16)))
                 for o, t in zip(off, t_inv)]
        y_lvl = [[y[j * half:(j + 1) * half] for j in range(len(sizes))] for y in y_all]
        for li, s in enumerate(sizes):
            lhs_m = [jnp.concatenate([gather(t, s)] + y[li + 1:], axis=0) for t, y in zip(t_inv, y_lvl)]
            prod = [_mm(l, _block_diag_rhs(scatter(None, y[li], s).astype(BF16)))
                    for l, y in zip(lhs_m, y_lvl)]
            t_inv = [scatter(t, gather(t, s) - x[0:half], s) for t, x in zip(t_inv, prod)]
            y_lvl = [y[:li + 1] + [y[j] - x[(j - li) * half:(j - li + 1) * half]
                                   for j in range(li + 1, len(sizes))]
                     for y, x in zip(y_lvl, prod)]
        rhs = [jnp.concatenate([_block_diag_rhs((kap[i] * e_mid[i]).astype(BF16)),
                                _block_diag_rhs(av[i][0:CHUNK].astype(BF16))], axis=1) for i in range(n)]
        pq = [_mm(t_inv[i], rhs[i]) for i in range(n)]
        rhs2 = [jnp.concatenate([_block_diag_rhs(x[:, 0:LANES].astype(BF16)),
                                 _block_diag_rhs(x[:, LANES:2 * LANES].astype(BF16))], axis=1) for x in pq]
        rb = [_mm(a_rb[i], rhs2[i]) for i in range(n)]
        for i, (c, p) in enumerate(ch):
            rq_refs[d][0, rows[c], lsl[p]] = (rt[i] * e_mid[i] - rb[i][:, 0:LANES]).astype(BF16)
        y0d = [av[i][CHUNK:2 * CHUNK] - rb[i][:, LANES:2 * LANES] for i in range(n)]
        b_end = [(bt[i] * to_end[i]).astype(BF16) for i in range(n)]
        k_end = [(kt[i] * to_end[i]).astype(BF16) for i in range(n)]
        pq_b = [_mm(pq[i].T, b_end[i]) for i in range(n)]
        vk = [_mm(v_pair[i].T, k_end[i]) for i in range(n)]
        for i, (c, p) in enumerate(ch):
            ml_refs[d][0, c, p] = _compact(pq_b[i][0:LANES]).astype(BF16)
            nn_refs[d][0, c, p] = (_compact(vk[i]) - _compact(pq_b[i][LANES:2 * LANES])).astype(BF16)
        return y0d

    kd_f = prepare(0)
    y0_f = chains(0)
    kd_b = prepare(1)

    gl = shifted(3 * W + LORA_COLS, RWKV_COLS)
    gg_ref[0] = _mm(_sigmoid(gl), wg2_ref[...])
    bonus_ref[0] = _head_sum(r * (kd_f + kd_b) * rk_ref[...], bones) * v
    gate_b = z_scr[:, RWKV_COLS:RWKV_COLS + CONV_WIDTH]
    u = z_scr[:, RWKV_COLS + CONV_WIDTH:RWKV_COLS + 2 * CONV_WIDTH] * z_scr[:, RWKV_COLS + 2 * CONV_WIDTH:IN_COLS]
    up, un = neighbours(u)
    oc = gate_b * (cw_ref[0:1, :] * up + cw_ref[1:2, :] * u + cw_ref[2:3, :] * un)
    oc = oc * lax.rsqrt(jnp.mean(oc * oc, axis=-1, keepdims=True) + NORM_EPS) * cg_ref[...]
    oconv_ref[0] = oc.astype(BF16)

    y0_b = chains(1)
    for i in range(len(y0_f)):
        ci, p = divmod(i, PAIRS)
        y0_ref[0, ci * CHUNK:(ci + 1) * CHUNK, p * LANES:(p + 1) * LANES] = y0_f[i] + y0_b[i]
    wtf_ref[0] = rowf_scr[1]
    wtb_ref[0] = rowb_scr[1]


def _phase1(x, mod, roww, lw):
    bsz, seq, _ = x.shape
    nt = seq // TILE
    nc = seq // CHUNK
    mod_map = (lambda b, t: (b, 0, 0)) if mod.shape[0] == bsz else (lambda b, t: (0, 0, 0))
    tok = lambda n: pl.BlockSpec((1, TILE, n), lambda b, t: (b, t, 0))
    mn_spec = pl.BlockSpec((1, CHUNKS_PER_TILE, PAIRS, HEAD_SIZE, LANES), lambda b, t: (b, t, 0, 0, 0))
    W = RWKV_WIDTH
    consts = (lw["norm1_g"], lw["w_in"], lw["mu"], lw["w_lora"], lw["b_lora"], lw["w_g2"], lw["k_k"],
              lw["k_a"], lw["r_k"], lw["conv_w"], lw["conv_gain"], lw["bones"], lw["tri_f"], lw["tri_b"])
    tok_shape = jax.ShapeDtypeStruct((bsz, seq, W), F32)
    tok_bf16 = jax.ShapeDtypeStruct((bsz, seq, W), BF16)
    ml_shape = jax.ShapeDtypeStruct((bsz, nc, PAIRS, HEAD_SIZE, LANES), BF16)
    nn_shape = jax.ShapeDtypeStruct((bsz, nc, PAIRS, HEAD_SIZE, LANES), BF16)
    wt_spec = pl.BlockSpec((1, CHUNKS_PER_TILE * SUBLANES, W), lambda b, t: (b, t, 0))
    wt_shape = jax.ShapeDtypeStruct((bsz, nc * SUBLANES, W), F32)
    return pl.pallas_call(
        functools.partial(_phase1_body, roww),
        grid=(bsz, nt),
        in_specs=[tok(D_MODEL), pl.BlockSpec((1, 1, 6 * D_MODEL), mod_map)]
                 + [_const_spec(a.shape) for a in consts],
        out_specs=[tok(W)] * 6 + [mn_spec] * 4 + [wt_spec, wt_spec],
        out_shape=[tok_shape, tok_bf16, tok_bf16, tok_shape, tok_shape, tok_bf16,
                   ml_shape, ml_shape, nn_shape, nn_shape, wt_shape, wt_shape],
        scratch_shapes=[pltpu.VMEM((TILE, IN_COLS), F32),
                        pltpu.VMEM((4, TILE, W), F32),
                        pltpu.VMEM((4, TILE, W), F32),
                        pltpu.VMEM((TILE, W), F32),
                        pltpu.VMEM((3, CHUNKS_PER_TILE * SUBLANES, W), F32),
                        pltpu.VMEM((3, CHUNKS_PER_TILE * SUBLANES, W), F32)],
        compiler_params=pltpu.CompilerParams(dimension_semantics=("parallel", "parallel"),
                                             vmem_limit_bytes=VMEM_LIMIT),
        name="phase1_chunk_summaries",
    )(x, mod, *consts)


def _phase2_body(nb, s0f_ref, s0b_ref, mlf_ref, mlb_ref, nnf_ref, nnb_ref, wtf_ref, wtb_ref,
                 sf_ref, sb_ref, finf_ref, finb_ref, st_scr):
    step = pl.program_id(1)

    @pl.when(step == 0)
    def _():
        st_scr[0] = s0f_ref[...]
        st_scr[1] = s0b_ref[...]

    chains = [(b, d, p) for b in range(nb) for d in range(2) for p in range(PAIRS)]
    ml_refs = (mlf_ref, mlb_ref)
    nn_refs = (nnf_ref, nnb_ref)
    wt_refs = (wtf_ref, wtb_ref)
    s_refs = (sf_ref, sb_ref)
    s = [st_scr[d, b, p] for (b, d, p) in chains]
    for j in range(SCAN_CHUNKS):
        cj = (j, SCAN_CHUNKS - 1 - j)
        for i, (b, d, p) in enumerate(chains):
            s_refs[d][b, cj[d], p] = s[i].astype(BF16)
        low = [_mm(s[i], _block_diag_rhs(ml_refs[d][b, cj[d], p])) for i, (b, d, p) in enumerate(chains)]
        for i, (b, d, p) in enumerate(chains):
            w_row = wt_refs[d][b, cj[d] * SUBLANES:cj[d] * SUBLANES + 1, p * LANES:(p + 1) * LANES]
            s[i] = s[i] * w_row - low[i] + nn_refs[d][b, cj[d], p]
    for i, (b, d, p) in enumerate(chains):
        st_scr[d, b, p] = s[i]

    @pl.when(step == pl.num_programs(1) - 1)
    def _():
        for i, (b, d, p) in enumerate(chains):
            fin_ref = finf_ref if d == 0 else finb_ref
            fin_ref[b, 2 * p] = s[i][:, 0:HEAD_SIZE]
            fin_ref[b, 2 * p + 1] = s[i][:, HEAD_SIZE:2 * HEAD_SIZE]


def _phase2(s0f, s0b, mlf, mlb, nnf, nnb, wtf, wtb):
    bsz, nc = mlf.shape[0], mlf.shape[1]
    bb = min(bsz, SCAN_BATCH)
    assert bsz % bb == 0 and nc % SCAN_CHUNKS == 0
    ns = nc // SCAN_CHUNKS
    st_blk = (bb, PAIRS, HEAD_SIZE, LANES)
    mn_blk = (bb, SCAN_CHUNKS, PAIRS, HEAD_SIZE, LANES)
    s_blk = (bb, SCAN_CHUNKS, PAIRS, HEAD_SIZE, LANES)
    wt_blk = (bb, SCAN_CHUNKS * SUBLANES, RWKV_WIDTH)
    fwd = lambda g, i: (g, i, 0, 0, 0)
    bwd = lambda g, i: (g, ns - 1 - i, 0, 0, 0)
    full = pl.BlockSpec(st_blk, lambda g, i: (g, 0, 0, 0))
    s_all = jax.ShapeDtypeStruct((bsz, nc, PAIRS, HEAD_SIZE, LANES), BF16)
    fin_spec = pl.BlockSpec((bb, RWKV_HEADS, HEAD_SIZE, HEAD_SIZE), lambda g, i: (g, 0, 0, 0))
    s_fin = jax.ShapeDtypeStruct((bsz, RWKV_HEADS, HEAD_SIZE, HEAD_SIZE), F32)
    return pl.pallas_call(
        functools.partial(_phase2_body, bb),
        grid=(bsz // bb, ns),
        in_specs=[full, full, pl.BlockSpec(mn_blk, fwd), pl.BlockSpec(mn_blk, bwd),
                  pl.BlockSpec(mn_blk, fwd), pl.BlockSpec(mn_blk, bwd),
                  pl.BlockSpec(wt_blk, lambda g, i: (g, i, 0)),
                  pl.BlockSpec(wt_blk, lambda g, i: (g, ns - 1 - i, 0))],
        out_specs=[pl.BlockSpec(s_blk, fwd), pl.BlockSpec(s_blk, bwd), fin_spec, fin_spec],
        out_shape=[s_all, s_all, s_fin, s_fin],
        scratch_shapes=[pltpu.VMEM((2,) + st_blk, F32)],
        compiler_params=pltpu.CompilerParams(dimension_semantics=("parallel", "arbitrary"),
                                             vmem_limit_bytes=VMEM_LIMIT),
        name="phase2_state_scan",
    )(s0f, s0b, mlf, mlb, nnf, nnb, wtf, wtb)


def _phase3_body(x_ref, mod_ref, y0_ref, rqf_ref, rqb_ref, bonus_ref, gg_ref, oconv_ref, sf_ref, sb_ref,
                 lnw_ref, lnb_ref, bones_ref, wout_ref, g2_ref, wff1_ref, wff2_ref, gfin_ref,
                 o_ref, y_scr):
    W = RWKV_WIDTH
    mod = mod_ref[0]
    gate_a = mod[:, 2 * D_MODEL:3 * D_MODEL]
    shift_f = mod[:, 3 * D_MODEL:4 * D_MODEL]
    scale_f = mod[:, 4 * D_MODEL:5 * D_MODEL]
    gate_f = mod[:, 5 * D_MODEL:6 * D_MODEL]
    inv_n = 1.0 / HEAD_SIZE
    carried = {}

    def mix(blk):
        r0 = blk * P3_ROWS
        rsl = slice(r0, r0 + P3_ROWS)
        for ci in range(P3_ROWS // CHUNK):
            cg = blk * (P3_ROWS // CHUNK) + ci
            rows = slice(cg * CHUNK, (cg + 1) * CHUNK)
            for p in range(PAIRS):
                ls = slice(p * LANES, (p + 1) * LANES)
                y_scr[rows, ls] = (y0_ref[rows, ls]
                                   + _mm_nt(rqf_ref[rows, ls], _block_diag_rhs(sf_ref[cg, p]))
                                   + _mm_nt(rqb_ref[rows, ls], _block_diag_rhs(sb_ref[cg, p])))
        yield
        y = y_scr[rsl, :]
        yc = y - _head_sum(y, bones_ref[...]) * inv_n
        yield
        y_var = _head_sum(yc * yc, bones_ref[...]) * inv_n
        yn = yc * lax.rsqrt(y_var + GN_EPS) * lnw_ref[...] + lnb_ref[...]
        o_rwkv = (yn + bonus_ref[rsl, :]) * gg_ref[rsl, :]
        yield
        mixed = (_dot(o_rwkv.astype(BF16), wout_ref[0:W, :])
                 + _dot(oconv_ref[rsl, :], wout_ref[W:D_MODEL, :]))
        x1 = x_ref[rsl, :] + gate_a * mixed
        ms = jnp.mean(x1 * x1, axis=-1, keepdims=True)
        h2 = (x1 * lax.rsqrt(ms + NORM_EPS)) * g2_ref[...] * (1.0 + scale_f) + shift_f
        carried[blk] = (x1, h2.astype(BF16))

    def mlp(blk):
        rsl = slice(blk * P3_ROWS, (blk + 1) * P3_ROWS)
        x1, h2 = carried.pop(blk)
        acc = None
        for j in range(D_FF // FF_CHUNK):
            cols = slice(j * FF_CHUNK, (j + 1) * FF_CHUNK)
            f1 = jnp.maximum(_dot(h2, wff1_ref[:, cols]), 0.0)
            f2 = _dot((f1 * f1).astype(BF16), wff2_ref[cols, :])
            acc = f2 if acc is None else acc + f2
            yield
        x2 = x1 + gate_f * acc
        ms2 = jnp.mean(x2 * x2, axis=-1, keepdims=True)
        o_ref[rsl, :] = (x2 * lax.rsqrt(ms2 + NORM_EPS)) * gfin_ref[...]

    def interleave(*gens):
        live = list(gens)
        while live:
            for g in list(live):
                try:
                    next(g)
                except StopIteration:
                    live.remove(g)

    n_blk = P3_TILE // P3_ROWS
    interleave(mix(0))
    for blk in range(1, n_blk):
        interleave(mlp(blk - 1), mix(blk))
    interleave(mlp(n_blk - 1))


def _phase3(x, mod, p1, sf, sb, lw, final_g):
    bsz, seq, _ = x.shape
    W = RWKV_WIDTH
    n_tok = bsz * seq
    nt = n_tok // P3_TILE
    assert n_tok % P3_TILE == 0 and (seq % P3_TILE == 0 or mod.shape[0] == 1)
    tiles_per_batch = max(seq // P3_TILE, 1)
    mod_map = (lambda t: (t // tiles_per_batch, 0, 0)) if mod.shape[0] == bsz else (lambda t: (0, 0, 0))
    tok = lambda n: pl.BlockSpec((P3_TILE, n), lambda t: (t, 0))
    cpt = P3_TILE // CHUNK
    s_spec = pl.BlockSpec((cpt, PAIRS, HEAD_SIZE, LANES), lambda t: (t, 0, 0, 0))
    flat = lambda a: a.reshape((n_tok,) + a.shape[2:])
    flat_s = lambda a: a.reshape((a.shape[0] * a.shape[1],) + a.shape[2:])
    consts = (lw["ln_x_w"], lw["ln_x_b"], lw["bones"], lw["w_out"], lw["norm2_g"], lw["w_ff1"], lw["w_ff2"],
              final_g)
    out = pl.pallas_call(
        _phase3_body,
        grid=(nt,),
        in_specs=[tok(D_MODEL), pl.BlockSpec((1, 1, 6 * D_MODEL), mod_map)] + [tok(W)] * 6 + [s_spec, s_spec]
                 + [_const_spec(a.shape) for a in consts],
        out_specs=tok(D_MODEL),
        out_shape=jax.ShapeDtypeStruct((n_tok, D_MODEL), F32),
        scratch_shapes=[pltpu.VMEM((P3_TILE, W), F32)],
        compiler_params=pltpu.CompilerParams(dimension_semantics=("parallel",),
                                             vmem_limit_bytes=VMEM_LIMIT),
        name="phase3_mix_mlp",
    )(flat(x), mod, *[flat(a) for a in p1], flat_s(sf), flat_s(sb), *consts)
    return out.reshape(bsz, seq, D_MODEL)


def _block_diag(blocks):
    rows = sum(b.shape[0] for b in blocks)
    cols = sum(b.shape[1] for b in blocks)
    out = jnp.zeros((rows, cols), blocks[0].dtype)
    r = c = 0
    for b in blocks:
        out = lax.dynamic_update_slice(out, b, (r, c))
        r += b.shape[0]
        c += b.shape[1]
    return out


def _tri_constants():
    t = jnp.arange(TILE)
    same = (t[:, None] // CHUNK) == (t[None, :] // CHUNK)
    tri_f = (same & (t[None, :] <= t[:, None])).astype(BF16)
    tri_b = (same & (t[None, :] >= t[:, None])).astype(BF16)
    lane = jnp.arange(RWKV_WIDTH)
    bones = ((lane[:, None] // HEAD_SIZE) == (lane[None, :] // HEAD_SIZE)).astype(BF16)
    return tri_f, tri_b, bones


def _pair_compact(s):
    b = s.shape[0]
    s = s.reshape(b, PAIRS, 2, HEAD_SIZE, HEAD_SIZE)
    return jnp.swapaxes(s, 2, 3).reshape(b, PAIRS, HEAD_SIZE, LANES)


def kernel(x_prompt, x_sample, c, state_rwkv_fwd, state_rwkv_bwd, c_ctx, w_mod, b_mod, norm1_g, w_in, mu_shift, w0_f, w_w2_f, w0_b, w_w2_b, a0_f, w_a2_f, a0_b, w_a2_b, w_g2, k_k, k_a, r_k, ln_x_w, ln_x_b, conv_w, conv_gain, w_out, norm2_g, w_ff1, w_ff2, final_g):
    depth = w_in.shape[0]
    b_ctx, seq_ctx, _ = x_prompt.shape
    b_lat, seq_lat, _ = x_sample.shape
    assert depth == 1
    assert seq_ctx == TILE and seq_lat % TILE == 0 and TILE % GRID_W == 0
    tri_f, tri_b, bones = _tri_constants()
    row = lambda a: a.reshape(1, -1)
    n_pad = (-(b_lat + 1)) % SUBLANES
    cvec = jnp.concatenate([c, c_ctx[None, :], jnp.zeros((n_pad, D_MODEL), F32)], axis=0)
    zero_state = jnp.zeros((b_ctx, PAIRS, HEAD_SIZE, LANES), F32)

    i = 0
    lw = dict(
        norm1_g=row(norm1_g[i]), w_in=w_in[i].astype(BF16), mu=row(mu_shift[i]),
        w_lora=_block_diag([w_w2_f[i], w_w2_b[i], w_a2_f[i], w_a2_b[i]]),
        b_lora=jnp.concatenate([w0_f[i], w0_b[i], a0_f[i], a0_b[i]]).reshape(1, -1),
        w_g2=w_g2[i], k_k=row(k_k[i]), k_a=row(k_a[i]), r_k=row(r_k[i]),
        conv_w=conv_w[i], conv_gain=row(conv_gain[i]), bones=bones, tri_f=tri_f, tri_b=tri_b,
        ln_x_w=row(ln_x_w[i]), ln_x_b=row(ln_x_b[i]), w_out=w_out[i].astype(BF16),
        norm2_g=row(norm2_g[i]), w_ff1=w_ff1[i].astype(BF16), w_ff2=w_ff2[i].astype(BF16))
    mod = _modulation(cvec, w_mod[i], row(b_mod[i]))
    mod_lat = mod[:b_lat].reshape(b_lat, 1, 6 * D_MODEL)
    mod_ctx = mod[b_lat:b_lat + 1].reshape(1, 1, 6 * D_MODEL)

    def run(x, mod_s, roww, s0f, s0b):
        *p1, mlf, mlb, nnf, nnb, wtf, wtb = _phase1(x, mod_s, roww, lw)
        sf, sb, fin_f, fin_b = _phase2(s0f, s0b, mlf, mlb, nnf, nnb, wtf, wtb)
        return _phase3(x, mod_s, p1, sf, sb, lw, row(final_g)), fin_f, fin_b

    y_ctx, s_f, s_b = run(x_prompt, mod_ctx, seq_ctx, zero_state, zero_state)
    y_lat, _, _ = run(x_sample, mod_lat, GRID_W, _pair_compact(state_rwkv_fwd[:, i]),
                      _pair_compact(state_rwkv_bwd[:, i]))
    return (y_ctx, y_lat, s_f[:, None], s_b[:, None])
```

```python
import functools
import math

import jax
import jax.numpy as jnp
from jax import lax
from jax.experimental import pallas as pl
from jax.experimental.pallas import tpu as pltpu

D_MODEL = 1024
GRID_W = 64
RWKV_WIDTH = D_MODEL // 2
HEAD_SIZE = 64
RWKV_HEADS = RWKV_WIDTH // HEAD_SIZE
CONV_WIDTH = D_MODEL - RWKV_WIDTH
LORA_COLS = 256
GATE_LORA = 128
RWKV_COLS = 3 * RWKV_WIDTH + LORA_COLS + GATE_LORA
IN_COLS = RWKV_COLS + 3 * CONV_WIDTH
D_FF = 4 * D_MODEL
NORM_EPS = 1e-6
GN_EPS = HEAD_SIZE * 1e-5

LANES = 128
SUBLANES = 8
PAIRS = RWKV_WIDTH // LANES
CHUNK = 64
TILE = 256
CHUNKS_PER_TILE = TILE // CHUNK
INV_BASE = 8
DECAY_SCALE = math.exp(-0.5)
P3_TILE = 512
P3_ROWS = 256
FF_CHUNK = 1024
SCAN_CHUNKS = 4
SCAN_BATCH = 8
VMEM_LIMIT = 56 * 1024 * 1024

assert 2 * HEAD_SIZE == LANES and CHUNK == HEAD_SIZE

F32 = jnp.float32
BF16 = jnp.bfloat16
HI = lax.Precision.HIGHEST


def _dot(a, b):
    return jnp.dot(a, b, preferred_element_type=F32)


def _dot_hi(a, b):
    return jnp.dot(a, b, precision=HI, preferred_element_type=F32)


def _mm(a, b):
    return jnp.dot(a.astype(BF16), b.astype(BF16), preferred_element_type=F32)


def _mm_nt(a, b):
    return lax.dot_general(a.astype(BF16), b.astype(BF16), (((1,), (1,)), ((), ())),
                           preferred_element_type=F32)


def _mm_tn(a, b):
    return lax.dot_general(a.astype(BF16), b.astype(BF16), (((0,), (0,)), ((), ())),
                           preferred_element_type=F32)


def _split2(x):
    hi = x.astype(BF16)
    lo = (x - hi.astype(F32)).astype(BF16)
    return hi, lo


def _head_sum(x, bones):
    return _dot(x.astype(BF16), bones)


def _sigmoid(x):
    return 1.0 / (1.0 + jnp.exp(-x))


def _block_diag_rhs(xc):
    first = lax.broadcasted_iota(jnp.int32, xc.shape, 1) < HEAD_SIZE
    zero = jnp.zeros_like(xc)
    return jnp.concatenate([jnp.where(first, xc, zero), jnp.where(first, zero, xc)], axis=0)


def _compact(xbd):
    first = lax.broadcasted_iota(jnp.int32, (HEAD_SIZE, LANES), 1) < HEAD_SIZE
    return jnp.where(first, xbd[0:HEAD_SIZE], xbd[HEAD_SIZE:2 * HEAD_SIZE])


def _const_spec(shape):
    nd = len(shape)
    return pl.BlockSpec(shape, lambda *_: (0,) * nd, pipeline_mode=pl.Buffered(1))


def _mod_body(c_ref, w_ref, b_ref, o_ref):
    cv = c_ref[...]
    s = cv * _sigmoid(cv)
    o_ref[...] = _dot_hi(s, w_ref[...]) + b_ref[...]


def _modulation(cvec, w_mod, b_mod):
    rows = cvec.shape[0]
    n = w_mod.shape[1]
    bn = 1536
    return pl.pallas_call(
        _mod_body,
        grid=(n // bn,),
        in_specs=[pl.BlockSpec((rows, D_MODEL), lambda j: (0, 0)),
                  pl.BlockSpec((D_MODEL, bn), lambda j: (0, j)),
                  pl.BlockSpec((1, bn), lambda j: (0, j))],
        out_specs=pl.BlockSpec((rows, bn), lambda j: (0, j)),
        out_shape=jax.ShapeDtypeStruct((rows, n), F32),
        compiler_params=pltpu.CompilerParams(dimension_semantics=("arbitrary",),
                                             vmem_limit_bytes=VMEM_LIMIT),
        name="modulation",
    )(cvec, w_mod, b_mod)


def _phase1_body(roww, x_ref, mod_ref, g1_ref, win_ref, mu_ref, wl_ref, bl_ref, wg2_ref, kk_ref,
                 ka_ref, rk_ref, cw_ref, cg_ref, bones_ref, trif_ref, trib_ref,
                 y0_ref, rqf_ref, rqb_ref, bonus_ref, gg_ref, oconv_ref, mlf_ref, mlb_ref,
                 nnf_ref, nnb_ref, wtf_ref, wtb_ref,
                 z_scr, opsf_scr, opsb_scr, v_scr, rowf_scr, rowb_scr):
    W = RWKV_WIDTH
    x = x_ref[0]
    mod = mod_ref[0]
    shift_a = mod[:, 0:D_MODEL]
    scale_a = mod[:, D_MODEL:2 * D_MODEL]
    ms = jnp.mean(x * x, axis=-1, keepdims=True)
    h = (x * lax.rsqrt(ms + NORM_EPS)) * g1_ref[...] * (1.0 + scale_a) + shift_a
    h_bf = h.astype(BF16)
    z_scr[:, 3 * W:RWKV_COLS] = _dot(h_bf, win_ref[:, 3 * W:RWKV_COLS])
    z_scr[:, 0:3 * W] = _dot(h_bf, win_ref[:, 0:3 * W])

    t_idx = lax.broadcasted_iota(jnp.int32, (TILE, 1), 0)
    pos = lax.rem(t_idx, roww)
    is_first = pos == 0
    is_last = pos == roww - 1

    def neighbours(zz):
        prev = jnp.where(is_first, 0.0, pltpu.roll(zz, 1, 0))
        nxt = jnp.where(is_last, 0.0, pltpu.roll(zz, TILE - 1, 0))
        return prev, nxt

    def shifted(lo, hi):
        zz = z_scr[:, lo:hi]
        prev, nxt = neighbours(zz)
        mu = mu_ref[:, lo:hi]
        return (1.0 - mu) * zz + (0.5 * mu) * (prev + nxt)

    lora = shifted(3 * W, 3 * W + LORA_COLS)
    lora_in = jnp.concatenate([jnp.tanh(lora[:, :LANES]), lora[:, LANES:]], axis=1)
    pre = _mm(lora_in, wl_ref[...]) + bl_ref[...]
    z_scr[:, RWKV_COLS:IN_COLS] = _dot(h_bf, win_ref[:, RWKV_COLS:IN_COLS])

    r = shifted(0, W)
    k = shifted(W, 2 * W)
    v = shifted(2 * W, 3 * W)
    v_scr[...] = v

    bones = bones_ref[...]
    kraw = k * kk_ref[...]
    kk = kraw * lax.rsqrt(_head_sum(kraw * kraw, bones) + 1e-12)

    log_decay, cum = [], []
    for d in range(2):
        lw = -DECAY_SCALE * _sigmoid(pre[:, d * W:(d + 1) * W])
        tri = trif_ref[...] if d == 0 else trib_ref[...]
        l_hi, l_lo = _split2(lw)
        log_decay.append(lw)
        cum.append(_dot(tri, l_hi) + _dot(tri, l_lo))

    def prepare(d):
        ops_scr, row_scr = (opsf_scr, rowf_scr) if d == 0 else (opsb_scr, rowb_scr)
        lw = log_decay[d]
        a = _sigmoid(pre[:, (2 + d) * W:(3 + d) * W])
        kd = k * (1.0 + (a - 1.0) * ka_ref[...])
        b = kk * a
        c3 = cum[d].reshape(CHUNKS_PER_TILE, CHUNK, W)
        mid = c3[:, CHUNK // 2:CHUNK // 2 + 1, :]
        end_row = CHUNK - 1 if d == 0 else 0
        end = c3[:, end_row:end_row + 1, :]
        cm = (c3 - mid).reshape(TILE, W)
        e_r = jnp.exp(cm)
        e_kap = jnp.exp(cm - lw)
        e_inv = jnp.exp(-cm)
        ops_scr[0] = kk * e_kap
        ops_scr[1] = r * e_r
        ops_scr[2] = kd * e_inv
        ops_scr[3] = b * e_inv
        for j, rv in enumerate((jnp.exp(end - mid), jnp.exp(end), jnp.exp(mid))):
            row_scr[j] = jnp.broadcast_to(rv, (CHUNKS_PER_TILE, SUBLANES, W)).reshape(
                CHUNKS_PER_TILE * SUBLANES, W)
        return kd

    row = lax.broadcasted_iota(jnp.int32, (CHUNK, LANES), 0)
    col = lax.bitwise_and(lax.broadcasted_iota(jnp.int32, (CHUNK, LANES), 1), HEAD_SIZE - 1)
    eye = row == col
    same_block = {}
    size = INV_BASE
    while size <= CHUNK:
        same_block[size] = (row // size) == (col // size)
        size *= 2
    masks = (jnp.concatenate([row > col, row >= col], axis=0),
             jnp.concatenate([row < col, row <= col], axis=0))
    ml_refs = (mlf_ref, mlb_ref)
    nn_refs = (nnf_ref, nnb_ref)
    rq_refs = (rqf_ref, rqb_ref)

    def chains(d):
        ops_scr, row_scr = (opsf_scr, rowf_scr) if d == 0 else (opsb_scr, rowb_scr)
        rows = [pl.ds(ci * CHUNK, CHUNK) for ci in range(CHUNKS_PER_TILE)]
        crow = [pl.ds(ci * SUBLANES, 1) for ci in range(CHUNKS_PER_TILE)]
        lsl = [slice(p * LANES, (p + 1) * LANES) for p in range(PAIRS)]
        ch = [(c, p) for c in range(CHUNKS_PER_TILE) for p in range(PAIRS)]
        n = len(ch)

        v_pair = [v_scr[rows[c], lsl[p]] for (c, p) in ch]
        vbd = [_block_diag_rhs(vp.astype(BF16)) for vp in v_pair]
        kap = [ops_scr[0, rows[c], lsl[p]] for (c, p) in ch]
        rt = [ops_scr[1, rows[c], lsl[p]] for (c, p) in ch]
        kt = [ops_scr[2, rows[c], lsl[p]] for (c, p) in ch]
        bt = [ops_scr[3, rows[c], lsl[p]] for (c, p) in ch]
        to_end = [row_scr[0, crow[c], lsl[p]] for (c, p) in ch]
        e_mid = [row_scr[2, crow[c], lsl[p]] for (c, p) in ch]
        lhs = [jnp.concatenate([kap[i], rt[i]], axis=0).astype(BF16) for i in range(n)]
        a_all = [_mm_nt(lhs[i], jnp.concatenate([_block_diag_rhs(kt[i].astype(BF16)),
                                                 _block_diag_rhs(bt[i].astype(BF16))], axis=0))
                 for i in range(n)]
        a_k = [jnp.where(masks[d], a_all[i][:, 0:LANES], 0.0) for i in range(n)]
        a_b = [jnp.where(masks[d], a_all[i][:, LANES:2 * LANES], 0.0) for i in range(n)]
        a_kb = [x[0:CHUNK] for x in a_b]
        a_rb = [x[CHUNK:2 * CHUNK] for x in a_b]
        av = [_mm(a_k[i], vbd[i]) for i in range(n)]
        l_diag = [jnp.where(same_block[INV_BASE], x, 0.0) for x in a_kb]
        t_inv = [jnp.where(eye, 1.0, 0.0) - x for x in l_diag]
        pw = [_mm(x, _block_diag_rhs(x.astype(BF16))) for x in l_diag]
        n_fac = 2
        while 2 * n_fac < INV_BASE:
            both = [_mm(jnp.concatenate([t, w], axis=0), _block_diag_rhs(w.astype(BF16)))
                    for t, w in zip(t_inv, pw)]
            t_inv = [t + x[0:CHUNK] for t, x in zip(t_inv, both)]
            pw = [x[CHUNK:2 * CHUNK] for x in both]
            n_fac *= 2
        t_inv = [t + _mm(t, _block_diag_rhs(w.astype(BF16))) for t, w in zip(t_inv, pw)]
        sizes = []
        size = INV_BASE
        while size < CHUNK:
            sizes.append(size)
            size *= 2
        half = CHUNK // 2

        def hit(s):
            return [j for j in range(CHUNK // s) if j % 2 == (1 if d == 0 else 0)]

        def gather(x, s):
            return jnp.concatenate([x[j * s:(j + 1) * s] for j in hit(s)], axis=0)

        def scatter(base, upd, s):
            rows_of = {j: i for i, j in enumerate(hit(s))}
            parts = [upd[rows_of[j] * s:(rows_of[j] + 1) * s] if j in rows_of
                     else (jnp.zeros((s, LANES), F32) if base is None else base[j * s:(j + 1) * s])
                     for j in range(CHUNK // s)]
            return jnp.concatenate(parts, axis=0)

        off = [[gather(jnp.where(same_block[2 * s] & ~same_block[s], x, 0.0), s) for s in sizes]
               for x in a_kb]
        y_all = [_mm(jnp.concatenate(o, axis=0), _block_diag_rhs(t.astype(BF16)))
                 for o, t in zip(off, t_inv)]
        y_lvl = [[y[j * half:(j + 1) * half] for j in range(len(sizes))] for y in y_all]
        for li, s in enumerate(sizes):
            lhs_m = [jnp.concatenate([gather(t, s)] + y[li + 1:], axis=0) for t, y in zip(t_inv, y_lvl)]
            prod = [_mm(l, _block_diag_rhs(scatter(None, y[li], s).astype(BF16)))
                    for l, y in zip(lhs_m, y_lvl)]
            t_inv = [scatter(t, gather(t, s) - x[0:half], s) for t, x in zip(t_inv, prod)]
            y_lvl = [y[:li + 1] + [y[j] - x[(j - li) * half:(j - li + 1) * half]
                                   for j in range(li + 1, len(sizes))]
                     for y, x in zip(y_lvl, prod)]
        rhs = [jnp.concatenate([_block_diag_rhs((kap[i] * e_mid[i]).astype(BF16)),
                                _block_diag_rhs(av[i][0:CHUNK].astype(BF16))], axis=1) for i in range(n)]
        pq = [_mm(t_inv[i], rhs[i]) for i in range(n)]
        rhs2 = [jnp.concatenate([_block_diag_rhs(x[:, 0:LANES].astype(BF16)),
                                 _block_diag_rhs(x[:, LANES:2 * LANES].astype(BF16))], axis=1) for x in pq]
        rb = [_mm(a_rb[i], rhs2[i]) for i in range(n)]
        for i, (c, p) in enumerate(ch):
            rq_refs[d][0, rows[c], lsl[p]] = (rt[i] * e_mid[i] - rb[i][:, 0:LANES]).astype(BF16)
        y0d = [av[i][CHUNK:2 * CHUNK] - rb[i][:, LANES:2 * LANES] for i in range(n)]
        b_end = [(bt[i] * to_end[i]).astype(BF16) for i in range(n)]
        k_end = [(kt[i] * to_end[i]).astype(BF16) for i in range(n)]
        pq_b = [_mm(pq[i].T, b_end[i]) for i in range(n)]
        vk = [_mm(v_pair[i].T, k_end[i]) for i in range(n)]
        for i, (c, p) in enumerate(ch):
            ml_refs[d][0, c, p] = _compact(pq_b[i][0:LANES]).astype(BF16)
            nn_refs[d][0, c, p] = (_compact(vk[i]) - _compact(pq_b[i][LANES:2 * LANES])).astype(BF16)
        return y0d

    kd_f = prepare(0)
    y0_f = chains(0)
    kd_b = prepare(1)

    gl = shifted(3 * W + LORA_COLS, RWKV_COLS)
    gg_ref[0] = _mm(_sigmoid(gl), wg2_ref[...])
    bonus_ref[0] = _head_sum(r * (kd_f + kd_b) * rk_ref[...], bones) * v
    gate_b = z_scr[:, RWKV_COLS:RWKV_COLS + CONV_WIDTH]
    u = z_scr[:, RWKV_COLS + CONV_WIDTH:RWKV_COLS + 2 * CONV_WIDTH] * z_scr[:, RWKV_COLS + 2 * CONV_WIDTH:IN_COLS]
    up, un = neighbours(u)
    oc = gate_b * (cw_ref[0:1, :] * up + cw_ref[1:2, :] * u + cw_ref[2:3, :] * un)
    oc = oc * lax.rsqrt(jnp.mean(oc * oc, axis=-1, keepdims=True) + NORM_EPS) * cg_ref[...]
    oconv_ref[0] = oc.astype(BF16)

    y0_b = chains(1)
    for i in range(len(y0_f)):
        ci, p = divmod(i, PAIRS)
        y0_ref[0, ci * CHUNK:(ci + 1) * CHUNK, p * LANES:(p + 1) * LANES] = y0_f[i] + y0_b[i]
    wtf_ref[0] = rowf_scr[1]
    wtb_ref[0] = rowb_scr[1]


def _phase1(x, mod, roww, lw):
    bsz, seq, _ = x.shape
    nt = seq // TILE
    nc = seq // CHUNK
    mod_map = (lambda b, t: (b, 0, 0)) if mod.shape[0] == bsz else (lambda b, t: (0, 0, 0))
    tok = lambda n: pl.BlockSpec((1, TILE, n), lambda b, t: (b, t, 0))
    mn_spec = pl.BlockSpec((1, CHUNKS_PER_TILE, PAIRS, HEAD_SIZE, LANES), lambda b, t: (b, t, 0, 0, 0))
    W = RWKV_WIDTH
    consts = (lw["norm1_g"], lw["w_in"], lw["mu"], lw["w_lora"], lw["b_lora"], lw["w_g2"], lw["k_k"],
              lw["k_a"], lw["r_k"], lw["conv_w"], lw["conv_gain"], lw["bones"], lw["tri_f"], lw["tri_b"])
    tok_shape = jax.ShapeDtypeStruct((bsz, seq, W), F32)
    tok_bf16 = jax.ShapeDtypeStruct((bsz, seq, W), BF16)
    ml_shape = jax.ShapeDtypeStruct((bsz, nc, PAIRS, HEAD_SIZE, LANES), BF16)
    nn_shape = jax.ShapeDtypeStruct((bsz, nc, PAIRS, HEAD_SIZE, LANES), BF16)
    wt_spec = pl.BlockSpec((1, CHUNKS_PER_TILE * SUBLANES, W), lambda b, t: (b, t, 0))
    wt_shape = jax.ShapeDtypeStruct((bsz, nc * SUBLANES, W), F32)
    return pl.pallas_call(
        functools.partial(_phase1_body, roww),
        grid=(bsz, nt),
        in_specs=[tok(D_MODEL), pl.BlockSpec((1, 1, 6 * D_MODEL), mod_map)]
                 + [_const_spec(a.shape) for a in consts],
        out_specs=[tok(W)] * 6 + [mn_spec] * 4 + [wt_spec, wt_spec],
        out_shape=[tok_shape, tok_bf16, tok_bf16, tok_shape, tok_shape, tok_bf16,
                   ml_shape, ml_shape, nn_shape, nn_shape, wt_shape, wt_shape],
        scratch_shapes=[pltpu.VMEM((TILE, IN_COLS), F32),
                        pltpu.VMEM((4, TILE, W), F32),
                        pltpu.VMEM((4, TILE, W), F32),
                        pltpu.VMEM((TILE, W), F32),
                        pltpu.VMEM((3, CHUNKS_PER_TILE * SUBLANES, W), F32),
                        pltpu.VMEM((3, CHUNKS_PER_TILE * SUBLANES, W), F32)],
        compiler_params=pltpu.CompilerParams(dimension_semantics=("parallel", "parallel"),
                                             vmem_limit_bytes=VMEM_LIMIT),
        name="phase1_chunk_summaries",
    )(x, mod, *consts)


def _phase2_body(nb, s0f_ref, s0b_ref, mlf_ref, mlb_ref, nnf_ref, nnb_ref, wtf_ref, wtb_ref,
                 sf_ref, sb_ref, finf_ref, finb_ref, st_scr):
    step = pl.program_id(1)

    @pl.when(step == 0)
    def _():
        st_scr[0] = s0f_ref[...]
        st_scr[1] = s0b_ref[...]

    chains = [(b, d, p) for b in range(nb) for d in range(2) for p in range(PAIRS)]
    ml_refs = (mlf_ref, mlb_ref)
    nn_refs = (nnf_ref, nnb_ref)
    wt_refs = (wtf_ref, wtb_ref)
    s_refs = (sf_ref, sb_ref)
    s = [st_scr[d, b, p] for (b, d, p) in chains]
    for j in range(SCAN_CHUNKS):
        cj = (j, SCAN_CHUNKS - 1 - j)
        for i, (b, d, p) in enumerate(chains):
            s_refs[d][b, cj[d], p] = _block_diag_rhs(s[i]).T.astype(BF16)
        low = [_mm(s[i], _block_diag_rhs(ml_refs[d][b, cj[d], p])) for i, (b, d, p) in enumerate(chains)]
        for i, (b, d, p) in enumerate(chains):
            w_row = wt_refs[d][b, cj[d] * SUBLANES:cj[d] * SUBLANES + 1, p * LANES:(p + 1) * LANES]
            s[i] = s[i] * w_row - low[i] + nn_refs[d][b, cj[d], p]
    for i, (b, d, p) in enumerate(chains):
        st_scr[d, b, p] = s[i]

    @pl.when(step == pl.num_programs(1) - 1)
    def _():
        for i, (b, d, p) in enumerate(chains):
            fin_ref = finf_ref if d == 0 else finb_ref
            fin_ref[b, 2 * p] = s[i][:, 0:HEAD_SIZE]
            fin_ref[b, 2 * p + 1] = s[i][:, HEAD_SIZE:2 * HEAD_SIZE]


def _phase2(s0f, s0b, mlf, mlb, nnf, nnb, wtf, wtb):
    bsz, nc = mlf.shape[0], mlf.shape[1]
    bb = min(bsz, SCAN_BATCH)
    assert bsz % bb == 0 and nc % SCAN_CHUNKS == 0
    ns = nc // SCAN_CHUNKS
    st_blk = (bb, PAIRS, HEAD_SIZE, LANES)
    mn_blk = (bb, SCAN_CHUNKS, PAIRS, HEAD_SIZE, LANES)
    s_blk = (bb, SCAN_CHUNKS, PAIRS, LANES, LANES)
    wt_blk = (bb, SCAN_CHUNKS * SUBLANES, RWKV_WIDTH)
    fwd = lambda g, i: (g, i, 0, 0, 0)
    bwd = lambda g, i: (g, ns - 1 - i, 0, 0, 0)
    full = pl.BlockSpec(st_blk, lambda g, i: (g, 0, 0, 0))
    s_all = jax.ShapeDtypeStruct((bsz, nc, PAIRS, LANES, LANES), BF16)
    fin_spec = pl.BlockSpec((bb, RWKV_HEADS, HEAD_SIZE, HEAD_SIZE), lambda g, i: (g, 0, 0, 0))
    s_fin = jax.ShapeDtypeStruct((bsz, RWKV_HEADS, HEAD_SIZE, HEAD_SIZE), F32)
    return pl.pallas_call(
        functools.partial(_phase2_body, bb),
        grid=(bsz // bb, ns),
        in_specs=[full, full, pl.BlockSpec(mn_blk, fwd), pl.BlockSpec(mn_blk, bwd),
                  pl.BlockSpec(mn_blk, fwd), pl.BlockSpec(mn_blk, bwd),
                  pl.BlockSpec(wt_blk, lambda g, i: (g, i, 0)),
                  pl.BlockSpec(wt_blk, lambda g, i: (g, ns - 1 - i, 0))],
        out_specs=[pl.BlockSpec(s_blk, fwd), pl.BlockSpec(s_blk, bwd), fin_spec, fin_spec],
        out_shape=[s_all, s_all, s_fin, s_fin],
        scratch_shapes=[pltpu.VMEM((2,) + st_blk, F32)],
        compiler_params=pltpu.CompilerParams(dimension_semantics=("parallel", "arbitrary"),
                                             vmem_limit_bytes=VMEM_LIMIT),
        name="phase2_state_scan",
    )(s0f, s0b, mlf, mlb, nnf, nnb, wtf, wtb)


def _phase3_body(x_ref, mod_ref, y0_ref, rqf_ref, rqb_ref, bonus_ref, gg_ref, oconv_ref, sf_ref, sb_ref,
                 lnw_ref, lnb_ref, bones_ref, wout_ref, g2_ref, wff1_ref, wff2_ref, gfin_ref,
                 o_ref, y_scr):
    W = RWKV_WIDTH
    mod = mod_ref[0]
    gate_a = mod[:, 2 * D_MODEL:3 * D_MODEL]
    shift_f = mod[:, 3 * D_MODEL:4 * D_MODEL]
    scale_f = mod[:, 4 * D_MODEL:5 * D_MODEL]
    gate_f = mod[:, 5 * D_MODEL:6 * D_MODEL]
    inv_n = 1.0 / HEAD_SIZE
    carried = {}

    def mix(blk):
        r0 = blk * P3_ROWS
        rsl = slice(r0, r0 + P3_ROWS)
        for ci in range(P3_ROWS // CHUNK):
            cg = blk * (P3_ROWS // CHUNK) + ci
            rows = slice(cg * CHUNK, (cg + 1) * CHUNK)
            for p in range(PAIRS):
                ls = slice(p * LANES, (p + 1) * LANES)
                y_scr[rows, ls] = (y0_ref[rows, ls] + _dot(rqf_ref[rows, ls], sf_ref[cg, p])
                                   + _dot(rqb_ref[rows, ls], sb_ref[cg, p]))
        yield
        y = y_scr[rsl, :]
        yc = y - _head_sum(y, bones_ref[...]) * inv_n
        yield
        y_var = _head_sum(yc * yc, bones_ref[...]) * inv_n
        yn = yc * lax.rsqrt(y_var + GN_EPS) * lnw_ref[...] + lnb_ref[...]
        o_rwkv = (yn + bonus_ref[rsl, :]) * gg_ref[rsl, :]
        yield
        mixed = (_dot(o_rwkv.astype(BF16), wout_ref[0:W, :])
                 + _dot(oconv_ref[rsl, :], wout_ref[W:D_MODEL, :]))
        x1 = x_ref[rsl, :] + gate_a * mixed
        ms = jnp.mean(x1 * x1, axis=-1, keepdims=True)
        h2 = (x1 * lax.rsqrt(ms + NORM_EPS)) * g2_ref[...] * (1.0 + scale_f) + shift_f
        carried[blk] = (x1, h2.astype(BF16))

    def mlp(blk):
        rsl = slice(blk * P3_ROWS, (blk + 1) * P3_ROWS)
        x1, h2 = carried.pop(blk)
        acc = None
        for j in range(D_FF // FF_CHUNK):
            cols = slice(j * FF_CHUNK, (j + 1) * FF_CHUNK)
            f1 = jnp.maximum(_dot(h2, wff1_ref[:, cols]), 0.0)
            f2 = _dot((f1 * f1).astype(BF16), wff2_ref[cols, :])
            acc = f2 if acc is None else acc + f2
            yield
        x2 = x1 + gate_f * acc
        ms2 = jnp.mean(x2 * x2, axis=-1, keepdims=True)
        o_ref[rsl, :] = (x2 * lax.rsqrt(ms2 + NORM_EPS)) * gfin_ref[...]

    def interleave(*gens):
        live = list(gens)
        while live:
            for g in list(live):
                try:
                    next(g)
                except StopIteration:
                    live.remove(g)

    n_blk = P3_TILE // P3_ROWS
    interleave(mix(0))
    for blk in range(1, n_blk):
        interleave(mlp(blk - 1), mix(blk))
    interleave(mlp(n_blk - 1))


def _phase3(x, mod, p1, sf, sb, lw, final_g):
    bsz, seq, _ = x.shape
    W = RWKV_WIDTH
    n_tok = bsz * seq
    nt = n_tok // P3_TILE
    assert n_tok % P3_TILE == 0 and (seq % P3_TILE == 0 or mod.shape[0] == 1)
    tiles_per_batch = max(seq // P3_TILE, 1)
    mod_map = (lambda t: (t // tiles_per_batch, 0, 0)) if mod.shape[0] == bsz else (lambda t: (0, 0, 0))
    tok = lambda n: pl.BlockSpec((P3_TILE, n), lambda t: (t, 0))
    cpt = P3_TILE // CHUNK
    s_spec = pl.BlockSpec((cpt, PAIRS, LANES, LANES), lambda t: (t, 0, 0, 0))
    flat = lambda a: a.reshape((n_tok,) + a.shape[2:])
    flat_s = lambda a: a.reshape((a.shape[0] * a.shape[1],) + a.shape[2:])
    consts = (lw["ln_x_w"], lw["ln_x_b"], lw["bones"], lw["w_out"], lw["norm2_g"], lw["w_ff1"], lw["w_ff2"],
              final_g)
    out = pl.pallas_call(
        _phase3_body,
        grid=(nt,),
        in_specs=[tok(D_MODEL), pl.BlockSpec((1, 1, 6 * D_MODEL), mod_map)] + [tok(W)] * 6 + [s_spec, s_spec]
                 + [_const_spec(a.shape) for a in consts],
        out_specs=tok(D_MODEL),
        out_shape=jax.ShapeDtypeStruct((n_tok, D_MODEL), F32),
        scratch_shapes=[pltpu.VMEM((P3_TILE, W), F32)],
        compiler_params=pltpu.CompilerParams(dimension_semantics=("parallel",),
                                             vmem_limit_bytes=VMEM_LIMIT),
        name="phase3_mix_mlp",
    )(flat(x), mod, *[flat(a) for a in p1], flat_s(sf), flat_s(sb), *consts)
    return out.reshape(bsz, seq, D_MODEL)


def _block_diag(blocks):
    rows = sum(b.shape[0] for b in blocks)
    cols = sum(b.shape[1] for b in blocks)
    out = jnp.zeros((rows, cols), blocks[0].dtype)
    r = c = 0
    for b in blocks:
        out = lax.dynamic_update_slice(out, b, (r, c))
        r += b.shape[0]
        c += b.shape[1]
    return out


def _tri_constants():
    t = jnp.arange(TILE)
    same = (t[:, None] // CHUNK) == (t[None, :] // CHUNK)
    tri_f = (same & (t[None, :] <= t[:, None])).astype(BF16)
    tri_b = (same & (t[None, :] >= t[:, None])).astype(BF16)
    lane = jnp.arange(RWKV_WIDTH)
    bones = ((lane[:, None] // HEAD_SIZE) == (lane[None, :] // HEAD_SIZE)).astype(BF16)
    return tri_f, tri_b, bones


def _pair_compact(s):
    b = s.shape[0]
    s = s.reshape(b, PAIRS, 2, HEAD_SIZE, HEAD_SIZE)
    return jnp.swapaxes(s, 2, 3).reshape(b, PAIRS, HEAD_SIZE, LANES)


def kernel(x_prompt, x_sample, c, state_rwkv_fwd, state_rwkv_bwd, c_ctx, w_mod, b_mod, norm1_g, w_in, mu_shift, w0_f, w_w2_f, w0_b, w_w2_b, a0_f, w_a2_f, a0_b, w_a2_b, w_g2, k_k, k_a, r_k, ln_x_w, ln_x_b, conv_w, conv_gain, w_out, norm2_g, w_ff1, w_ff2, final_g):
    depth = w_in.shape[0]
    b_ctx, seq_ctx, _ = x_prompt.shape
    b_lat, seq_lat, _ = x_sample.shape
    assert depth == 1
    assert seq_ctx == TILE and seq_lat % TILE == 0 and TILE % GRID_W == 0
    tri_f, tri_b, bones = _tri_constants()
    row = lambda a: a.reshape(1, -1)
    n_pad = (-(b_lat + 1)) % SUBLANES
    cvec = jnp.concatenate([c, c_ctx[None, :], jnp.zeros((n_pad, D_MODEL), F32)], axis=0)
    zero_state = jnp.zeros((b_ctx, PAIRS, HEAD_SIZE, LANES), F32)

    i = 0
    lw = dict(
        norm1_g=row(norm1_g[i]), w_in=w_in[i].astype(BF16), mu=row(mu_shift[i]),
        w_lora=_block_diag([w_w2_f[i], w_w2_b[i], w_a2_f[i], w_a2_b[i]]),
        b_lora=jnp.concatenate([w0_f[i], w0_b[i], a0_f[i], a0_b[i]]).reshape(1, -1),
        w_g2=w_g2[i], k_k=row(k_k[i]), k_a=row(k_a[i]), r_k=row(r_k[i]),
        conv_w=conv_w[i], conv_gain=row(conv_gain[i]), bones=bones, tri_f=tri_f, tri_b=tri_b,
        ln_x_w=row(ln_x_w[i]), ln_x_b=row(ln_x_b[i]), w_out=w_out[i].astype(BF16),
        norm2_g=row(norm2_g[i]), w_ff1=w_ff1[i].astype(BF16), w_ff2=w_ff2[i].astype(BF16))
    mod = _modulation(cvec, w_mod[i], row(b_mod[i]))
    mod_lat = mod[:b_lat].reshape(b_lat, 1, 6 * D_MODEL)
    mod_ctx = mod[b_lat:b_lat + 1].reshape(1, 1, 6 * D_MODEL)

    def run(x, mod_s, roww, s0f, s0b):
        *p1, mlf, mlb, nnf, nnb, wtf, wtb = _phase1(x, mod_s, roww, lw)
        sf, sb, fin_f, fin_b = _phase2(s0f, s0b, mlf, mlb, nnf, nnb, wtf, wtb)
        return _phase3(x, mod_s, p1, sf, sb, lw, row(final_g)), fin_f, fin_b

    y_ctx, s_f, s_b = run(x_prompt, mod_ctx, seq_ctx, zero_state, zero_state)
    y_lat, _, _ = run(x_sample, mod_lat, GRID_W, _pair_compact(state_rwkv_fwd[:, i]),
                      _pair_compact(state_rwkv_bwd[:, i]))
    return (y_ctx, y_lat, s_f[:, None], s_b[:, None])
```

```python
import functools
import math

import jax
import jax.numpy as jnp
from jax import lax
from jax.experimental import pallas as pl
from jax.experimental.pallas import tpu as pltpu

D_MODEL = 1024
GRID_W = 64
RWKV_WIDTH = D_MODEL // 2
HEAD_SIZE = 64
RWKV_HEADS = RWKV_WIDTH // HEAD_SIZE
CONV_WIDTH = D_MODEL - RWKV_WIDTH
LORA_COLS = 256
GATE_LORA = 128
RWKV_COLS = 3 * RWKV_WIDTH + LORA_COLS + GATE_LORA
IN_COLS = RWKV_COLS + 3 * CONV_WIDTH
D_FF = 4 * D_MODEL
NORM_EPS = 1e-6
GN_EPS = HEAD_SIZE * 1e-5

LANES = 128
SUBLANES = 8
PAIRS = RWKV_WIDTH // LANES
CHUNK = 64
TILE = 256
CHUNKS_PER_TILE = TILE // CHUNK
INV_BASE = 8
DECAY_SCALE = math.exp(-0.5)
P3_TILE = 512
P3_ROWS = 256
FF_CHUNK = 1024
SCAN_CHUNKS = 4
SCAN_BATCH = 8
VMEM_LIMIT = 56 * 1024 * 1024

assert 2 * HEAD_SIZE == LANES and CHUNK == HEAD_SIZE

F32 = jnp.float32
BF16 = jnp.bfloat16
HI = lax.Precision.HIGHEST


def _dot(a, b):
    return jnp.dot(a, b, preferred_element_type=F32)


def _dot_hi(a, b):
    return jnp.dot(a, b, precision=HI, preferred_element_type=F32)


def _mm(a, b):
    return jnp.dot(a.astype(BF16), b.astype(BF16), preferred_element_type=F32)


def _mm_nt(a, b):
    return lax.dot_general(a.astype(BF16), b.astype(BF16), (((1,), (1,)), ((), ())),
                           preferred_element_type=F32)


def _mm_tn(a, b):
    return lax.dot_general(a.astype(BF16), b.astype(BF16), (((0,), (0,)), ((), ())),
                           preferred_element_type=F32)


def _split2(x):
    hi = x.astype(BF16)
    lo = (x - hi.astype(F32)).astype(BF16)
    return hi, lo


def _head_sum(x, bones):
    return _dot(x.astype(BF16), bones)


def _sigmoid(x):
    return 1.0 / (1.0 + jnp.exp(-x))


def _block_diag_rhs(xc):
    first = lax.broadcasted_iota(jnp.int32, xc.shape, 1) < HEAD_SIZE
    zero = jnp.zeros_like(xc)
    return jnp.concatenate([jnp.where(first, xc, zero), jnp.where(first, zero, xc)], axis=0)


def _compact(xbd):
    first = lax.broadcasted_iota(jnp.int32, (HEAD_SIZE, LANES), 1) < HEAD_SIZE
    return jnp.where(first, xbd[0:HEAD_SIZE], xbd[HEAD_SIZE:2 * HEAD_SIZE])


def _const_spec(shape):
    nd = len(shape)
    return pl.BlockSpec(shape, lambda *_: (0,) * nd, pipeline_mode=pl.Buffered(1))


def _mod_body(c_ref, w_ref, b_ref, o_ref):
    cv = c_ref[...]
    s = cv * _sigmoid(cv)
    o_ref[...] = _dot_hi(s, w_ref[...]) + b_ref[...]


def _modulation(cvec, w_mod, b_mod):
    rows = cvec.shape[0]
    n = w_mod.shape[1]
    bn = 1536
    return pl.pallas_call(
        _mod_body,
        grid=(n // bn,),
        in_specs=[pl.BlockSpec((rows, D_MODEL), lambda j: (0, 0)),
                  pl.BlockSpec((D_MODEL, bn), lambda j: (0, j)),
                  pl.BlockSpec((1, bn), lambda j: (0, j))],
        out_specs=pl.BlockSpec((rows, bn), lambda j: (0, j)),
        out_shape=jax.ShapeDtypeStruct((rows, n), F32),
        compiler_params=pltpu.CompilerParams(dimension_semantics=("arbitrary",),
                                             vmem_limit_bytes=VMEM_LIMIT),
        name="modulation",
    )(cvec, w_mod, b_mod)


def _phase1_body(roww, x_ref, mod_ref, g1_ref, win_ref, mu_ref, wl_ref, bl_ref, wg2_ref, kk_ref,
                 ka_ref, rk_ref, cw_ref, cg_ref, bones_ref, trif_ref, trib_ref,
                 y0_ref, rqf_ref, rqb_ref, bonus_ref, gg_ref, oconv_ref, mlf_ref, mlb_ref,
                 nnf_ref, nnb_ref, wtf_ref, wtb_ref,
                 z_scr, opsf_scr, opsb_scr, v_scr, rowf_scr, rowb_scr):
    W = RWKV_WIDTH
    x = x_ref[0]
    mod = mod_ref[0]
    shift_a = mod[:, 0:D_MODEL]
    scale_a = mod[:, D_MODEL:2 * D_MODEL]
    ms = jnp.mean(x * x, axis=-1, keepdims=True)
    h = (x * lax.rsqrt(ms + NORM_EPS)) * g1_ref[...] * (1.0 + scale_a) + shift_a
    h_bf = h.astype(BF16)
    z_scr[:, 3 * W:RWKV_COLS] = _dot(h_bf, win_ref[:, 3 * W:RWKV_COLS])
    z_scr[:, 0:3 * W] = _dot(h_bf, win_ref[:, 0:3 * W])

    t_idx = lax.broadcasted_iota(jnp.int32, (TILE, 1), 0)
    pos = lax.rem(t_idx, roww)
    is_first = pos == 0
    is_last = pos == roww - 1

    def neighbours(zz):
        prev = jnp.where(is_first, 0.0, pltpu.roll(zz, 1, 0))
        nxt = jnp.where(is_last, 0.0, pltpu.roll(zz, TILE - 1, 0))
        return prev, nxt

    def shifted(lo, hi):
        zz = z_scr[:, lo:hi]
        prev, nxt = neighbours(zz)
        mu = mu_ref[:, lo:hi]
        return (1.0 - mu) * zz + (0.5 * mu) * (prev + nxt)

    lora = shifted(3 * W, 3 * W + LORA_COLS)
    lora_in = jnp.concatenate([jnp.tanh(lora[:, :LANES]), lora[:, LANES:]], axis=1)
    pre = _mm(lora_in, wl_ref[...]) + bl_ref[...]
    z_scr[:, RWKV_COLS:IN_COLS] = _dot(h_bf, win_ref[:, RWKV_COLS:IN_COLS])

    r = shifted(0, W)
    k = shifted(W, 2 * W)
    v = shifted(2 * W, 3 * W)
    v_scr[...] = v

    bones = bones_ref[...]
    kraw = k * kk_ref[...]
    kk = kraw * lax.rsqrt(_head_sum(kraw * kraw, bones) + 1e-12)

    log_decay, cum = [], []
    for d in range(2):
        lw = -DECAY_SCALE * _sigmoid(pre[:, d * W:(d + 1) * W])
        tri = trif_ref[...] if d == 0 else trib_ref[...]
        l_hi, l_lo = _split2(lw)
        log_decay.append(lw)
        cum.append(_dot(tri, l_hi) + _dot(tri, l_lo))

    def prepare(d):
        ops_scr, row_scr = (opsf_scr, rowf_scr) if d == 0 else (opsb_scr, rowb_scr)
        lw = log_decay[d]
        a = _sigmoid(pre[:, (2 + d) * W:(3 + d) * W])
        kd = k * (1.0 + (a - 1.0) * ka_ref[...])
        b = kk * a
        c3 = cum[d].reshape(CHUNKS_PER_TILE, CHUNK, W)
        mid = c3[:, CHUNK // 2:CHUNK // 2 + 1, :]
        end_row = CHUNK - 1 if d == 0 else 0
        end = c3[:, end_row:end_row + 1, :]
        cm = (c3 - mid).reshape(TILE, W)
        e_r = jnp.exp(cm)
        e_kap = jnp.exp(cm - lw)
        e_inv = jnp.exp(-cm)
        ops_scr[0] = kk * e_kap
        ops_scr[1] = r * e_r
        ops_scr[2] = kd * e_inv
        ops_scr[3] = b * e_inv
        for j, rv in enumerate((jnp.exp(end - mid), jnp.exp(end), jnp.exp(mid))):
            row_scr[j] = jnp.broadcast_to(rv, (CHUNKS_PER_TILE, SUBLANES, W)).reshape(
                CHUNKS_PER_TILE * SUBLANES, W)
        return kd

    row = lax.broadcasted_iota(jnp.int32, (CHUNK, LANES), 0)
    col = lax.bitwise_and(lax.broadcasted_iota(jnp.int32, (CHUNK, LANES), 1), HEAD_SIZE - 1)
    eye = row == col
    same_block = {}
    size = INV_BASE
    while size <= CHUNK:
        same_block[size] = (row // size) == (col // size)
        size *= 2
    masks = (jnp.concatenate([row > col, row >= col], axis=0),
             jnp.concatenate([row < col, row <= col], axis=0))
    ml_refs = (mlf_ref, mlb_ref)
    nn_refs = (nnf_ref, nnb_ref)
    rq_refs = (rqf_ref, rqb_ref)

    def chains(d):
        ops_scr, row_scr = (opsf_scr, rowf_scr) if d == 0 else (opsb_scr, rowb_scr)
        rows = [pl.ds(ci * CHUNK, CHUNK) for ci in range(CHUNKS_PER_TILE)]
        crow = [pl.ds(ci * SUBLANES, 1) for ci in range(CHUNKS_PER_TILE)]
        lsl = [slice(p * LANES, (p + 1) * LANES) for p in range(PAIRS)]
        ch = [(c, p) for c in range(CHUNKS_PER_TILE) for p in range(PAIRS)]
        n = len(ch)

        v_pair = [v_scr[rows[c], lsl[p]] for (c, p) in ch]
        vbd = [_block_diag_rhs(vp.astype(BF16)) for vp in v_pair]
        kap = [ops_scr[0, rows[c], lsl[p]] for (c, p) in ch]
        rt = [ops_scr[1, rows[c], lsl[p]] for (c, p) in ch]
        kt = [ops_scr[2, rows[c], lsl[p]] for (c, p) in ch]
        bt = [ops_scr[3, rows[c], lsl[p]] for (c, p) in ch]
        to_end = [row_scr[0, crow[c], lsl[p]] for (c, p) in ch]
        e_mid = [row_scr[2, crow[c], lsl[p]] for (c, p) in ch]
        lhs = [jnp.concatenate([kap[i], rt[i]], axis=0).astype(BF16) for i in range(n)]
        a_all = [_mm_nt(lhs[i], jnp.concatenate([_block_diag_rhs(kt[i].astype(BF16)),
                                                 _block_diag_rhs(bt[i].astype(BF16))], axis=0))
                 for i in range(n)]
        a_k = [jnp.where(masks[d], a_all[i][:, 0:LANES], 0.0) for i in range(n)]
        a_b = [jnp.where(masks[d], a_all[i][:, LANES:2 * LANES], 0.0) for i in range(n)]
        a_kb = [x[0:CHUNK] for x in a_b]
        a_rb = [x[CHUNK:2 * CHUNK] for x in a_b]
        av = [_mm(a_k[i], vbd[i]) for i in range(n)]
        l_diag = [jnp.where(same_block[INV_BASE], x, 0.0) for x in a_kb]
        t_inv = [jnp.where(eye, 1.0, 0.0) - x for x in l_diag]
        pw = [_mm(x, _block_diag_rhs(x.astype(BF16))) for x in l_diag]
        n_fac = 2
        while 2 * n_fac < INV_BASE:
            both = [_mm(jnp.concatenate([t, w], axis=0), _block_diag_rhs(w.astype(BF16)))
                    for t, w in zip(t_inv, pw)]
            t_inv = [t + x[0:CHUNK] for t, x in zip(t_inv, both)]
            pw = [x[CHUNK:2 * CHUNK] for x in both]
            n_fac *= 2
        t_inv = [t + _mm(t, _block_diag_rhs(w.astype(BF16))) for t, w in zip(t_inv, pw)]
        sizes = []
        size = INV_BASE
        while size < CHUNK:
            sizes.append(size)
            size *= 2
        half = CHUNK // 2

        def hit(s):
            return [j for j in range(CHUNK // s) if j % 2 == (1 if d == 0 else 0)]

        def gather(x, s):
            return jnp.concatenate([x[j * s:(j + 1) * s] for j in hit(s)], axis=0)

        def scatter(base, upd, s):
            rows_of = {j: i for i, j in enumerate(hit(s))}
            parts = [upd[rows_of[j] * s:(rows_of[j] + 1) * s] if j in rows_of
                     else (jnp.zeros((s, LANES), F32) if base is None else base[j * s:(j + 1) * s])
                     for j in range(CHUNK // s)]
            return jnp.concatenate(parts, axis=0)

        off = [[gather(jnp.where(same_block[2 * s] & ~same_block[s], x, 0.0), s) for s in sizes]
               for x in a_kb]
        y_all = [_mm(jnp.concatenate(o, axis=0), _block_diag_rhs(t.astype(BF16)))
                 for o, t in zip(off, t_inv)]
        y_lvl = [[y[j * half:(j + 1) * half] for j in range(len(sizes))] for y in y_all]
        for li, s in enumerate(sizes):
            lhs_m = [jnp.concatenate([gather(t, s)] + y[li + 1:], axis=0) for t, y in zip(t_inv, y_lvl)]
            prod = [_mm(l, _block_diag_rhs(scatter(None, y[li], s).astype(BF16)))
                    for l, y in zip(lhs_m, y_lvl)]
            t_inv = [scatter(t, gather(t, s) - x[0:half], s) for t, x in zip(t_inv, prod)]
            y_lvl = [y[:li + 1] + [y[j] - x[(j - li) * half:(j - li + 1) * half]
                                   for j in range(li + 1, len(sizes))]
                     for y, x in zip(y_lvl, prod)]
        rhs = [jnp.concatenate([_block_diag_rhs((kap[i] * e_mid[i]).astype(BF16)),
                                _block_diag_rhs(av[i][0:CHUNK].astype(BF16))], axis=1) for i in range(n)]
        pq = [_mm(t_inv[i], rhs[i]) for i in range(n)]
        rhs2 = [jnp.concatenate([_block_diag_rhs(x[:, 0:LANES].astype(BF16)),
                                 _block_diag_rhs(x[:, LANES:2 * LANES].astype(BF16))], axis=1) for x in pq]
        rb = [_mm(a_rb[i], rhs2[i]) for i in range(n)]
        for i, (c, p) in enumerate(ch):
            rq_refs[d][0, rows[c], lsl[p]] = (rt[i] * e_mid[i] - rb[i][:, 0:LANES]).astype(BF16)
        y0d = [av[i][CHUNK:2 * CHUNK] - rb[i][:, LANES:2 * LANES] for i in range(n)]
        b_end = [(bt[i] * to_end[i]).astype(BF16) for i in range(n)]
        k_end = [(kt[i] * to_end[i]).astype(BF16) for i in range(n)]
        pq_b = [_mm(pq[i].T, b_end[i]) for i in range(n)]
        vk = [_mm(v_pair[i].T, k_end[i]) for i in range(n)]
        for i, (c, p) in enumerate(ch):
            ml_refs[d][0, c, p] = _compact(pq_b[i][0:LANES]).astype(BF16)
            nn_refs[d][0, c, p] = (_compact(vk[i]) - _compact(pq_b[i][LANES:2 * LANES])).astype(BF16)
        return y0d

    kd_f = prepare(0)
    y0_f = chains(0)
    kd_b = prepare(1)

    gl = shifted(3 * W + LORA_COLS, RWKV_COLS)
    gg_ref[0] = _mm(_sigmoid(gl), wg2_ref[...])
    bonus_ref[0] = _head_sum(r * (kd_f + kd_b) * rk_ref[...], bones) * v
    gate_b = z_scr[:, RWKV_COLS:RWKV_COLS + CONV_WIDTH]
    u = z_scr[:, RWKV_COLS + CONV_WIDTH:RWKV_COLS + 2 * CONV_WIDTH] * z_scr[:, RWKV_COLS + 2 * CONV_WIDTH:IN_COLS]
    up, un = neighbours(u)
    oc = gate_b * (cw_ref[0:1, :] * up + cw_ref[1:2, :] * u + cw_ref[2:3, :] * un)
    oc = oc * lax.rsqrt(jnp.mean(oc * oc, axis=-1, keepdims=True) + NORM_EPS) * cg_ref[...]
    oconv_ref[0] = oc.astype(BF16)

    y0_b = chains(1)
    for i in range(len(y0_f)):
        ci, p = divmod(i, PAIRS)
        y0_ref[0, ci * CHUNK:(ci + 1) * CHUNK, p * LANES:(p + 1) * LANES] = y0_f[i] + y0_b[i]
    wtf_ref[0] = rowf_scr[1]
    wtb_ref[0] = rowb_scr[1]


def _phase1(x, mod, roww, lw):
    bsz, seq, _ = x.shape
    nt = seq // TILE
    nc = seq // CHUNK
    mod_map = (lambda b, t: (b, 0, 0)) if mod.shape[0] == bsz else (lambda b, t: (0, 0, 0))
    tok = lambda n: pl.BlockSpec((1, TILE, n), lambda b, t: (b, t, 0))
    mn_spec = pl.BlockSpec((1, CHUNKS_PER_TILE, PAIRS, HEAD_SIZE, LANES), lambda b, t: (b, t, 0, 0, 0))
    W = RWKV_WIDTH
    consts = (lw["norm1_g"], lw["w_in"], lw["mu"], lw["w_lora"], lw["b_lora"], lw["w_g2"], lw["k_k"],
              lw["k_a"], lw["r_k"], lw["conv_w"], lw["conv_gain"], lw["bones"], lw["tri_f"], lw["tri_b"])
    tok_shape = jax.ShapeDtypeStruct((bsz, seq, W), F32)
    tok_bf16 = jax.ShapeDtypeStruct((bsz, seq, W), BF16)
    ml_shape = jax.ShapeDtypeStruct((bsz, nc, PAIRS, HEAD_SIZE, LANES), BF16)
    nn_shape = jax.ShapeDtypeStruct((bsz, nc, PAIRS, HEAD_SIZE, LANES), BF16)
    wt_spec = pl.BlockSpec((1, CHUNKS_PER_TILE * SUBLANES, W), lambda b, t: (b, t, 0))
    wt_shape = jax.ShapeDtypeStruct((bsz, nc * SUBLANES, W), F32)
    return pl.pallas_call(
        functools.partial(_phase1_body, roww),
        grid=(bsz, nt),
        in_specs=[tok(D_MODEL), pl.BlockSpec((1, 1, 6 * D_MODEL), mod_map)]
                 + [_const_spec(a.shape) for a in consts],
        out_specs=[tok(W)] * 6 + [mn_spec] * 4 + [wt_spec, wt_spec],
        out_shape=[tok_shape, tok_bf16, tok_bf16, tok_shape, tok_shape, tok_bf16,
                   ml_shape, ml_shape, nn_shape, nn_shape, wt_shape, wt_shape],
        scratch_shapes=[pltpu.VMEM((TILE, IN_COLS), F32),
                        pltpu.VMEM((4, TILE, W), F32),
                        pltpu.VMEM((4, TILE, W), F32),
                        pltpu.VMEM((TILE, W), F32),
                        pltpu.VMEM((3, CHUNKS_PER_TILE * SUBLANES, W), F32),
                        pltpu.VMEM((3, CHUNKS_PER_TILE * SUBLANES, W), F32)],
        compiler_params=pltpu.CompilerParams(dimension_semantics=("parallel", "parallel"),
                                             vmem_limit_bytes=VMEM_LIMIT),
        name="phase1_chunk_summaries",
    )(x, mod, *consts)


def _phase2_body(nb, s0f_ref, s0b_ref, mlf_ref, mlb_ref, nnf_ref, nnb_ref, wtf_ref, wtb_ref,
                 sf_ref, sb_ref, finf_ref, finb_ref, st_scr):
    step = pl.program_id(1)

    @pl.when(step == 0)
    def _():
        st_scr[0] = s0f_ref[...]
        st_scr[1] = s0b_ref[...]

    chains = [(b, d, p) for b in range(nb) for d in range(2) for p in range(PAIRS)]
    ml_refs = (mlf_ref, mlb_ref)
    nn_refs = (nnf_ref, nnb_ref)
    wt_refs = (wtf_ref, wtb_ref)
    s_refs = (sf_ref, sb_ref)
    s = [st_scr[d, b, p] for (b, d, p) in chains]
    for j in range(SCAN_CHUNKS):
        cj = (j, SCAN_CHUNKS - 1 - j)
        for i, (b, d, p) in enumerate(chains):
            s_refs[d][b, cj[d], p] = _block_diag_rhs(s[i].astype(BF16)).T
        low = [_mm(s[i], _block_diag_rhs(ml_refs[d][b, cj[d], p])) for i, (b, d, p) in enumerate(chains)]
        for i, (b, d, p) in enumerate(chains):
            w_row = wt_refs[d][b, cj[d] * SUBLANES:cj[d] * SUBLANES + 1, p * LANES:(p + 1) * LANES]
            s[i] = s[i] * w_row - low[i] + nn_refs[d][b, cj[d], p]
    for i, (b, d, p) in enumerate(chains):
        st_scr[d, b, p] = s[i]

    @pl.when(step == pl.num_programs(1) - 1)
    def _():
        for i, (b, d, p) in enumerate(chains):
            fin_ref = finf_ref if d == 0 else finb_ref
            fin_ref[b, 2 * p] = s[i][:, 0:HEAD_SIZE]
            fin_ref[b, 2 * p + 1] = s[i][:, HEAD_SIZE:2 * HEAD_SIZE]


def _phase2(s0f, s0b, mlf, mlb, nnf, nnb, wtf, wtb):
    bsz, nc = mlf.shape[0], mlf.shape[1]
    bb = min(bsz, SCAN_BATCH)
    assert bsz % bb == 0 and nc % SCAN_CHUNKS == 0
    ns = nc // SCAN_CHUNKS
    st_blk = (bb, PAIRS, HEAD_SIZE, LANES)
    mn_blk = (bb, SCAN_CHUNKS, PAIRS, HEAD_SIZE, LANES)
    s_blk = (bb, SCAN_CHUNKS, PAIRS, LANES, LANES)
    wt_blk = (bb, SCAN_CHUNKS * SUBLANES, RWKV_WIDTH)
    fwd = lambda g, i: (g, i, 0, 0, 0)
    bwd = lambda g, i: (g, ns - 1 - i, 0, 0, 0)
    full = pl.BlockSpec(st_blk, lambda g, i: (g, 0, 0, 0))
    s_all = jax.ShapeDtypeStruct((bsz, nc, PAIRS, LANES, LANES), BF16)
    fin_spec = pl.BlockSpec((bb, RWKV_HEADS, HEAD_SIZE, HEAD_SIZE), lambda g, i: (g, 0, 0, 0))
    s_fin = jax.ShapeDtypeStruct((bsz, RWKV_HEADS, HEAD_SIZE, HEAD_SIZE), F32)
    return pl.pallas_call(
        functools.partial(_phase2_body, bb),
        grid=(bsz // bb, ns),
        in_specs=[full, full, pl.BlockSpec(mn_blk, fwd), pl.BlockSpec(mn_blk, bwd),
                  pl.BlockSpec(mn_blk, fwd), pl.BlockSpec(mn_blk, bwd),
                  pl.BlockSpec(wt_blk, lambda g, i: (g, i, 0)),
                  pl.BlockSpec(wt_blk, lambda g, i: (g, ns - 1 - i, 0))],
        out_specs=[pl.BlockSpec(s_blk, fwd), pl.BlockSpec(s_blk, bwd), fin_spec, fin_spec],
        out_shape=[s_all, s_all, s_fin, s_fin],
        scratch_shapes=[pltpu.VMEM((2,) + st_blk, F32)],
        compiler_params=pltpu.CompilerParams(dimension_semantics=("parallel", "arbitrary"),
                                             vmem_limit_bytes=VMEM_LIMIT),
        name="phase2_state_scan",
    )(s0f, s0b, mlf, mlb, nnf, nnb, wtf, wtb)


def _phase3_body(x_ref, mod_ref, y0_ref, rqf_ref, rqb_ref, bonus_ref, gg_ref, oconv_ref, sf_ref, sb_ref,
                 lnw_ref, lnb_ref, bones_ref, wout_ref, g2_ref, wff1_ref, wff2_ref, gfin_ref,
                 o_ref, y_scr):
    W = RWKV_WIDTH
    mod = mod_ref[0]
    gate_a = mod[:, 2 * D_MODEL:3 * D_MODEL]
    shift_f = mod[:, 3 * D_MODEL:4 * D_MODEL]
    scale_f = mod[:, 4 * D_MODEL:5 * D_MODEL]
    gate_f = mod[:, 5 * D_MODEL:6 * D_MODEL]
    inv_n = 1.0 / HEAD_SIZE
    carried = {}

    def mix(blk):
        r0 = blk * P3_ROWS
        rsl = slice(r0, r0 + P3_ROWS)
        for ci in range(P3_ROWS // CHUNK):
            cg = blk * (P3_ROWS // CHUNK) + ci
            rows = slice(cg * CHUNK, (cg + 1) * CHUNK)
            for p in range(PAIRS):
                ls = slice(p * LANES, (p + 1) * LANES)
                y_scr[rows, ls] = y0_ref[rows, ls] + _dot(
                    jnp.concatenate([rqf_ref[rows, ls], rqb_ref[rows, ls]], axis=1),
                    jnp.concatenate([sf_ref[cg, p], sb_ref[cg, p]], axis=0))
        yield
        y = y_scr[rsl, :]
        yc = y - _head_sum(y, bones_ref[...]) * inv_n
        yield
        y_var = _head_sum(yc * yc, bones_ref[...]) * inv_n
        yn = yc * lax.rsqrt(y_var + GN_EPS) * lnw_ref[...] + lnb_ref[...]
        o_rwkv = (yn + bonus_ref[rsl, :]) * gg_ref[rsl, :]
        yield
        mixed = (_dot(o_rwkv.astype(BF16), wout_ref[0:W, :])
                 + _dot(oconv_ref[rsl, :], wout_ref[W:D_MODEL, :]))
        x1 = x_ref[rsl, :] + gate_a * mixed
        ms = jnp.mean(x1 * x1, axis=-1, keepdims=True)
        h2 = (x1 * lax.rsqrt(ms + NORM_EPS)) * g2_ref[...] * (1.0 + scale_f) + shift_f
        carried[blk] = (x1, h2.astype(BF16))

    def mlp(blk):
        rsl = slice(blk * P3_ROWS, (blk + 1) * P3_ROWS)
        x1, h2 = carried.pop(blk)
        acc = None
        for j in range(D_FF // FF_CHUNK):
            cols = slice(j * FF_CHUNK, (j + 1) * FF_CHUNK)
            f1 = jnp.maximum(_dot(h2, wff1_ref[:, cols]), 0.0)
            f2 = _dot((f1 * f1).astype(BF16), wff2_ref[cols, :])
            acc = f2 if acc is None else acc + f2
            yield
        x2 = x1 + gate_f * acc
        ms2 = jnp.mean(x2 * x2, axis=-1, keepdims=True)
        o_ref[rsl, :] = (x2 * lax.rsqrt(ms2 + NORM_EPS)) * gfin_ref[...]

    def interleave(*gens):
        live = list(gens)
        while live:
            for g in list(live):
                try:
                    next(g)
                except StopIteration:
                    live.remove(g)

    n_blk = P3_TILE // P3_ROWS
    interleave(mix(0))
    for blk in range(1, n_blk):
        interleave(mlp(blk - 1), mix(blk))
    interleave(mlp(n_blk - 1))


def _phase3(x, mod, p1, sf, sb, lw, final_g):
    bsz, seq, _ = x.shape
    W = RWKV_WIDTH
    n_tok = bsz * seq
    nt = n_tok // P3_TILE
    assert n_tok % P3_TILE == 0 and (seq % P3_TILE == 0 or mod.shape[0] == 1)
    tiles_per_batch = max(seq // P3_TILE, 1)
    mod_map = (lambda t: (t // tiles_per_batch, 0, 0)) if mod.shape[0] == bsz else (lambda t: (0, 0, 0))
    tok = lambda n: pl.BlockSpec((P3_TILE, n), lambda t: (t, 0))
    cpt = P3_TILE // CHUNK
    s_spec = pl.BlockSpec((cpt, PAIRS, LANES, LANES), lambda t: (t, 0, 0, 0))
    flat = lambda a: a.reshape((n_tok,) + a.shape[2:])
    flat_s = lambda a: a.reshape((a.shape[0] * a.shape[1],) + a.shape[2:])
    consts = (lw["ln_x_w"], lw["ln_x_b"], lw["bones"], lw["w_out"], lw["norm2_g"], lw["w_ff1"], lw["w_ff2"],
              final_g)
    out = pl.pallas_call(
        _phase3_body,
        grid=(nt,),
        in_specs=[tok(D_MODEL), pl.BlockSpec((1, 1, 6 * D_MODEL), mod_map)] + [tok(W)] * 6 + [s_spec, s_spec]
                 + [_const_spec(a.shape) for a in consts],
        out_specs=tok(D_MODEL),
        out_shape=jax.ShapeDtypeStruct((n_tok, D_MODEL), F32),
        scratch_shapes=[pltpu.VMEM((P3_TILE, W), F32)],
        compiler_params=pltpu.CompilerParams(dimension_semantics=("parallel",),
                                             vmem_limit_bytes=VMEM_LIMIT),
        name="phase3_mix_mlp",
    )(flat(x), mod, *[flat(a) for a in p1], flat_s(sf), flat_s(sb), *consts)
    return out.reshape(bsz, seq, D_MODEL)


def _block_diag(blocks):
    rows = sum(b.shape[0] for b in blocks)
    cols = sum(b.shape[1] for b in blocks)
    out = jnp.zeros((rows, cols), blocks[0].dtype)
    r = c = 0
    for b in blocks:
        out = lax.dynamic_update_slice(out, b, (r, c))
        r += b.shape[0]
        c += b.shape[1]
    return out


def _tri_constants():
    t = jnp.arange(TILE)
    same = (t[:, None] // CHUNK) == (t[None, :] // CHUNK)
    tri_f = (same & (t[None, :] <= t[:, None])).astype(BF16)
    tri_b = (same & (t[None, :] >= t[:, None])).astype(BF16)
    lane = jnp.arange(RWKV_WIDTH)
    bones = ((lane[:, None] // HEAD_SIZE) == (lane[None, :] // HEAD_SIZE)).astype(BF16)
    return tri_f, tri_b, bones


def _pair_compact(s):
    b = s.shape[0]
    s = s.reshape(b, PAIRS, 2, HEAD_SIZE, HEAD_SIZE)
    return jnp.swapaxes(s, 2, 3).reshape(b, PAIRS, HEAD_SIZE, LANES)


def kernel(x_prompt, x_sample, c, state_rwkv_fwd, state_rwkv_bwd, c_ctx, w_mod, b_mod, norm1_g, w_in, mu_shift, w0_f, w_w2_f, w0_b, w_w2_b, a0_f, w_a2_f, a0_b, w_a2_b, w_g2, k_k, k_a, r_k, ln_x_w, ln_x_b, conv_w, conv_gain, w_out, norm2_g, w_ff1, w_ff2, final_g):
    depth = w_in.shape[0]
    b_ctx, seq_ctx, _ = x_prompt.shape
    b_lat, seq_lat, _ = x_sample.shape
    assert depth == 1
    assert seq_ctx == TILE and seq_lat % TILE == 0 and TILE % GRID_W == 0
    tri_f, tri_b, bones = _tri_constants()
    row = lambda a: a.reshape(1, -1)
    n_pad = (-(b_lat + 1)) % SUBLANES
    cvec = jnp.concatenate([c, c_ctx[None, :], jnp.zeros((n_pad, D_MODEL), F32)], axis=0)
    zero_state = jnp.zeros((b_ctx, PAIRS, HEAD_SIZE, LANES), F32)

    i = 0
    lw = dict(
        norm1_g=row(norm1_g[i]), w_in=w_in[i].astype(BF16), mu=row(mu_shift[i]),
        w_lora=_block_diag([w_w2_f[i], w_w2_b[i], w_a2_f[i], w_a2_b[i]]),
        b_lora=jnp.concatenate([w0_f[i], w0_b[i], a0_f[i], a0_b[i]]).reshape(1, -1),
        w_g2=w_g2[i], k_k=row(k_k[i]), k_a=row(k_a[i]), r_k=row(r_k[i]),
        conv_w=conv_w[i], conv_gain=row(conv_gain[i]), bones=bones, tri_f=tri_f, tri_b=tri_b,
        ln_x_w=row(ln_x_w[i]), ln_x_b=row(ln_x_b[i]), w_out=w_out[i].astype(BF16),
        norm2_g=row(norm2_g[i]), w_ff1=w_ff1[i].astype(BF16), w_ff2=w_ff2[i].astype(BF16))
    mod = _modulation(cvec, w_mod[i], row(b_mod[i]))
    mod_lat = mod[:b_lat].reshape(b_lat, 1, 6 * D_MODEL)
    mod_ctx = mod[b_lat:b_lat + 1].reshape(1, 1, 6 * D_MODEL)

    def run(x, mod_s, roww, s0f, s0b):
        *p1, mlf, mlb, nnf, nnb, wtf, wtb = _phase1(x, mod_s, roww, lw)
        sf, sb, fin_f, fin_b = _phase2(s0f, s0b, mlf, mlb, nnf, nnb, wtf, wtb)
        return _phase3(x, mod_s, p1, sf, sb, lw, row(final_g)), fin_f, fin_b

    y_ctx, s_f, s_b = run(x_prompt, mod_ctx, seq_ctx, zero_state, zero_state)
    y_lat, _, _ = run(x_sample, mod_lat, GRID_W, _pair_compact(state_rwkv_fwd[:, i]),
                      _pair_compact(state_rwkv_bwd[:, i]))
    return (y_ctx, y_lat, s_f[:, None], s_b[:, None])
```

```python
import functools
import math

import jax
import jax.numpy as jnp
from jax import lax
from jax.experimental import pallas as pl
from jax.experimental.pallas import tpu as pltpu

D_MODEL = 1024
GRID_W = 64
RWKV_WIDTH = D_MODEL // 2
HEAD_SIZE = 64
RWKV_HEADS = RWKV_WIDTH // HEAD_SIZE
CONV_WIDTH = D_MODEL - RWKV_WIDTH
LORA_COLS = 256
GATE_LORA = 128
RWKV_COLS = 3 * RWKV_WIDTH + LORA_COLS + GATE_LORA
IN_COLS = RWKV_COLS + 3 * CONV_WIDTH
D_FF = 4 * D_MODEL
NORM_EPS = 1e-6
GN_EPS = HEAD_SIZE * 1e-5

LANES = 128
SUBLANES = 8
PAIRS = RWKV_WIDTH // LANES
CHUNK = 64
TILE = 256
CHUNKS_PER_TILE = TILE // CHUNK
INV_BASE = 8
DECAY_SCALE = math.exp(-0.5)
P3_TILE = 512
P3_ROWS = 256
FF_CHUNK = 1024
SCAN_CHUNKS = 4
SCAN_BATCH = 8
VMEM_LIMIT = 56 * 1024 * 1024

assert 2 * HEAD_SIZE == LANES and CHUNK == HEAD_SIZE

F32 = jnp.float32
BF16 = jnp.bfloat16
HI = lax.Precision.HIGHEST


def _dot(a, b):
    return jnp.dot(a, b, preferred_element_type=F32)


def _dot_hi(a, b):
    return jnp.dot(a, b, precision=HI, preferred_element_type=F32)


def _mm(a, b):
    return jnp.dot(a.astype(BF16), b.astype(BF16), preferred_element_type=F32)


def _mm_nt(a, b):
    return lax.dot_general(a.astype(BF16), b.astype(BF16), (((1,), (1,)), ((), ())),
                           preferred_element_type=F32)


def _mm_tn(a, b):
    return lax.dot_general(a.astype(BF16), b.astype(BF16), (((0,), (0,)), ((), ())),
                           preferred_element_type=F32)


def _split2(x):
    hi = x.astype(BF16)
    lo = (x - hi.astype(F32)).astype(BF16)
    return hi, lo


def _head_sum(x, bones):
    return _dot(x.astype(BF16), bones)


def _sigmoid(x):
    return 1.0 / (1.0 + jnp.exp(-x))


def _block_diag_rhs(xc):
    first = lax.broadcasted_iota(jnp.int32, xc.shape, 1) < HEAD_SIZE
    zero = jnp.zeros_like(xc)
    return jnp.concatenate([jnp.where(first, xc, zero), jnp.where(first, zero, xc)], axis=0)


def _compact(xbd):
    first = lax.broadcasted_iota(jnp.int32, (HEAD_SIZE, LANES), 1) < HEAD_SIZE
    return jnp.where(first, xbd[0:HEAD_SIZE], xbd[HEAD_SIZE:2 * HEAD_SIZE])


def _const_spec(shape):
    nd = len(shape)
    return pl.BlockSpec(shape, lambda *_: (0,) * nd, pipeline_mode=pl.Buffered(1))


def _mod_body(c_ref, w_ref, b_ref, o_ref):
    cv = c_ref[...]
    s = cv * _sigmoid(cv)
    o_ref[...] = _dot_hi(s, w_ref[...]) + b_ref[...]


def _modulation(cvec, w_mod, b_mod):
    rows = cvec.shape[0]
    n = w_mod.shape[1]
    bn = 1536
    return pl.pallas_call(
        _mod_body,
        grid=(n // bn,),
        in_specs=[pl.BlockSpec((rows, D_MODEL), lambda j: (0, 0)),
                  pl.BlockSpec((D_MODEL, bn), lambda j: (0, j)),
                  pl.BlockSpec((1, bn), lambda j: (0, j))],
        out_specs=pl.BlockSpec((rows, bn), lambda j: (0, j)),
        out_shape=jax.ShapeDtypeStruct((rows, n), F32),
        compiler_params=pltpu.CompilerParams(dimension_semantics=("arbitrary",),
                                             vmem_limit_bytes=VMEM_LIMIT),
        name="modulation",
    )(cvec, w_mod, b_mod)


def _phase1_body(roww, x_ref, mod_ref, g1_ref, win_ref, mu_ref, wl_ref, bl_ref, wg2_ref, kk_ref,
                 ka_ref, rk_ref, cw_ref, cg_ref, bones_ref, trif_ref, trib_ref,
                 y0_ref, rqf_ref, rqb_ref, bonus_ref, gg_ref, oconv_ref, mlf_ref, mlb_ref,
                 nnf_ref, nnb_ref, wtf_ref, wtb_ref,
                 z_scr, opsf_scr, opsb_scr, v_scr, rowf_scr, rowb_scr):
    W = RWKV_WIDTH
    x = x_ref[0]
    mod = mod_ref[0]
    shift_a = mod[:, 0:D_MODEL]
    scale_a = mod[:, D_MODEL:2 * D_MODEL]
    ms = jnp.mean(x * x, axis=-1, keepdims=True)
    h = (x * lax.rsqrt(ms + NORM_EPS)) * g1_ref[...] * (1.0 + scale_a) + shift_a
    h_bf = h.astype(BF16)
    z_scr[:, 3 * W:RWKV_COLS] = _dot(h_bf, win_ref[:, 3 * W:RWKV_COLS])
    z_scr[:, 0:3 * W] = _dot(h_bf, win_ref[:, 0:3 * W])

    t_idx = lax.broadcasted_iota(jnp.int32, (TILE, 1), 0)
    pos = lax.rem(t_idx, roww)
    is_first = pos == 0
    is_last = pos == roww - 1

    def neighbours(zz):
        prev = jnp.where(is_first, 0.0, pltpu.roll(zz, 1, 0))
        nxt = jnp.where(is_last, 0.0, pltpu.roll(zz, TILE - 1, 0))
        return prev, nxt

    def shifted(lo, hi):
        zz = z_scr[:, lo:hi]
        prev, nxt = neighbours(zz)
        mu = mu_ref[:, lo:hi]
        return (1.0 - mu) * zz + (0.5 * mu) * (prev + nxt)

    lora = shifted(3 * W, 3 * W + LORA_COLS)
    lora_in = jnp.concatenate([jnp.tanh(lora[:, :LANES]), lora[:, LANES:]], axis=1)
    pre = _mm(lora_in, wl_ref[...]) + bl_ref[...]
    z_scr[:, RWKV_COLS:IN_COLS] = _dot(h_bf, win_ref[:, RWKV_COLS:IN_COLS])

    r = shifted(0, W)
    k = shifted(W, 2 * W)
    v = shifted(2 * W, 3 * W)
    v_scr[...] = v

    bones = bones_ref[...]
    kraw = k * kk_ref[...]
    kk = kraw * lax.rsqrt(_head_sum(kraw * kraw, bones) + 1e-12)

    log_decay, cum = [], []
    for d in range(2):
        lw = -DECAY_SCALE * _sigmoid(pre[:, d * W:(d + 1) * W])
        tri = trif_ref[...] if d == 0 else trib_ref[...]
        l_hi, l_lo = _split2(lw)
        log_decay.append(lw)
        cum.append(_dot(tri, l_hi) + _dot(tri, l_lo))

    def prepare(d):
        ops_scr, row_scr = (opsf_scr, rowf_scr) if d == 0 else (opsb_scr, rowb_scr)
        lw = log_decay[d]
        a = _sigmoid(pre[:, (2 + d) * W:(3 + d) * W])
        kd = k * (1.0 + (a - 1.0) * ka_ref[...])
        b = kk * a
        c3 = cum[d].reshape(CHUNKS_PER_TILE, CHUNK, W)
        mid = c3[:, CHUNK // 2:CHUNK // 2 + 1, :]
        end_row = CHUNK - 1 if d == 0 else 0
        end = c3[:, end_row:end_row + 1, :]
        cm = (c3 - mid).reshape(TILE, W)
        e_r = jnp.exp(cm)
        e_kap = jnp.exp(cm - lw)
        e_inv = jnp.exp(-cm)
        ops_scr[0] = kk * e_kap
        ops_scr[1] = r * e_r
        ops_scr[2] = kd * e_inv
        ops_scr[3] = b * e_inv
        for j, rv in enumerate((jnp.exp(end - mid), jnp.exp(end), jnp.exp(mid))):
            row_scr[j] = jnp.broadcast_to(rv, (CHUNKS_PER_TILE, SUBLANES, W)).reshape(
                CHUNKS_PER_TILE * SUBLANES, W)
        return kd

    row = lax.broadcasted_iota(jnp.int32, (CHUNK, LANES), 0)
    col = lax.bitwise_and(lax.broadcasted_iota(jnp.int32, (CHUNK, LANES), 1), HEAD_SIZE - 1)
    eye = row == col
    same_block = {}
    size = INV_BASE
    while size <= CHUNK:
        same_block[size] = (row // size) == (col // size)
        size *= 2
    masks = (jnp.concatenate([row > col, row >= col], axis=0),
             jnp.concatenate([row < col, row <= col], axis=0))
    ml_refs = (mlf_ref, mlb_ref)
    nn_refs = (nnf_ref, nnb_ref)
    rq_refs = (rqf_ref, rqb_ref)

    def chains(d):
        ops_scr, row_scr = (opsf_scr, rowf_scr) if d == 0 else (opsb_scr, rowb_scr)
        rows = [pl.ds(ci * CHUNK, CHUNK) for ci in range(CHUNKS_PER_TILE)]
        crow = [pl.ds(ci * SUBLANES, 1) for ci in range(CHUNKS_PER_TILE)]
        lsl = [slice(p * LANES, (p + 1) * LANES) for p in range(PAIRS)]
        ch = [(c, p) for c in range(CHUNKS_PER_TILE) for p in range(PAIRS)]
        n = len(ch)

        v_pair = [v_scr[rows[c], lsl[p]] for (c, p) in ch]
        vbd = [_block_diag_rhs(vp.astype(BF16)) for vp in v_pair]
        kap = [ops_scr[0, rows[c], lsl[p]] for (c, p) in ch]
        rt = [ops_scr[1, rows[c], lsl[p]] for (c, p) in ch]
        kt = [ops_scr[2, rows[c], lsl[p]] for (c, p) in ch]
        bt = [ops_scr[3, rows[c], lsl[p]] for (c, p) in ch]
        to_end = [row_scr[0, crow[c], lsl[p]] for (c, p) in ch]
        e_mid = [row_scr[2, crow[c], lsl[p]] for (c, p) in ch]
        lhs = [jnp.concatenate([kap[i], rt[i]], axis=0).astype(BF16) for i in range(n)]
        a_all = [_mm_nt(lhs[i], jnp.concatenate([_block_diag_rhs(kt[i].astype(BF16)),
                                                 _block_diag_rhs(bt[i].astype(BF16))], axis=0))
                 for i in range(n)]
        a_k = [jnp.where(masks[d], a_all[i][:, 0:LANES], 0.0) for i in range(n)]
        a_b = [jnp.where(masks[d], a_all[i][:, LANES:2 * LANES], 0.0) for i in range(n)]
        a_kb = [x[0:CHUNK] for x in a_b]
        a_rb = [x[CHUNK:2 * CHUNK] for x in a_b]
        av = [_mm(a_k[i], vbd[i]) for i in range(n)]
        l_diag = [jnp.where(same_block[INV_BASE], x, 0.0) for x in a_kb]
        t_inv = [jnp.where(eye, 1.0, 0.0) - x for x in l_diag]
        pw = [_mm(x, _block_diag_rhs(x.astype(BF16))) for x in l_diag]
        n_fac = 2
        while 2 * n_fac < INV_BASE:
            both = [_mm(jnp.concatenate([t, w], axis=0), _block_diag_rhs(w.astype(BF16)))
                    for t, w in zip(t_inv, pw)]
            t_inv = [t + x[0:CHUNK] for t, x in zip(t_inv, both)]
            pw = [x[CHUNK:2 * CHUNK] for x in both]
            n_fac *= 2
        t_inv = [t + _mm(t, _block_diag_rhs(w.astype(BF16))) for t, w in zip(t_inv, pw)]
        sizes = []
        size = INV_BASE
        while size < CHUNK:
            sizes.append(size)
            size *= 2
        half = CHUNK // 2

        def hit(s):
            return [j for j in range(CHUNK // s) if j % 2 == (1 if d == 0 else 0)]

        def gather(x, s):
            return jnp.concatenate([x[j * s:(j + 1) * s] for j in hit(s)], axis=0)

        def scatter(base, upd, s):
            rows_of = {j: i for i, j in enumerate(hit(s))}
            parts = [upd[rows_of[j] * s:(rows_of[j] + 1) * s] if j in rows_of
                     else (jnp.zeros((s, LANES), F32) if base is None else base[j * s:(j + 1) * s])
                     for j in range(CHUNK // s)]
            return jnp.concatenate(parts, axis=0)

        off = [[gather(jnp.where(same_block[2 * s] & ~same_block[s], x, 0.0), s) for s in sizes]
               for x in a_kb]
        y_all = [_mm(jnp.concatenate(o, axis=0), _block_diag_rhs(t.astype(BF16)))
                 for o, t in zip(off, t_inv)]
        y_lvl = [[y[j * half:(j + 1) * half] for j in range(len(sizes))] for y in y_all]
        for li, s in enumerate(sizes):
            lhs_m = [jnp.concatenate([gather(t, s)] + y[li + 1:], axis=0) for t, y in zip(t_inv, y_lvl)]
            prod = [_mm(l, _block_diag_rhs(scatter(None, y[li], s).astype(BF16)))
                    for l, y in zip(lhs_m, y_lvl)]
            t_inv = [scatter(t, gather(t, s) - x[0:half], s) for t, x in zip(t_inv, prod)]
            y_lvl = [y[:li + 1] + [y[j] - x[(j - li) * half:(j - li + 1) * half]
                                   for j in range(li + 1, len(sizes))]
                     for y, x in zip(y_lvl, prod)]
        rhs = [jnp.concatenate([_block_diag_rhs((kap[i] * e_mid[i]).astype(BF16)),
                                _block_diag_rhs(av[i][0:CHUNK].astype(BF16))], axis=1) for i in range(n)]
        pq = [_mm(t_inv[i], rhs[i]) for i in range(n)]
        rhs2 = [jnp.concatenate([_block_diag_rhs(x[:, 0:LANES].astype(BF16)),
                                 _block_diag_rhs(x[:, LANES:2 * LANES].astype(BF16))], axis=1) for x in pq]
        rb = [_mm(a_rb[i], rhs2[i]) for i in range(n)]
        for i, (c, p) in enumerate(ch):
            rq_refs[d][0, rows[c], lsl[p]] = (rt[i] * e_mid[i] - rb[i][:, 0:LANES]).astype(BF16)
        y0d = [av[i][CHUNK:2 * CHUNK] - rb[i][:, LANES:2 * LANES] for i in range(n)]
        b_end = [(bt[i] * to_end[i]).astype(BF16) for i in range(n)]
        k_end = [(kt[i] * to_end[i]).astype(BF16) for i in range(n)]
        pq_b = [_mm(pq[i].T, b_end[i]) for i in range(n)]
        vk = [_mm(v_pair[i].T, k_end[i]) for i in range(n)]
        for i, (c, p) in enumerate(ch):
            ml_refs[d][0, c, p] = _compact(pq_b[i][0:LANES]).astype(BF16)
            nn_refs[d][0, c, p] = (_compact(vk[i]) - _compact(pq_b[i][LANES:2 * LANES])).astype(BF16)
        return y0d

    kd_f = prepare(0)
    y0_f = chains(0)
    kd_b = prepare(1)

    gl = shifted(3 * W + LORA_COLS, RWKV_COLS)
    gg_ref[0] = _mm(_sigmoid(gl), wg2_ref[...])
    bonus_ref[0] = _head_sum(r * (kd_f + kd_b) * rk_ref[...], bones) * v
    gate_b = z_scr[:, RWKV_COLS:RWKV_COLS + CONV_WIDTH]
    u = z_scr[:, RWKV_COLS + CONV_WIDTH:RWKV_COLS + 2 * CONV_WIDTH] * z_scr[:, RWKV_COLS + 2 * CONV_WIDTH:IN_COLS]
    up, un = neighbours(u)
    oc = gate_b * (cw_ref[0:1, :] * up + cw_ref[1:2, :] * u + cw_ref[2:3, :] * un)
    oc = oc * lax.rsqrt(jnp.mean(oc * oc, axis=-1, keepdims=True) + NORM_EPS) * cg_ref[...]
    oconv_ref[0] = oc.astype(BF16)

    y0_b = chains(1)
    for i in range(len(y0_f)):
        ci, p = divmod(i, PAIRS)
        y0_ref[0, ci * CHUNK:(ci + 1) * CHUNK, p * LANES:(p + 1) * LANES] = y0_f[i] + y0_b[i]
    wtf_ref[0] = rowf_scr[1]
    wtb_ref[0] = rowb_scr[1]


def _phase1(x, mod, roww, lw):
    bsz, seq, _ = x.shape
    nt = seq // TILE
    nc = seq // CHUNK
    mod_map = (lambda b, t: (b, 0, 0)) if mod.shape[0] == bsz else (lambda b, t: (0, 0, 0))
    tok = lambda n: pl.BlockSpec((1, TILE, n), lambda b, t: (b, t, 0))
    mn_spec = pl.BlockSpec((1, CHUNKS_PER_TILE, PAIRS, HEAD_SIZE, LANES), lambda b, t: (b, t, 0, 0, 0))
    W = RWKV_WIDTH
    consts = (lw["norm1_g"], lw["w_in"], lw["mu"], lw["w_lora"], lw["b_lora"], lw["w_g2"], lw["k_k"],
              lw["k_a"], lw["r_k"], lw["conv_w"], lw["conv_gain"], lw["bones"], lw["tri_f"], lw["tri_b"])
    tok_shape = jax.ShapeDtypeStruct((bsz, seq, W), F32)
    tok_bf16 = jax.ShapeDtypeStruct((bsz, seq, W), BF16)
    ml_shape = jax.ShapeDtypeStruct((bsz, nc, PAIRS, HEAD_SIZE, LANES), BF16)
    nn_shape = jax.ShapeDtypeStruct((bsz, nc, PAIRS, HEAD_SIZE, LANES), BF16)
    wt_spec = pl.BlockSpec((1, CHUNKS_PER_TILE * SUBLANES, W), lambda b, t: (b, t, 0))
    wt_shape = jax.ShapeDtypeStruct((bsz, nc * SUBLANES, W), F32)
    return pl.pallas_call(
        functools.partial(_phase1_body, roww),
        grid=(bsz, nt),
        in_specs=[tok(D_MODEL), pl.BlockSpec((1, 1, 6 * D_MODEL), mod_map)]
                 + [_const_spec(a.shape) for a in consts],
        out_specs=[tok(W)] * 6 + [mn_spec] * 4 + [wt_spec, wt_spec],
        out_shape=[tok_shape, tok_bf16, tok_bf16, tok_shape, tok_shape, tok_bf16,
                   ml_shape, ml_shape, nn_shape, nn_shape, wt_shape, wt_shape],
        scratch_shapes=[pltpu.VMEM((TILE, IN_COLS), F32),
                        pltpu.VMEM((4, TILE, W), F32),
                        pltpu.VMEM((4, TILE, W), F32),
                        pltpu.VMEM((TILE, W), F32),
                        pltpu.VMEM((3, CHUNKS_PER_TILE * SUBLANES, W), F32),
                        pltpu.VMEM((3, CHUNKS_PER_TILE * SUBLANES, W), F32)],
        compiler_params=pltpu.CompilerParams(dimension_semantics=("parallel", "parallel"),
                                             vmem_limit_bytes=VMEM_LIMIT),
        name="phase1_chunk_summaries",
    )(x, mod, *consts)


def _phase2_body(nb, s0f_ref, s0b_ref, mlf_ref, mlb_ref, nnf_ref, nnb_ref, wtf_ref, wtb_ref,
                 sf_ref, sb_ref, finf_ref, finb_ref, st_scr):
    step = pl.program_id(1)

    @pl.when(step == 0)
    def _():
        st_scr[0] = s0f_ref[...]
        st_scr[1] = s0b_ref[...]

    chains = [(b, d, p) for b in range(nb) for d in range(2) for p in range(PAIRS)]
    ml_refs = (mlf_ref, mlb_ref)
    nn_refs = (nnf_ref, nnb_ref)
    wt_refs = (wtf_ref, wtb_ref)
    s_refs = (sf_ref, sb_ref)
    s = [st_scr[d, b, p] for (b, d, p) in chains]
    for j in range(SCAN_CHUNKS):
        cj = (j, SCAN_CHUNKS - 1 - j)
        for i, (b, d, p) in enumerate(chains):
            s_refs[d][b, cj[d], p] = _compact(_block_diag_rhs(s[i].astype(BF16)).T)
        low = [_mm(s[i], _block_diag_rhs(ml_refs[d][b, cj[d], p])) for i, (b, d, p) in enumerate(chains)]
        for i, (b, d, p) in enumerate(chains):
            w_row = wt_refs[d][b, cj[d] * SUBLANES:cj[d] * SUBLANES + 1, p * LANES:(p + 1) * LANES]
            s[i] = s[i] * w_row - low[i] + nn_refs[d][b, cj[d], p]
    for i, (b, d, p) in enumerate(chains):
        st_scr[d, b, p] = s[i]

    @pl.when(step == pl.num_programs(1) - 1)
    def _():
        for i, (b, d, p) in enumerate(chains):
            fin_ref = finf_ref if d == 0 else finb_ref
            fin_ref[b, 2 * p] = s[i][:, 0:HEAD_SIZE]
            fin_ref[b, 2 * p + 1] = s[i][:, HEAD_SIZE:2 * HEAD_SIZE]


def _phase2(s0f, s0b, mlf, mlb, nnf, nnb, wtf, wtb):
    bsz, nc = mlf.shape[0], mlf.shape[1]
    bb = min(bsz, SCAN_BATCH)
    assert bsz % bb == 0 and nc % SCAN_CHUNKS == 0
    ns = nc // SCAN_CHUNKS
    st_blk = (bb, PAIRS, HEAD_SIZE, LANES)
    mn_blk = (bb, SCAN_CHUNKS, PAIRS, HEAD_SIZE, LANES)
    s_blk = (bb, SCAN_CHUNKS, PAIRS, HEAD_SIZE, LANES)
    wt_blk = (bb, SCAN_CHUNKS * SUBLANES, RWKV_WIDTH)
    fwd = lambda g, i: (g, i, 0, 0, 0)
    bwd = lambda g, i: (g, ns - 1 - i, 0, 0, 0)
    full = pl.BlockSpec(st_blk, lambda g, i: (g, 0, 0, 0))
    s_all = jax.ShapeDtypeStruct((bsz, nc, PAIRS, HEAD_SIZE, LANES), BF16)
    fin_spec = pl.BlockSpec((bb, RWKV_HEADS, HEAD_SIZE, HEAD_SIZE), lambda g, i: (g, 0, 0, 0))
    s_fin = jax.ShapeDtypeStruct((bsz, RWKV_HEADS, HEAD_SIZE, HEAD_SIZE), F32)
    return pl.pallas_call(
        functools.partial(_phase2_body, bb),
        grid=(bsz // bb, ns),
        in_specs=[full, full, pl.BlockSpec(mn_blk, fwd), pl.BlockSpec(mn_blk, bwd),
                  pl.BlockSpec(mn_blk, fwd), pl.BlockSpec(mn_blk, bwd),
                  pl.BlockSpec(wt_blk, lambda g, i: (g, i, 0)),
                  pl.BlockSpec(wt_blk, lambda g, i: (g, ns - 1 - i, 0))],
        out_specs=[pl.BlockSpec(s_blk, fwd), pl.BlockSpec(s_blk, bwd), fin_spec, fin_spec],
        out_shape=[s_all, s_all, s_fin, s_fin],
        scratch_shapes=[pltpu.VMEM((2,) + st_blk, F32)],
        compiler_params=pltpu.CompilerParams(dimension_semantics=("parallel", "arbitrary"),
                                             vmem_limit_bytes=VMEM_LIMIT),
        name="phase2_state_scan",
    )(s0f, s0b, mlf, mlb, nnf, nnb, wtf, wtb)


def _phase3_body(x_ref, mod_ref, y0_ref, rqf_ref, rqb_ref, bonus_ref, gg_ref, oconv_ref, sf_ref, sb_ref,
                 lnw_ref, lnb_ref, bones_ref, wout_ref, g2_ref, wff1_ref, wff2_ref, gfin_ref,
                 o_ref, y_scr):
    W = RWKV_WIDTH
    mod = mod_ref[0]
    gate_a = mod[:, 2 * D_MODEL:3 * D_MODEL]
    shift_f = mod[:, 3 * D_MODEL:4 * D_MODEL]
    scale_f = mod[:, 4 * D_MODEL:5 * D_MODEL]
    gate_f = mod[:, 5 * D_MODEL:6 * D_MODEL]
    inv_n = 1.0 / HEAD_SIZE
    carried = {}

    def mix(blk):
        r0 = blk * P3_ROWS
        rsl = slice(r0, r0 + P3_ROWS)
        for ci in range(P3_ROWS // CHUNK):
            cg = blk * (P3_ROWS // CHUNK) + ci
            rows = slice(cg * CHUNK, (cg + 1) * CHUNK)
            for p in range(PAIRS):
                ls = slice(p * LANES, (p + 1) * LANES)
                y_scr[rows, ls] = y0_ref[rows, ls] + _dot(
                    jnp.concatenate([rqf_ref[rows, ls], rqb_ref[rows, ls]], axis=1),
                    jnp.concatenate([_block_diag_rhs(sf_ref[cg, p]), _block_diag_rhs(sb_ref[cg, p])], axis=0))
        yield
        y = y_scr[rsl, :]
        yc = y - _head_sum(y, bones_ref[...]) * inv_n
        yield
        y_var = _head_sum(yc * yc, bones_ref[...]) * inv_n
        yn = yc * lax.rsqrt(y_var + GN_EPS) * lnw_ref[...] + lnb_ref[...]
        o_rwkv = (yn + bonus_ref[rsl, :]) * gg_ref[rsl, :]
        yield
        mixed = (_dot(o_rwkv.astype(BF16), wout_ref[0:W, :])
                 + _dot(oconv_ref[rsl, :], wout_ref[W:D_MODEL, :]))
        x1 = x_ref[rsl, :] + gate_a * mixed
        ms = jnp.mean(x1 * x1, axis=-1, keepdims=True)
        h2 = (x1 * lax.rsqrt(ms + NORM_EPS)) * g2_ref[...] * (1.0 + scale_f) + shift_f
        carried[blk] = (x1, h2.astype(BF16))

    def mlp(blk):
        rsl = slice(blk * P3_ROWS, (blk + 1) * P3_ROWS)
        x1, h2 = carried.pop(blk)
        acc = None
        for j in range(D_FF // FF_CHUNK):
            cols = slice(j * FF_CHUNK, (j + 1) * FF_CHUNK)
            f1 = jnp.maximum(_dot(h2, wff1_ref[:, cols]), 0.0)
            f2 = _dot((f1 * f1).astype(BF16), wff2_ref[cols, :])
            acc = f2 if acc is None else acc + f2
            yield
        x2 = x1 + gate_f * acc
        ms2 = jnp.mean(x2 * x2, axis=-1, keepdims=True)
        o_ref[rsl, :] = (x2 * lax.rsqrt(ms2 + NORM_EPS)) * gfin_ref[...]

    def interleave(*gens):
        live = list(gens)
        while live:
            for g in list(live):
                try:
                    next(g)
                except StopIteration:
                    live.remove(g)

    n_blk = P3_TILE // P3_ROWS
    interleave(mix(0))
    for blk in range(1, n_blk):
        interleave(mlp(blk - 1), mix(blk))
    interleave(mlp(n_blk - 1))


def _phase3(x, mod, p1, sf, sb, lw, final_g):
    bsz, seq, _ = x.shape
    W = RWKV_WIDTH
    n_tok = bsz * seq
    nt = n_tok // P3_TILE
    assert n_tok % P3_TILE == 0 and (seq % P3_TILE == 0 or mod.shape[0] == 1)
    tiles_per_batch = max(seq // P3_TILE, 1)
    mod_map = (lambda t: (t // tiles_per_batch, 0, 0)) if mod.shape[0] == bsz else (lambda t: (0, 0, 0))
    tok = lambda n: pl.BlockSpec((P3_TILE, n), lambda t: (t, 0))
    cpt = P3_TILE // CHUNK
    s_spec = pl.BlockSpec((cpt, PAIRS, HEAD_SIZE, LANES), lambda t: (t, 0, 0, 0))
    flat = lambda a: a.reshape((n_tok,) + a.shape[2:])
    flat_s = lambda a: a.reshape((a.shape[0] * a.shape[1],) + a.shape[2:])
    consts = (lw["ln_x_w"], lw["ln_x_b"], lw["bones"], lw["w_out"], lw["norm2_g"], lw["w_ff1"], lw["w_ff2"],
              final_g)
    out = pl.pallas_call(
        _phase3_body,
        grid=(nt,),
        in_specs=[tok(D_MODEL), pl.BlockSpec((1, 1, 6 * D_MODEL), mod_map)] + [tok(W)] * 6 + [s_spec, s_spec]
                 + [_const_spec(a.shape) for a in consts],
        out_specs=tok(D_MODEL),
        out_shape=jax.ShapeDtypeStruct((n_tok, D_MODEL), F32),
        scratch_shapes=[pltpu.VMEM((P3_TILE, W), F32)],
        compiler_params=pltpu.CompilerParams(dimension_semantics=("parallel",),
                                             vmem_limit_bytes=VMEM_LIMIT),
        name="phase3_mix_mlp",
    )(flat(x), mod, *[flat(a) for a in p1], flat_s(sf), flat_s(sb), *consts)
    return out.reshape(bsz, seq, D_MODEL)


def _block_diag(blocks):
    rows = sum(b.shape[0] for b in blocks)
    cols = sum(b.shape[1] for b in blocks)
    out = jnp.zeros((rows, cols), blocks[0].dtype)
    r = c = 0
    for b in blocks:
        out = lax.dynamic_update_slice(out, b, (r, c))
        r += b.shape[0]
        c += b.shape[1]
    return out


def _tri_constants():
    t = jnp.arange(TILE)
    same = (t[:, None] // CHUNK) == (t[None, :] // CHUNK)
    tri_f = (same & (t[None, :] <= t[:, None])).astype(BF16)
    tri_b = (same & (t[None, :] >= t[:, None])).astype(BF16)
    lane = jnp.arange(RWKV_WIDTH)
    bones = ((lane[:, None] // HEAD_SIZE) == (lane[None, :] // HEAD_SIZE)).astype(BF16)
    return tri_f, tri_b, bones


def _pair_compact(s):
    b = s.shape[0]
    s = s.reshape(b, PAIRS, 2, HEAD_SIZE, HEAD_SIZE)
    return jnp.swapaxes(s, 2, 3).reshape(b, PAIRS, HEAD_SIZE, LANES)


def kernel(x_prompt, x_sample, c, state_rwkv_fwd, state_rwkv_bwd, c_ctx, w_mod, b_mod, norm1_g, w_in, mu_shift, w0_f, w_w2_f, w0_b, w_w2_b, a0_f, w_a2_f, a0_b, w_a2_b, w_g2, k_k, k_a, r_k, ln_x_w, ln_x_b, conv_w, conv_gain, w_out, norm2_g, w_ff1, w_ff2, final_g):
    depth = w_in.shape[0]
    b_ctx, seq_ctx, _ = x_prompt.shape
    b_lat, seq_lat, _ = x_sample.shape
    assert depth == 1
    assert seq_ctx == TILE and seq_lat % TILE == 0 and TILE % GRID_W == 0
    tri_f, tri_b, bones = _tri_constants()
    row = lambda a: a.reshape(1, -1)
    n_pad = (-(b_lat + 1)) % SUBLANES
    cvec = jnp.concatenate([c, c_ctx[None, :], jnp.zeros((n_pad, D_MODEL), F32)], axis=0)
    zero_state = jnp.zeros((b_ctx, PAIRS, HEAD_SIZE, LANES), F32)

    i = 0
    lw = dict(
        norm1_g=row(norm1_g[i]), w_in=w_in[i].astype(BF16), mu=row(mu_shift[i]),
        w_lora=_block_diag([w_w2_f[i], w_w2_b[i], w_a2_f[i], w_a2_b[i]]),
        b_lora=jnp.concatenate([w0_f[i], w0_b[i], a0_f[i], a0_b[i]]).reshape(1, -1),
        w_g2=w_g2[i], k_k=row(k_k[i]), k_a=row(k_a[i]), r_k=row(r_k[i]),
        conv_w=conv_w[i], conv_gain=row(conv_gain[i]), bones=bones, tri_f=tri_f, tri_b=tri_b,
        ln_x_w=row(ln_x_w[i]), ln_x_b=row(ln_x_b[i]), w_out=w_out[i].astype(BF16),
        norm2_g=row(norm2_g[i]), w_ff1=w_ff1[i].astype(BF16), w_ff2=w_ff2[i].astype(BF16))
    mod = _modulation(cvec, w_mod[i], row(b_mod[i]))
    mod_lat = mod[:b_lat].reshape(b_lat, 1, 6 * D_MODEL)
    mod_ctx = mod[b_lat:b_lat + 1].reshape(1, 1, 6 * D_MODEL)

    def run(x, mod_s, roww, s0f, s0b):
        *p1, mlf, mlb, nnf, nnb, wtf, wtb = _phase1(x, mod_s, roww, lw)
        sf, sb, fin_f, fin_b = _phase2(s0f, s0b, mlf, mlb, nnf, nnb, wtf, wtb)
        return _phase3(x, mod_s, p1, sf, sb, lw, row(final_g)), fin_f, fin_b

    y_ctx, s_f, s_b = run(x_prompt, mod_ctx, seq_ctx, zero_state, zero_state)
    y_lat, _, _ = run(x_sample, mod_lat, GRID_W, _pair_compact(state_rwkv_fwd[:, i]),
                      _pair_compact(state_rwkv_bwd[:, i]))
    return (y_ctx, y_lat, s_f[:, None], s_b[:, None])
```

```python
import functools
import math

import jax
import jax.numpy as jnp
from jax import lax
from jax.experimental import pallas as pl
from jax.experimental.pallas import tpu as pltpu

D_MODEL = 1024
GRID_W = 64
RWKV_WIDTH = D_MODEL // 2
HEAD_SIZE = 64
RWKV_HEADS = RWKV_WIDTH // HEAD_SIZE
CONV_WIDTH = D_MODEL - RWKV_WIDTH
LORA_COLS = 256
GATE_LORA = 128
RWKV_COLS = 3 * RWKV_WIDTH + LORA_COLS + GATE_LORA
IN_COLS = RWKV_COLS + 3 * CONV_WIDTH
D_FF = 4 * D_MODEL
NORM_EPS = 1e-6
GN_EPS = HEAD_SIZE * 1e-5

LANES = 128
SUBLANES = 8
PAIRS = RWKV_WIDTH // LANES
CHUNK = 64
TILE = 256
CHUNKS_PER_TILE = TILE // CHUNK
INV_BASE = 8
DECAY_SCALE = math.exp(-0.5)
P3_TILE = 512
P3_ROWS = 256
FF_CHUNK = 1024
SCAN_CHUNKS = 4
SCAN_BATCH = 8
VMEM_LIMIT = 56 * 1024 * 1024

assert 2 * HEAD_SIZE == LANES and CHUNK == HEAD_SIZE

F32 = jnp.float32
BF16 = jnp.bfloat16
HI = lax.Precision.HIGHEST


def _dot(a, b):
    return jnp.dot(a, b, preferred_element_type=F32)


def _dot_hi(a, b):
    return jnp.dot(a, b, precision=HI, preferred_element_type=F32)


def _mm(a, b):
    return jnp.dot(a.astype(BF16), b.astype(BF16), preferred_element_type=F32)


def _mm_nt(a, b):
    return lax.dot_general(a.astype(BF16), b.astype(BF16), (((1,), (1,)), ((), ())),
                           preferred_element_type=F32)


def _split2(x):
    hi = x.astype(BF16)
    lo = (x - hi.astype(F32)).astype(BF16)
    return hi, lo


def _head_sum(x, bones):
    return _dot(x.astype(BF16), bones)


def _sigmoid(x):
    return 1.0 / (1.0 + jnp.exp(-x))


def _block_diag_rhs(xc):
    first = lax.broadcasted_iota(jnp.int32, xc.shape, 1) < HEAD_SIZE
    zero = jnp.zeros_like(xc)
    return jnp.concatenate([jnp.where(first, xc, zero), jnp.where(first, zero, xc)], axis=0)


def _compact(xbd):
    first = lax.broadcasted_iota(jnp.int32, (HEAD_SIZE, LANES), 1) < HEAD_SIZE
    return jnp.where(first, xbd[0:HEAD_SIZE], xbd[HEAD_SIZE:2 * HEAD_SIZE])


def _const_spec(shape):
    nd = len(shape)
    return pl.BlockSpec(shape, lambda *_: (0,) * nd, pipeline_mode=pl.Buffered(1))


def _mod_body(c_ref, w_ref, b_ref, o_ref):
    cv = c_ref[...]
    s = cv * _sigmoid(cv)
    o_ref[...] = _dot_hi(s, w_ref[...]) + b_ref[...]


def _modulation(cvec, w_mod, b_mod):
    rows = cvec.shape[0]
    n = w_mod.shape[1]
    bn = 1536
    return pl.pallas_call(
        _mod_body,
        grid=(n // bn,),
        in_specs=[pl.BlockSpec((rows, D_MODEL), lambda j: (0, 0)),
                  pl.BlockSpec((D_MODEL, bn), lambda j: (0, j)),
                  pl.BlockSpec((1, bn), lambda j: (0, j))],
        out_specs=pl.BlockSpec((rows, bn), lambda j: (0, j)),
        out_shape=jax.ShapeDtypeStruct((rows, n), F32),
        compiler_params=pltpu.CompilerParams(dimension_semantics=("arbitrary",),
                                             vmem_limit_bytes=VMEM_LIMIT),
        name="modulation",
    )(cvec, w_mod, b_mod)


def _phase1_body(roww, x_ref, mod_ref, g1_ref, win_ref, mu_ref, wl_ref, bl_ref, wg2_ref, kk_ref,
                 ka_ref, rk_ref, cw_ref, cg_ref, bones_ref, trif_ref, trib_ref,
                 y0_ref, rqf_ref, rqb_ref, bonus_ref, gg_ref, oconv_ref, mlf_ref, mlb_ref,
                 nnf_ref, nnb_ref, wtf_ref, wtb_ref,
                 z_scr, opsf_scr, opsb_scr, v_scr, rowf_scr, rowb_scr):
    W = RWKV_WIDTH
    x = x_ref[0]
    mod = mod_ref[0]
    shift_a = mod[:, 0:D_MODEL]
    scale_a = mod[:, D_MODEL:2 * D_MODEL]
    ms = jnp.mean(x * x, axis=-1, keepdims=True)
    h = (x * lax.rsqrt(ms + NORM_EPS)) * g1_ref[...] * (1.0 + scale_a) + shift_a
    h_bf = h.astype(BF16)
    z_scr[:, 3 * W:RWKV_COLS] = _dot(h_bf, win_ref[:, 3 * W:RWKV_COLS])
    z_scr[:, 0:3 * W] = _dot(h_bf, win_ref[:, 0:3 * W])

    t_idx = lax.broadcasted_iota(jnp.int32, (TILE, 1), 0)
    pos = lax.rem(t_idx, roww)
    is_first = pos == 0
    is_last = pos == roww - 1

    def neighbours(zz):
        prev = jnp.where(is_first, 0.0, pltpu.roll(zz, 1, 0))
        nxt = jnp.where(is_last, 0.0, pltpu.roll(zz, TILE - 1, 0))
        return prev, nxt

    def shifted(lo, hi):
        zz = z_scr[:, lo:hi]
        prev, nxt = neighbours(zz)
        mu = mu_ref[:, lo:hi]
        return (1.0 - mu) * zz + (0.5 * mu) * (prev + nxt)

    lora = shifted(3 * W, 3 * W + LORA_COLS)
    lora_in = jnp.concatenate([jnp.tanh(lora[:, :LANES]), lora[:, LANES:]], axis=1)
    pre = _mm(lora_in, wl_ref[...]) + bl_ref[...]
    z_scr[:, RWKV_COLS:IN_COLS] = _dot(h_bf, win_ref[:, RWKV_COLS:IN_COLS])

    r = shifted(0, W)
    k = shifted(W, 2 * W)
    v = shifted(2 * W, 3 * W)
    v_scr[...] = v

    bones = bones_ref[...]
    kraw = k * kk_ref[...]
    kk = kraw * lax.rsqrt(_head_sum(kraw * kraw, bones) + 1e-12)

    log_decay, cum = [], []
    for d in range(2):
        lw = -DECAY_SCALE * _sigmoid(pre[:, d * W:(d + 1) * W])
        tri = trif_ref[...] if d == 0 else trib_ref[...]
        l_hi, l_lo = _split2(lw)
        log_decay.append(lw)
        cum.append(_dot(tri, l_hi) + _dot(tri, l_lo))

    def prepare(d):
        ops_scr, row_scr = (opsf_scr, rowf_scr) if d == 0 else (opsb_scr, rowb_scr)
        lw = log_decay[d]
        a = _sigmoid(pre[:, (2 + d) * W:(3 + d) * W])
        kd = k * (1.0 + (a - 1.0) * ka_ref[...])
        b = kk * a
        c3 = cum[d].reshape(CHUNKS_PER_TILE, CHUNK, W)
        mid = c3[:, CHUNK // 2:CHUNK // 2 + 1, :]
        end_row = CHUNK - 1 if d == 0 else 0
        end = c3[:, end_row:end_row + 1, :]
        cm = (c3 - mid).reshape(TILE, W)
        e_r = jnp.exp(cm)
        e_kap = jnp.exp(cm - lw)
        e_inv = jnp.exp(-cm)
        ops_scr[0] = kk * e_kap
        ops_scr[1] = r * e_r
        ops_scr[2] = kd * e_inv
        ops_scr[3] = b * e_inv
        for j, rv in enumerate((jnp.exp(end - mid), jnp.exp(end), jnp.exp(mid))):
            row_scr[j] = jnp.broadcast_to(rv, (CHUNKS_PER_TILE, SUBLANES, W)).reshape(
                CHUNKS_PER_TILE * SUBLANES, W)
        return kd

    row = lax.broadcasted_iota(jnp.int32, (CHUNK, LANES), 0)
    col = lax.bitwise_and(lax.broadcasted_iota(jnp.int32, (CHUNK, LANES), 1), HEAD_SIZE - 1)
    eye = row == col
    same_block = {}
    size = INV_BASE
    while size <= CHUNK:
        same_block[size] = (row // size) == (col // size)
        size *= 2
    masks = (jnp.concatenate([row > col, row >= col], axis=0),
             jnp.concatenate([row < col, row <= col], axis=0))
    ml_refs = (mlf_ref, mlb_ref)
    nn_refs = (nnf_ref, nnb_ref)
    rq_refs = (rqf_ref, rqb_ref)

    def chains(d):
        ops_scr, row_scr = (opsf_scr, rowf_scr) if d == 0 else (opsb_scr, rowb_scr)
        rows = [pl.ds(ci * CHUNK, CHUNK) for ci in range(CHUNKS_PER_TILE)]
        crow = [pl.ds(ci * SUBLANES, 1) for ci in range(CHUNKS_PER_TILE)]
        lsl = [slice(p * LANES, (p + 1) * LANES) for p in range(PAIRS)]
        ch = [(c, p) for c in range(CHUNKS_PER_TILE) for p in range(PAIRS)]
        n = len(ch)

        v_pair = [v_scr[rows[c], lsl[p]] for (c, p) in ch]
        vbd = [_block_diag_rhs(vp.astype(BF16)) for vp in v_pair]
        kap = [ops_scr[0, rows[c], lsl[p]] for (c, p) in ch]
        rt = [ops_scr[1, rows[c], lsl[p]] for (c, p) in ch]
        kt = [ops_scr[2, rows[c], lsl[p]] for (c, p) in ch]
        bt = [ops_scr[3, rows[c], lsl[p]] for (c, p) in ch]
        to_end = [row_scr[0, crow[c], lsl[p]] for (c, p) in ch]
        e_mid = [row_scr[2, crow[c], lsl[p]] for (c, p) in ch]
        lhs = [jnp.concatenate([kap[i], rt[i]], axis=0).astype(BF16) for i in range(n)]
        a_all = [_mm_nt(lhs[i], jnp.concatenate([_block_diag_rhs(kt[i].astype(BF16)),
                                                 _block_diag_rhs(bt[i].astype(BF16))], axis=0))
                 for i in range(n)]
        a_k = [jnp.where(masks[d], a_all[i][:, 0:LANES], 0.0) for i in range(n)]
        a_b = [jnp.where(masks[d], a_all[i][:, LANES:2 * LANES], 0.0) for i in range(n)]
        a_kb = [x[0:CHUNK] for x in a_b]
        a_rb = [x[CHUNK:2 * CHUNK] for x in a_b]
        av = [_mm(a_k[i], vbd[i]) for i in range(n)]
        l_diag = [jnp.where(same_block[INV_BASE], x, 0.0) for x in a_kb]
        t_inv = [jnp.where(eye, 1.0, 0.0) - x for x in l_diag]
        pw = [_mm(x, _block_diag_rhs(x.astype(BF16))) for x in l_diag]
        n_fac = 2
        while 2 * n_fac < INV_BASE:
            both = [_mm(jnp.concatenate([t, w], axis=0), _block_diag_rhs(w.astype(BF16)))
                    for t, w in zip(t_inv, pw)]
            t_inv = [t + x[0:CHUNK] for t, x in zip(t_inv, both)]
            pw = [x[CHUNK:2 * CHUNK] for x in both]
            n_fac *= 2
        t_inv = [t + _mm(t, _block_diag_rhs(w.astype(BF16))) for t, w in zip(t_inv, pw)]
        sizes = []
        size = INV_BASE
        while size < CHUNK:
            sizes.append(size)
            size *= 2
        half = CHUNK // 2

        def hit(s):
            return [j for j in range(CHUNK // s) if j % 2 == (1 if d == 0 else 0)]

        def gather(x, s):
            return jnp.concatenate([x[j * s:(j + 1) * s] for j in hit(s)], axis=0)

        def scatter(base, upd, s):
            rows_of = {j: i for i, j in enumerate(hit(s))}
            parts = [upd[rows_of[j] * s:(rows_of[j] + 1) * s] if j in rows_of
                     else (jnp.zeros((s, LANES), F32) if base is None else base[j * s:(j + 1) * s])
                     for j in range(CHUNK // s)]
            return jnp.concatenate(parts, axis=0)

        off = [[gather(jnp.where(same_block[2 * s] & ~same_block[s], x, 0.0), s) for s in sizes]
               for x in a_kb]
        y_all = [_mm(jnp.concatenate(o, axis=0), _block_diag_rhs(t.astype(BF16)))
                 for o, t in zip(off, t_inv)]
        y_lvl = [[y[j * half:(j + 1) * half] for j in range(len(sizes))] for y in y_all]
        for li, s in enumerate(sizes):
            lhs_m = [jnp.concatenate([gather(t, s)] + y[li + 1:], axis=0) for t, y in zip(t_inv, y_lvl)]
            prod = [_mm(l, _block_diag_rhs(scatter(None, y[li], s).astype(BF16)))
                    for l, y in zip(lhs_m, y_lvl)]
            t_inv = [scatter(t, gather(t, s) - x[0:half], s) for t, x in zip(t_inv, prod)]
            y_lvl = [y[:li + 1] + [y[j] - x[(j - li) * half:(j - li + 1) * half]
                                   for j in range(li + 1, len(sizes))]
                     for y, x in zip(y_lvl, prod)]
        rhs = [jnp.concatenate([_block_diag_rhs((kap[i] * e_mid[i]).astype(BF16)),
                                _block_diag_rhs(av[i][0:CHUNK].astype(BF16))], axis=1) for i in range(n)]
        pq = [_mm(t_inv[i], rhs[i]) for i in range(n)]
        rhs2 = [jnp.concatenate([_block_diag_rhs(x[:, 0:LANES].astype(BF16)),
                                 _block_diag_rhs(x[:, LANES:2 * LANES].astype(BF16))], axis=1) for x in pq]
        rb = [_mm(a_rb[i], rhs2[i]) for i in range(n)]
        for i, (c, p) in enumerate(ch):
            rq_refs[d][0, rows[c], lsl[p]] = (rt[i] * e_mid[i] - rb[i][:, 0:LANES]).astype(BF16)
        y0d = [av[i][CHUNK:2 * CHUNK] - rb[i][:, LANES:2 * LANES] for i in range(n)]
        b_end = [(bt[i] * to_end[i]).astype(BF16) for i in range(n)]
        k_end = [(kt[i] * to_end[i]).astype(BF16) for i in range(n)]
        pq_b = [_mm(pq[i].T, b_end[i]) for i in range(n)]
        vk = [_mm(v_pair[i].T, k_end[i]) for i in range(n)]
        for i, (c, p) in enumerate(ch):
            ml_refs[d][0, c, p] = _compact(pq_b[i][0:LANES]).astype(BF16)
            nn_refs[d][0, c, p] = (_compact(vk[i]) - _compact(pq_b[i][LANES:2 * LANES])).astype(BF16)
        return y0d

    kd_f = prepare(0)
    y0_f = chains(0)
    kd_b = prepare(1)

    gl = shifted(3 * W + LORA_COLS, RWKV_COLS)
    gg_ref[0] = _mm(_sigmoid(gl), wg2_ref[...])
    bonus_ref[0] = _head_sum(r * (kd_f + kd_b) * rk_ref[...], bones) * v
    gate_b = z_scr[:, RWKV_COLS:RWKV_COLS + CONV_WIDTH]
    u = z_scr[:, RWKV_COLS + CONV_WIDTH:RWKV_COLS + 2 * CONV_WIDTH] * z_scr[:, RWKV_COLS + 2 * CONV_WIDTH:IN_COLS]
    up, un = neighbours(u)
    oc = gate_b * (cw_ref[0:1, :] * up + cw_ref[1:2, :] * u + cw_ref[2:3, :] * un)
    oc = oc * lax.rsqrt(jnp.mean(oc * oc, axis=-1, keepdims=True) + NORM_EPS) * cg_ref[...]
    oconv_ref[0] = oc.astype(BF16)

    y0_b = chains(1)
    for i in range(len(y0_f)):
        ci, p = divmod(i, PAIRS)
        y0_ref[0, ci * CHUNK:(ci + 1) * CHUNK, p * LANES:(p + 1) * LANES] = y0_f[i] + y0_b[i]
    wtf_ref[0] = rowf_scr[1]
    wtb_ref[0] = rowb_scr[1]


def _phase1(x, mod, roww, lw):
    bsz, seq, _ = x.shape
    nt = seq // TILE
    nc = seq // CHUNK
    mod_map = (lambda b, t: (b, 0, 0)) if mod.shape[0] == bsz else (lambda b, t: (0, 0, 0))
    tok = lambda n: pl.BlockSpec((1, TILE, n), lambda b, t: (b, t, 0))
    mn_spec = pl.BlockSpec((1, CHUNKS_PER_TILE, PAIRS, HEAD_SIZE, LANES), lambda b, t: (b, t, 0, 0, 0))
    W = RWKV_WIDTH
    consts = (lw["norm1_g"], lw["w_in"], lw["mu"], lw["w_lora"], lw["b_lora"], lw["w_g2"], lw["k_k"],
              lw["k_a"], lw["r_k"], lw["conv_w"], lw["conv_gain"], lw["bones"], lw["tri_f"], lw["tri_b"])
    tok_shape = jax.ShapeDtypeStruct((bsz, seq, W), F32)
    tok_bf16 = jax.ShapeDtypeStruct((bsz, seq, W), BF16)
    ml_shape = jax.ShapeDtypeStruct((bsz, nc, PAIRS, HEAD_SIZE, LANES), BF16)
    nn_shape = jax.ShapeDtypeStruct((bsz, nc, PAIRS, HEAD_SIZE, LANES), BF16)
    wt_spec = pl.BlockSpec((1, CHUNKS_PER_TILE * SUBLANES, W), lambda b, t: (b, t, 0))
    wt_shape = jax.ShapeDtypeStruct((bsz, nc * SUBLANES, W), F32)
    return pl.pallas_call(
        functools.partial(_phase1_body, roww),
        grid=(bsz, nt),
        in_specs=[tok(D_MODEL), pl.BlockSpec((1, 1, 6 * D_MODEL), mod_map)]
                 + [_const_spec(a.shape) for a in consts],
        out_specs=[tok(W)] * 6 + [mn_spec] * 4 + [wt_spec, wt_spec],
        out_shape=[tok_shape, tok_bf16, tok_bf16, tok_shape, tok_shape, tok_bf16,
                   ml_shape, ml_shape, nn_shape, nn_shape, wt_shape, wt_shape],
        scratch_shapes=[pltpu.VMEM((TILE, IN_COLS), F32),
                        pltpu.VMEM((4, TILE, W), F32),
                        pltpu.VMEM((4, TILE, W), F32),
                        pltpu.VMEM((TILE, W), F32),
                        pltpu.VMEM((3, CHUNKS_PER_TILE * SUBLANES, W), F32),
                        pltpu.VMEM((3, CHUNKS_PER_TILE * SUBLANES, W), F32)],
        compiler_params=pltpu.CompilerParams(dimension_semantics=("parallel", "parallel"),
                                             vmem_limit_bytes=VMEM_LIMIT),
        name="phase1_chunk_summaries",
    )(x, mod, *consts)


def _phase2_body(nb, s0f_ref, s0b_ref, mlf_ref, mlb_ref, nnf_ref, nnb_ref, wtf_ref, wtb_ref,
                 sf_ref, sb_ref, finf_ref, finb_ref, st_scr):
    step = pl.program_id(1)

    @pl.when(step == 0)
    def _():
        st_scr[0] = s0f_ref[...]
        st_scr[1] = s0b_ref[...]

    chains = [(b, d, p) for b in range(nb) for d in range(2) for p in range(PAIRS)]
    ml_refs = (mlf_ref, mlb_ref)
    nn_refs = (nnf_ref, nnb_ref)
    wt_refs = (wtf_ref, wtb_ref)
    s_refs = (sf_ref, sb_ref)
    s = [st_scr[d, b, p] for (b, d, p) in chains]
    for j in range(SCAN_CHUNKS):
        cj = (j, SCAN_CHUNKS - 1 - j)
        for i, (b, d, p) in enumerate(chains):
            s_refs[d][b, cj[d], p] = _compact(_block_diag_rhs(s[i].astype(BF16)).T)
        low = [_mm(s[i], _block_diag_rhs(ml_refs[d][b, cj[d], p])) for i, (b, d, p) in enumerate(chains)]
        for i, (b, d, p) in enumerate(chains):
            w_row = wt_refs[d][b, cj[d] * SUBLANES:cj[d] * SUBLANES + 1, p * LANES:(p + 1) * LANES]
            s[i] = s[i] * w_row - low[i] + nn_refs[d][b, cj[d], p]
    for i, (b, d, p) in enumerate(chains):
        st_scr[d, b, p] = s[i]

    @pl.when(step == pl.num_programs(1) - 1)
    def _():
        for i, (b, d, p) in enumerate(chains):
            fin_ref = finf_ref if d == 0 else finb_ref
            fin_ref[b, 2 * p] = s[i][:, 0:HEAD_SIZE]
            fin_ref[b, 2 * p + 1] = s[i][:, HEAD_SIZE:2 * HEAD_SIZE]


def _phase2(s0f, s0b, mlf, mlb, nnf, nnb, wtf, wtb):
    bsz, nc = mlf.shape[0], mlf.shape[1]
    bb = min(bsz, SCAN_BATCH)
    assert bsz % bb == 0 and nc % SCAN_CHUNKS == 0
    ns = nc // SCAN_CHUNKS
    st_blk = (bb, PAIRS, HEAD_SIZE, LANES)
    mn_blk = (bb, SCAN_CHUNKS, PAIRS, HEAD_SIZE, LANES)
    s_blk = (bb, SCAN_CHUNKS, PAIRS, HEAD_SIZE, LANES)
    wt_blk = (bb, SCAN_CHUNKS * SUBLANES, RWKV_WIDTH)
    fwd = lambda g, i: (g, i, 0, 0, 0)
    bwd = lambda g, i: (g, ns - 1 - i, 0, 0, 0)
    full = pl.BlockSpec(st_blk, lambda g, i: (g, 0, 0, 0))
    s_all = jax.ShapeDtypeStruct((bsz, nc, PAIRS, HEAD_SIZE, LANES), BF16)
    fin_spec = pl.BlockSpec((bb, RWKV_HEADS, HEAD_SIZE, HEAD_SIZE), lambda g, i: (g, 0, 0, 0))
    s_fin = jax.ShapeDtypeStruct((bsz, RWKV_HEADS, HEAD_SIZE, HEAD_SIZE), F32)
    return pl.pallas_call(
        functools.partial(_phase2_body, bb),
        grid=(bsz // bb, ns),
        in_specs=[full, full, pl.BlockSpec(mn_blk, fwd), pl.BlockSpec(mn_blk, bwd),
                  pl.BlockSpec(mn_blk, fwd), pl.BlockSpec(mn_blk, bwd),
                  pl.BlockSpec(wt_blk, lambda g, i: (g, i, 0)),
                  pl.BlockSpec(wt_blk, lambda g, i: (g, ns - 1 - i, 0))],
        out_specs=[pl.BlockSpec(s_blk, fwd), pl.BlockSpec(s_blk, bwd), fin_spec, fin_spec],
        out_shape=[s_all, s_all, s_fin, s_fin],
        scratch_shapes=[pltpu.VMEM((2,) + st_blk, F32)],
        compiler_params=pltpu.CompilerParams(dimension_semantics=("parallel", "arbitrary"),
                                             vmem_limit_bytes=VMEM_LIMIT),
        name="phase2_state_scan",
    )(s0f, s0b, mlf, mlb, nnf, nnb, wtf, wtb)


def _phase3_body(x_ref, mod_ref, y0_ref, rqf_ref, rqb_ref, bonus_ref, gg_ref, oconv_ref, sf_ref, sb_ref,
                 lnw_ref, lnb_ref, bones_ref, wout_ref, g2_ref, wff1_ref, wff2_ref, gfin_ref,
                 o_ref, y_scr):
    W = RWKV_WIDTH
    mod = mod_ref[0]
    gate_a = mod[:, 2 * D_MODEL:3 * D_MODEL]
    shift_f = mod[:, 3 * D_MODEL:4 * D_MODEL]
    scale_f = mod[:, 4 * D_MODEL:5 * D_MODEL]
    gate_f = mod[:, 5 * D_MODEL:6 * D_MODEL]
    inv_n = 1.0 / HEAD_SIZE
    carried = {}

    def mix(blk):
        r0 = blk * P3_ROWS
        rsl = slice(r0, r0 + P3_ROWS)
        for ci in range(P3_ROWS // CHUNK):
            cg = blk * (P3_ROWS // CHUNK) + ci
            rows = slice(cg * CHUNK, (cg + 1) * CHUNK)
            for p in range(PAIRS):
                ls = slice(p * LANES, (p + 1) * LANES)
                y_scr[rows, ls] = y0_ref[rows, ls] + _dot(
                    jnp.concatenate([rqf_ref[rows, ls], rqb_ref[rows, ls]], axis=1),
                    jnp.concatenate([_block_diag_rhs(sf_ref[cg, p]), _block_diag_rhs(sb_ref[cg, p])], axis=0))
        yield
        y = y_scr[rsl, :]
        yc = y - _head_sum(y, bones_ref[...]) * inv_n
        yield
        y_var = _head_sum(yc * yc, bones_ref[...]) * inv_n
        yn = yc * lax.rsqrt(y_var + GN_EPS) * lnw_ref[...] + lnb_ref[...]
        o_rwkv = (yn + bonus_ref[rsl, :]) * gg_ref[rsl, :]
        yield
        mixed = (_dot(o_rwkv.astype(BF16), wout_ref[0:W, :])
                 + _dot(oconv_ref[rsl, :], wout_ref[W:D_MODEL, :]))
        x1 = x_ref[rsl, :] + gate_a * mixed
        ms = jnp.mean(x1 * x1, axis=-1, keepdims=True)
        h2 = (x1 * lax.rsqrt(ms + NORM_EPS)) * g2_ref[...] * (1.0 + scale_f) + shift_f
        carried[blk] = (x1, h2.astype(BF16))

    def mlp(blk):
        rsl = slice(blk * P3_ROWS, (blk + 1) * P3_ROWS)
        x1, h2 = carried.pop(blk)
        acc = None
        for j in range(D_FF // FF_CHUNK):
            cols = slice(j * FF_CHUNK, (j + 1) * FF_CHUNK)
            f1 = jnp.maximum(_dot(h2, wff1_ref[:, cols]), 0.0)
            f2 = _dot((f1 * f1).astype(BF16), wff2_ref[cols, :])
            acc = f2 if acc is None else acc + f2
            yield
        x2 = x1 + gate_f * acc
        ms2 = jnp.mean(x2 * x2, axis=-1, keepdims=True)
        o_ref[rsl, :] = (x2 * lax.rsqrt(ms2 + NORM_EPS)) * gfin_ref[...]

    def interleave(*gens):
        live = list(gens)
        while live:
            for g in list(live):
                try:
                    next(g)
                except StopIteration:
                    live.remove(g)

    n_blk = P3_TILE // P3_ROWS
    mixes = [mix(blk) for blk in range(n_blk)]
    interleave(*mixes)
    for blk in range(n_blk):
        interleave(mlp(blk))


def _phase3(x, mod, p1, sf, sb, lw, final_g):
    bsz, seq, _ = x.shape
    W = RWKV_WIDTH
    n_tok = bsz * seq
    nt = n_tok // P3_TILE
    assert n_tok % P3_TILE == 0 and (seq % P3_TILE == 0 or mod.shape[0] == 1)
    tiles_per_batch = max(seq // P3_TILE, 1)
    mod_map = (lambda t: (t // tiles_per_batch, 0, 0)) if mod.shape[0] == bsz else (lambda t: (0, 0, 0))
    tok = lambda n: pl.BlockSpec((P3_TILE, n), lambda t: (t, 0))
    cpt = P3_TILE // CHUNK
    s_spec = pl.BlockSpec((cpt, PAIRS, HEAD_SIZE, LANES), lambda t: (t, 0, 0, 0))
    flat = lambda a: a.reshape((n_tok,) + a.shape[2:])
    flat_s = lambda a: a.reshape((a.shape[0] * a.shape[1],) + a.shape[2:])
    consts = (lw["ln_x_w"], lw["ln_x_b"], lw["bones"], lw["w_out"], lw["norm2_g"], lw["w_ff1"], lw["w_ff2"],
              final_g)
    out = pl.pallas_call(
        _phase3_body,
        grid=(nt,),
        in_specs=[tok(D_MODEL), pl.BlockSpec((1, 1, 6 * D_MODEL), mod_map)] + [tok(W)] * 6 + [s_spec, s_spec]
                 + [_const_spec(a.shape) for a in consts],
        out_specs=tok(D_MODEL),
        out_shape=jax.ShapeDtypeStruct((n_tok, D_MODEL), F32),
        scratch_shapes=[pltpu.VMEM((P3_TILE, W), F32)],
        compiler_params=pltpu.CompilerParams(dimension_semantics=("parallel",),
                                             vmem_limit_bytes=VMEM_LIMIT),
        name="phase3_mix_mlp",
    )(flat(x), mod, *[flat(a) for a in p1], flat_s(sf), flat_s(sb), *consts)
    return out.reshape(bsz, seq, D_MODEL)


def _block_diag(blocks):
    rows = sum(b.shape[0] for b in blocks)
    cols = sum(b.shape[1] for b in blocks)
    out = jnp.zeros((rows, cols), blocks[0].dtype)
    r = c = 0
    for b in blocks:
        out = lax.dynamic_update_slice(out, b, (r, c))
        r += b.shape[0]
        c += b.shape[1]
    return out


def _tri_constants():
    t = jnp.arange(TILE)
    same = (t[:, None] // CHUNK) == (t[None, :] // CHUNK)
    tri_f = (same & (t[None, :] <= t[:, None])).astype(BF16)
    tri_b = (same & (t[None, :] >= t[:, None])).astype(BF16)
    lane = jnp.arange(RWKV_WIDTH)
    bones = ((lane[:, None] // HEAD_SIZE) == (lane[None, :] // HEAD_SIZE)).astype(BF16)
    return tri_f, tri_b, bones


def _pair_compact(s):
    b = s.shape[0]
    s = s.reshape(b, PAIRS, 2, HEAD_SIZE, HEAD_SIZE)
    return jnp.swapaxes(s, 2, 3).reshape(b, PAIRS, HEAD_SIZE, LANES)


def kernel(x_prompt, x_sample, c, state_rwkv_fwd, state_rwkv_bwd, c_ctx, w_mod, b_mod, norm1_g, w_in, mu_shift, w0_f, w_w2_f, w0_b, w_w2_b, a0_f, w_a2_f, a0_b, w_a2_b, w_g2, k_k, k_a, r_k, ln_x_w, ln_x_b, conv_w, conv_gain, w_out, norm2_g, w_ff1, w_ff2, final_g):
    depth = w_in.shape[0]
    b_ctx, seq_ctx, _ = x_prompt.shape
    b_lat, seq_lat, _ = x_sample.shape
    assert depth == 1
    assert seq_ctx == TILE and seq_lat % TILE == 0 and TILE % GRID_W == 0
    tri_f, tri_b, bones = _tri_constants()
    row = lambda a: a.reshape(1, -1)
    n_pad = (-(b_lat + 1)) % SUBLANES
    cvec = jnp.concatenate([c, c_ctx[None, :], jnp.zeros((n_pad, D_MODEL), F32)], axis=0)
    zero_state = jnp.zeros((b_ctx, PAIRS, HEAD_SIZE, LANES), F32)

    i = 0
    lw = dict(
        norm1_g=row(norm1_g[i]), w_in=w_in[i].astype(BF16), mu=row(mu_shift[i]),
        w_lora=_block_diag([w_w2_f[i], w_w2_b[i], w_a2_f[i], w_a2_b[i]]),
        b_lora=jnp.concatenate([w0_f[i], w0_b[i], a0_f[i], a0_b[i]]).reshape(1, -1),
        w_g2=w_g2[i], k_k=row(k_k[i]), k_a=row(k_a[i]), r_k=row(r_k[i]),
        conv_w=conv_w[i], conv_gain=row(conv_gain[i]), bones=bones, tri_f=tri_f, tri_b=tri_b,
        ln_x_w=row(ln_x_w[i]), ln_x_b=row(ln_x_b[i]), w_out=w_out[i].astype(BF16),
        norm2_g=row(norm2_g[i]), w_ff1=w_ff1[i].astype(BF16), w_ff2=w_ff2[i].astype(BF16))
    mod = _modulation(cvec, w_mod[i], row(b_mod[i]))
    mod_lat = mod[:b_lat].reshape(b_lat, 1, 6 * D_MODEL)
    mod_ctx = mod[b_lat:b_lat + 1].reshape(1, 1, 6 * D_MODEL)

    def run(x, mod_s, roww, s0f, s0b):
        *p1, mlf, mlb, nnf, nnb, wtf, wtb = _phase1(x, mod_s, roww, lw)
        sf, sb, fin_f, fin_b = _phase2(s0f, s0b, mlf, mlb, nnf, nnb, wtf, wtb)
        return _phase3(x, mod_s, p1, sf, sb, lw, row(final_g)), fin_f, fin_b

    y_ctx, s_f, s_b = run(x_prompt, mod_ctx, seq_ctx, zero_state, zero_state)
    y_lat, _, _ = run(x_sample, mod_lat, GRID_W, _pair_compact(state_rwkv_fwd[:, i]),
                      _pair_compact(state_rwkv_bwd[:, i]))
    return (y_ctx, y_lat, s_f[:, None], s_b[:, None])
```
